```python
import math
import jax, jax.numpy as jnp
from jax import lax
import numpy as np

D_MODEL = 1024
BATCH = 4
SEQ = 4096
DEPTH = 2

CTX_LEN = 256
GRID_W = 64
EPS = 1e-6

A_CHUNK = 128
A_GROUPS = 8
A_WIDTH = D_MODEL
A_GROUP_DIM = A_WIDTH // A_GROUPS
ROWS_PER_CHUNK = A_CHUNK // GRID_W

B_HEADS = 8
B_HEAD_DIM = 128
B_WIDTH = B_HEADS * B_HEAD_DIM
B_CONV = 5
B_CHUNK = 64

N_EXPERTS = 16
N_GROUPS = 4
EXPERTS_PER_GROUP = N_EXPERTS // N_GROUPS
TOP_K = 2
D_EXPERT = 512

IN_COLS = 2 * A_WIDTH + 4 * B_WIDTH + 4 * B_HEADS + 2 * D_MODEL
IN_SPLITS = (A_WIDTH, 2 * A_WIDTH, 2 * A_WIDTH + 3 * B_WIDTH, 2 * A_WIDTH + 4 * B_WIDTH,
             2 * A_WIDTH + 4 * B_WIDTH + 4 * B_HEADS, 2 * A_WIDTH + 4 * B_WIDTH + 4 * B_HEADS + D_MODEL)

kernel_name = 'hybrid_sgu_gdn_moe_dit_block'


def rmsnorm(x, gain):
    xf = x.astype(jnp.float32)
    xf = xf * lax.rsqrt(jnp.mean(jnp.square(xf), axis=-1, keepdims=True) + EPS)
    return (xf * gain.astype(jnp.float32)).astype(x.dtype)


def layernorm(x, gain):
    xf = x.astype(jnp.float32)
    mu = jnp.mean(xf, axis=-1, keepdims=True)
    var = jnp.mean(jnp.square(xf - mu), axis=-1, keepdims=True)
    return ((xf - mu) * lax.rsqrt(var + EPS) * gain.astype(jnp.float32)).astype(x.dtype)


def l2norm(x):
    xf = x.astype(jnp.float32)
    return xf * lax.rsqrt(jnp.sum(jnp.square(xf), axis=-1, keepdims=True) + EPS)


def ada_params(cond, w_ada, b_ada):
    return jnp.split(jax.nn.silu(cond) @ w_ada + b_ada, 6, axis=-1)


def centred_conv(x, w):
    pad = B_CONV // 2
    return lax.conv_general_dilated(
        x, w[:, None, :].astype(x.dtype), window_strides=(1,), padding=[(pad, pad)],
        dimension_numbers=('NWC', 'WIO', 'NWC'), feature_group_count=x.shape[-1])


def spatial_gating(u, v, n_chunks, ln_gain, w_s, b_s):
    u = jax.nn.gelu(u)
    v = layernorm(jax.nn.gelu(v), ln_gain)
    bn = v.shape[0]
    vc = v.reshape(bn, n_chunks, A_CHUNK, A_GROUPS, A_GROUP_DIM)
    mixed = jnp.einsum('gij,bnjgd->bnigd', w_s, vc) + b_s.T[None, None, :, :, None]
    return u * mixed.reshape(u.shape)


def gdn_prep(qkv, ab, conv_w, a_log, dt_bias):
    bn, t = qkv.shape[0], qkv.shape[1]
    qkv = jax.nn.silu(centred_conv(qkv, conv_w))
    q, k, v = jnp.split(qkv, 3, axis=-1)
    heads = lambda y: y.reshape(bn, t, B_HEADS, B_HEAD_DIM).transpose(0, 2, 1, 3)
    q = l2norm(heads(q)) * (B_HEAD_DIM ** -0.5)
    k = l2norm(heads(k))
    v = heads(v).astype(jnp.float32)
    a_f, b_f, a_b, b_b = ab.astype(jnp.float32).reshape(bn, t, 4, B_HEADS).transpose(2, 0, 3, 1)
    a_log = a_log.astype(jnp.float32)
    dt_bias = dt_bias.astype(jnp.float32)
    g_f = -jnp.exp(a_log[0])[:, None] * jax.nn.softplus(a_f + dt_bias[0][:, None])
    g_b = -jnp.exp(a_log[1])[:, None] * jax.nn.softplus(a_b + dt_bias[1][:, None])
    return q, k, v, (g_f, jax.nn.sigmoid(b_f)), (g_b, jax.nn.sigmoid(b_b))


def gated_delta_chunked(q, k, v, g, beta, s0):
    bn, h, t, dk = q.shape
    dv = v.shape[-1]
    c = B_CHUNK
    n = t // c
    f32 = jnp.float32
    q = q.astype(f32).reshape(bn, h, n, c, dk)
    k = k.astype(f32).reshape(bn, h, n, c, dk)
    v = v.astype(f32).reshape(bn, h, n, c, dv)
    beta = beta.astype(f32).reshape(bn, h, n, c, 1)
    gc = jnp.cumsum(g.astype(f32).reshape(bn, h, n, c), axis=-1)
    incl = jnp.tril(jnp.ones((c, c), dtype=bool))
    strict = jnp.tril(jnp.ones((c, c), dtype=bool), -1)
    decay = jnp.exp(jnp.where(incl, gc[..., :, None] - gc[..., None, :], -jnp.inf))
    k_beta = k * beta
    m = jnp.where(strict, jnp.einsum('bhnik,bhnjk->bhnij', k_beta, k) * decay, 0.0)
    eye = jnp.eye(c, dtype=f32)
    t_inv = lax.linalg.triangular_solve(eye + m, jnp.broadcast_to(eye, m.shape),
                                        left_side=True, lower=True, unit_diagonal=True)
    u_val = t_inv @ (v * beta)
    w_key = t_inv @ (k_beta * jnp.exp(gc)[..., None])
    intra = jnp.einsum('bhnik,bhnjk->bhnij', q, k) * decay
    q_dec = q * jnp.exp(gc)[..., None]
    k_dec = k * jnp.exp(gc[..., -1:] - gc)[..., None]
    g_tot = jnp.exp(gc[..., -1])[..., None, None]
    xs = tuple(jnp.moveaxis(y, 2, 0) for y in (intra, u_val, w_key, q_dec, k_dec, g_tot))

    def step(s, inp):
        a_n, u_n, w_n, qd_n, kd_n, gt_n = inp
        v_new = u_n - w_n @ s
        o_n = qd_n @ s + a_n @ v_new
        s = s * gt_n + jnp.einsum('bhck,bhcv->bhkv', kd_n, v_new)
        return s, o_n

    s_final, o = lax.scan(step, s0, xs)
    return jnp.moveaxis(o, 0, 2).reshape(bn, h, t, dv), s_final


def gated_delta_reverse(q, k, v, g, beta, s0):
    o, s = gated_delta_chunked(jnp.flip(q, 2), jnp.flip(k, 2), jnp.flip(v, 2),
                               jnp.flip(g, -1), jnp.flip(beta, -1), s0)
    return jnp.flip(o, 2), s


def gdn_output(o, z, gain):
    bn, h, t, dv = o.shape
    o = o.transpose(0, 2, 1, 3)
    o = o * lax.rsqrt(jnp.mean(jnp.square(o), axis=-1, keepdims=True) + EPS) * gain.astype(jnp.float32)
    o = o * jax.nn.silu(z.astype(jnp.float32).reshape(bn, t, h, dv))
    return o.reshape(bn, t, h * dv).astype(z.dtype)


def merge_branches(u, v, n_chunks, o_gdn, z, gate_a, gate_b, a_ln_g, a_w_s, a_b_s, b_norm_g, w_pa, w_pb, w_out):
    y_a = spatial_gating(u, v, n_chunks, a_ln_g, a_w_s, a_b_s) @ w_pa
    y_b = gdn_output(o_gdn, z, b_norm_g) @ w_pb
    return (jax.nn.sigmoid(gate_a) * y_a + jax.nn.sigmoid(gate_b) * y_b) @ w_out


def token_mixer(h_l, h_c, n_chunks_l, n_chunks_c, with_ctx, w_in, a_ln_g, a_w_s, a_b_s,
                conv_w, a_log, dt_bias, b_norm_g, w_pa, w_pb, w_out):
    u_l, v_l, qkv_l, z_l, ab_l, ga_l, gb_l = jnp.split(h_l @ w_in, IN_SPLITS, axis=-1)
    u_c, v_c, qkv_c, z_c, ab_c, ga_c, gb_c = jnp.split(h_c @ w_in, IN_SPLITS, axis=-1)
    qc, kc, vc, fwd_c, bwd_c = gdn_prep(qkv_c, ab_c, conv_w, a_log, dt_bias)
    ql, kl, vl, fwd_l, bwd_l = gdn_prep(qkv_l, ab_l, conv_w, a_log, dt_bias)
    bn = h_l.shape[0]
    s0 = jnp.zeros((bn, B_HEADS, B_HEAD_DIM, B_HEAD_DIM), jnp.float32)
    oc_f, s_f = gated_delta_chunked(qc, kc, vc, fwd_c[0], fwd_c[1], s0)
    oc_b, s_b = gated_delta_reverse(qc, kc, vc, bwd_c[0], bwd_c[1], s0)
    ol_f, _ = gated_delta_chunked(ql, kl, vl, fwd_l[0], fwd_l[1], s_f)
    ol_b, _ = gated_delta_reverse(ql, kl, vl, bwd_l[0], bwd_l[1], s_b)
    y_l = merge_branches(u_l, v_l, n_chunks_l, ol_f + ol_b, z_l, ga_l, gb_l,
                         a_ln_g, a_w_s, a_b_s, b_norm_g, w_pa, w_pb, w_out)
    y_c = None
    if with_ctx:
        y_c = merge_branches(u_c, v_c, n_chunks_c, oc_f + oc_b, z_c, ga_c, gb_c,
                             a_ln_g, a_w_s, a_b_s, b_norm_g, w_pa, w_pb, w_out)
    return y_l, y_c


def routed_moe(h, w_router, b_router, w1, w3, w2):
    scores = jax.nn.sigmoid(h.astype(jnp.float32) @ w_router.astype(jnp.float32))
    sel = scores + b_router.astype(jnp.float32)
    grp_score = jnp.sum(lax.top_k(sel.reshape(-1, N_GROUPS, EXPERTS_PER_GROUP), TOP_K)[0], axis=-1)
    best = jnp.argmax(grp_score, axis=-1)
    in_group = (jnp.arange(N_EXPERTS)[None, :] // EXPERTS_PER_GROUP) == best[:, None]
    _, idx = lax.top_k(jnp.where(in_group, sel, -jnp.inf), TOP_K)
    wts = jnp.take_along_axis(scores, idx, axis=-1)
    wts = wts / jnp.sum(wts, axis=-1, keepdims=True)
    combine = jnp.sum(jax.nn.one_hot(idx, N_EXPERTS, dtype=jnp.float32) * wts[..., None], axis=1).astype(h.dtype)
    y = jnp.zeros_like(h)
    for e in range(N_EXPERTS):
        y_e = (jax.nn.silu(h @ w1[e]) * (h @ w3[e])) @ w2[e]
        y = y + combine[:, e:e + 1] * y_e
    return y


def setup_inputs(seed: int = 0) -> dict:
    key = jax.random.key(seed)
    ks = jax.random.split(key, 26)
    nrm = lambda k, shape, scale: scale * jax.random.normal(k, shape, jnp.float32)
    L = DEPTH
    a_log = jnp.log(jax.random.uniform(ks[12], (L, 2, B_HEADS), jnp.float32, 1.0, 16.0))
    dt = jnp.exp(jax.random.uniform(ks[13], (L, 2, B_HEADS), jnp.float32, math.log(1e-3), math.log(1e-1)))
    dt_bias = dt + jnp.log(-jnp.expm1(-dt))
    return {
        'x': nrm(ks[0], (BATCH, SEQ, D_MODEL), 1.0),
        'c': nrm(ks[1], (BATCH, D_MODEL), 1.0),
        'ctx': nrm(ks[2], (BATCH, CTX_LEN, D_MODEL), 1.0),
        'c_ctx': nrm(ks[3], (D_MODEL,), 1.0),
        'w_ada': nrm(ks[4], (L, D_MODEL, 6 * D_MODEL), 0.5 * D_MODEL ** -0.5),
        'b_ada': nrm(ks[5], (L, 6 * D_MODEL), 0.02),
        'norm1_g': 1.0 + nrm(ks[6], (L, D_MODEL), 0.02),
        'norm2_g': 1.0 + nrm(ks[7], (L, D_MODEL), 0.02),
        'w_in': nrm(ks[8], (L, D_MODEL, IN_COLS), D_MODEL ** -0.5),
        'a_ln_g': 1.0 + nrm(ks[9], (L, A_WIDTH), 0.02),
        'a_w_s': nrm(ks[10], (L, A_GROUPS, A_CHUNK, A_CHUNK), A_CHUNK ** -0.5),
        'a_b_s': 1.0 + nrm(ks[11], (L, A_GROUPS, A_CHUNK), 0.02),
        'b_conv_w': nrm(ks[14], (L, B_CONV, 3 * B_WIDTH), B_CONV ** -0.5),
        'b_A_log': a_log,
        'b_dt_bias': dt_bias,
        'b_norm_g': 1.0 + nrm(ks[15], (L, B_HEAD_DIM), 0.02),
        'w_proj_a': nrm(ks[16], (L, A_WIDTH, D_MODEL), A_WIDTH ** -0.5),
        'w_proj_b': nrm(ks[17], (L, B_WIDTH, D_MODEL), B_WIDTH ** -0.5),
        'w_out': nrm(ks[18], (L, D_MODEL, D_MODEL), D_MODEL ** -0.5),
        'w_router': nrm(ks[19], (D_MODEL, N_EXPERTS), D_MODEL ** -0.5),
        'b_router': nrm(ks[20], (N_EXPERTS,), 0.01),
        'w_e1': nrm(ks[21], (L, N_EXPERTS, D_MODEL, D_EXPERT), D_MODEL ** -0.5),
        'w_e3': nrm(ks[22], (L, N_EXPERTS, D_MODEL, D_EXPERT), D_MODEL ** -0.5),
        'w_e2': nrm(ks[23], (L, N_EXPERTS, D_EXPERT, D_MODEL), D_EXPERT ** -0.5),
        'final_g': 1.0 + nrm(ks[24], (D_MODEL,), 0.02),
    }


def reference(x, c, ctx, c_ctx, w_ada, b_ada, norm1_g, norm2_g, w_in, a_ln_g, a_w_s, a_b_s,
              b_conv_w, b_A_log, b_dt_bias, b_norm_g, w_proj_a, w_proj_b, w_out,
              w_router, b_router, w_e1, w_e3, w_e2, final_g):
    bn, t, d = x.shape
    rows = t // GRID_W
    n_chunks_l = rows // ROWS_PER_CHUNK
    n_chunks_c = ctx.shape[1] // A_CHUNK
    x_l, x_c = x, ctx
    for l in range(DEPTH):
        with_ctx = l < DEPTH - 1
        sh1_l, sc1_l, g1_l, sh2_l, sc2_l, g2_l = [m[:, None, :] for m in ada_params(c, w_ada[l], b_ada[l])]
        sh1_c, sc1_c, g1_c, sh2_c, sc2_c, g2_c = ada_params(c_ctx, w_ada[l], b_ada[l])
        h_l = rmsnorm(x_l, norm1_g[l]) * (1.0 + sc1_l) + sh1_l
        h_c = rmsnorm(x_c, norm1_g[l]) * (1.0 + sc1_c) + sh1_c
        y_l, y_c = token_mixer(h_l, h_c, n_chunks_l, n_chunks_c, with_ctx, w_in[l], a_ln_g[l], a_w_s[l], a_b_s[l],
                               b_conv_w[l], b_A_log[l], b_dt_bias[l], b_norm_g[l], w_proj_a[l], w_proj_b[l], w_out[l])
        x_l = x_l + g1_l * y_l
        h2_l = rmsnorm(x_l, norm2_g[l]) * (1.0 + sc2_l) + sh2_l
        if with_ctx:
            x_c = x_c + g1_c * y_c
            h2_c = rmsnorm(x_c, norm2_g[l]) * (1.0 + sc2_c) + sh2_c
            n_ctx_tok = h2_c.shape[0] * h2_c.shape[1]
            flat = jnp.concatenate([h2_c.reshape(-1, d), h2_l.reshape(-1, d)], axis=0)
            y2 = routed_moe(flat, w_router, b_router, w_e1[l], w_e3[l], w_e2[l])
            x_c = x_c + g2_c * y2[:n_ctx_tok].reshape(h2_c.shape)
            x_l = x_l + g2_l * y2[n_ctx_tok:].reshape(h2_l.shape)
        else:
            y2 = routed_moe(h2_l.reshape(-1, d), w_router, b_router, w_e1[l], w_e3[l], w_e2[l])
            x_l = x_l + g2_l * y2.reshape(h2_l.shape)
    return rmsnorm(x_l, final_g)
```

```python
import functools

import jax
import jax.numpy as jnp
from jax import lax
from jax.experimental import pallas as pl
from jax.experimental.pallas import tpu as pltpu

F32 = jnp.float32
BF16 = jnp.bfloat16
HIGHEST = lax.Precision.HIGHEST

EPS = 1e-6
D_MODEL = 1024
N_HEADS = 8
HEAD_DIM = 128
SGU_CHUNK = 128
SGU_GROUPS = 8
CONV_TAPS = 5
N_EXPERTS = 16
EXPERTS_PER_GROUP = 4
N_EXPERT_GROUPS = 4
D_EXPERT = 512
ROW_TILE = 256
GDN_CHUNK = 128
MOE_SUB = 4
COND_ROWS = 8
HALO = 16
VMEM_LIMIT = 56 * 1024 * 1024


def _cparams(sem):
    return pltpu.CompilerParams(dimension_semantics=sem, vmem_limit_bytes=VMEM_LIMIT)


def _sigmoid(x):
    return 1.0 / (1.0 + jnp.exp(-x))


def _silu(x):
    return x * _sigmoid(x)


def _gelu_tanh(x):
    return 0.5 * x * (1.0 + jnp.tanh(0.7978845608028654 * (x + 0.044715 * (x * x * x))))


def _softplus(x):
    return jnp.maximum(x, 0.0) + jnp.log1p(jnp.exp(-jnp.abs(x)))


def _ada_kernel(cond_ref, w_ref, b_ref, o_ref):
    s = _silu(cond_ref[...])
    o_ref[0, 0] = jnp.dot(s, w_ref[0], precision=HIGHEST, preferred_element_type=F32) + b_ref[0, 0]


def _ada_call(cond, w_ada, b_ada):
    n_layers = w_ada.shape[0]
    d = D_MODEL
    return pl.pallas_call(
        _ada_kernel,
        grid=(n_layers, 6),
        in_specs=[
            pl.BlockSpec((COND_ROWS, d), lambda l, j: (0, 0)),
            pl.BlockSpec((1, d, d), lambda l, j: (l, 0, j)),
            pl.BlockSpec((1, 1, 1, d), lambda l, j: (l, j, 0, 0)),
        ],
        out_specs=pl.BlockSpec((1, 1, COND_ROWS, d), lambda l, j: (l, j, 0, 0)),
        out_shape=jax.ShapeDtypeStruct((n_layers, 6, COND_ROWS, d), F32),
        compiler_params=_cparams(("arbitrary", "arbitrary")),
        name="ada_params",
    )(cond, w_ada, b_ada.reshape(n_layers, 6, 1, d))


def _in_kernel(x_ref, sh_ref, sc_ref, g_ref, w_ref, wab_ref,
               u_ref, v_ref, q_ref, k_ref, vv_ref, z_ref, ga_ref, gb_ref, ab_ref):
    x = x_ref[...]
    ms = jnp.mean(x * x, axis=-1, keepdims=True)
    h = x * lax.rsqrt(ms + EPS) * g_ref[...]
    h = h * (1.0 + sc_ref[0]) + sh_ref[0]
    hb = h.astype(BF16)
    d = D_MODEL
    outs = (u_ref, v_ref, q_ref, k_ref, vv_ref, z_ref, ga_ref, gb_ref)
    for n, o_ref in enumerate(outs):
        o_ref[...] = jnp.dot(hb, w_ref[:, n * d:(n + 1) * d],
                             preferred_element_type=F32).astype(BF16)
    ab_ref[...] = jnp.dot(h, wab_ref[...], precision=HIGHEST, preferred_element_type=F32)


def _in_call(x, mods, layer, norm_g, w_main, w_ab, mod_row):
    rows, d = x.shape
    nt = rows // ROW_TILE
    sh_idx = (layer * 6 + 0) * COND_ROWS
    sc_idx = (layer * 6 + 1) * COND_ROWS
    row_spec = pl.BlockSpec((ROW_TILE, d), lambda i: (i, 0))
    out_shapes = [jax.ShapeDtypeStruct((rows, d), BF16)] * 8 + [jax.ShapeDtypeStruct((rows, 128), F32)]
    return pl.pallas_call(
        _in_kernel,
        grid=(nt,),
        in_specs=[
            row_spec,
            pl.BlockSpec((1, 1, d), lambda i: (sh_idx + mod_row(i), 0, 0)),
            pl.BlockSpec((1, 1, d), lambda i: (sc_idx + mod_row(i), 0, 0)),
            pl.BlockSpec((1, d), lambda i: (0, 0)),
            pl.BlockSpec((d, 8 * d), lambda i: (0, 0), pipeline_mode=pl.Buffered(1)),
            pl.BlockSpec((d, 128), lambda i: (0, 0)),
        ],
        out_specs=[row_spec] * 8 + [pl.BlockSpec((ROW_TILE, 128), lambda i: (i, 0))],
        out_shape=out_shapes,
        compiler_params=_cparams(("arbitrary",)),
        name="norm_in_proj",
    )(x, mods, mods, norm_g, w_main, w_ab)


def _conv_kernel(tiles_per_batch, ctx_tiles,
                 q_ref, qp_ref, qn_ref, k_ref, kp_ref, kn_ref, v_ref, vp_ref, vn_ref,
                 cw_ref, ab_ref, alog_ref, dtb_ref,
                 qo_ref, ko_ref, vo_ref, gate_ref, ext_ref):
    j = pl.program_id(0) % tiles_per_batch
    has_prev = jnp.logical_and(j != 0, j != ctx_tiles).astype(F32)
    has_next = jnp.logical_and(j != ctx_tiles - 1, j != tiles_per_batch - 1).astype(F32)
    pad = CONV_TAPS // 2
    srcs = ((q_ref, qp_ref, qn_ref, qo_ref, True, HEAD_DIM ** -0.5),
            (k_ref, kp_ref, kn_ref, ko_ref, True, 1.0),
            (v_ref, vp_ref, vn_ref, vo_ref, False, 1.0))
    for t, (m_ref, p_ref, n_ref, o_ref, l2, scale) in enumerate(srcs):
        ext_ref[0:8, :] = p_ref[HALO - 8:HALO, :].astype(F32) * has_prev
        ext_ref[8:8 + ROW_TILE, :] = m_ref[...].astype(F32)
        ext_ref[8 + ROW_TILE:16 + ROW_TILE, :] = n_ref[0:8, :].astype(F32) * has_next
        for h in range(N_HEADS):
            lanes = slice(h * HEAD_DIM, (h + 1) * HEAD_DIM)
            acc = None
            for tap in range(CONV_TAPS):
                w_row = cw_ref[tap:tap + 1, t * D_MODEL + h * HEAD_DIM:t * D_MODEL + (h + 1) * HEAD_DIM]
                start = 8 - pad + tap
                term = ext_ref[start:start + ROW_TILE, lanes] * w_row
                acc = term if acc is None else acc + term
            y = _silu(acc)
            if l2:
                ss = jnp.sum(y * y, axis=-1, keepdims=True)
                y = y * (lax.rsqrt(ss + EPS) * scale)
            o_ref[:, lanes] = y.astype(BF16)
    ab = ab_ref[...]
    lane = lax.broadcasted_iota(jnp.int32, ab.shape, 1)
    is_decay = ((lane // N_HEADS) % 2) == 0
    g = -jnp.exp(alog_ref[...]) * _softplus(ab + dtb_ref[...])
    gate = jnp.where(is_decay, g, _sigmoid(ab))
    gate = jnp.where(lane < 4 * N_HEADS, gate, 0.0)
    gate_ref[0] = gate
    gate_ref[1] = pltpu.roll(gate, 128 - 2 * N_HEADS, axis=1)


def _conv_call(q, k, v, conv_w, ab, alog_row, dtb_row, tiles_per_batch, ctx_tiles):
    rows, d = q.shape
    nt = rows // ROW_TILE
    per = ROW_TILE // HALO
    n_halo = rows // HALO
    main = pl.BlockSpec((ROW_TILE, d), lambda i: (i, 0))
    prev = pl.BlockSpec((HALO, d), lambda i: (jnp.maximum(i * per - 1, 0), 0))
    nxt = pl.BlockSpec((HALO, d), lambda i: (jnp.minimum((i + 1) * per, n_halo - 1), 0))
    small = pl.BlockSpec((1, 128), lambda i: (0, 0))
    return pl.pallas_call(
        functools.partial(_conv_kernel, tiles_per_batch, ctx_tiles),
        grid=(nt,),
        in_specs=[main, prev, nxt] * 3 + [
            pl.BlockSpec((8, 3 * d), lambda i: (0, 0)),
            pl.BlockSpec((ROW_TILE, 128), lambda i: (i, 0)),
            small, small,
        ],
        out_specs=[main, main, main, pl.BlockSpec((2, ROW_TILE, 128), lambda i: (0, i, 0))],
        out_shape=[jax.ShapeDtypeStruct((rows, d), BF16)] * 3 + [jax.ShapeDtypeStruct((2, rows, 128), F32)],
        scratch_shapes=[pltpu.VMEM((ROW_TILE + 16, d), F32)],
        compiler_params=_cparams(("arbitrary",)),
        name="conv_silu_gates",
    )(q, q, q, k, k, k, v, v, v, conv_w, ab, alog_row, dtb_row)


def _gdn_kernel(q_ref, k_ref, v_ref, gate_ref, o_ref, s_ref):
    c = GDN_CHUNK
    fwd = pl.program_id(1) == 0

    @pl.when(pl.program_id(2) == 0)
    def _():
        s_ref[...] = jnp.zeros_like(s_ref)

    row = lax.broadcasted_iota(jnp.int32, (c, c), 0)
    col = lax.broadcasted_iota(jnp.int32, (c, c), 1)
    later = jnp.where(fwd, row, col)
    earlier = jnp.where(fwd, col, row)
    incl = later >= earlier
    strict = later > earlier
    eye = (row == col).astype(F32)

    gate = gate_ref[0]
    gc_all = jnp.dot(incl.astype(F32), gate, precision=HIGHEST, preferred_element_type=F32)
    gtot_all = jnp.dot(jnp.ones((c, c), F32), gate, precision=HIGHEST, preferred_element_type=F32)
    gate_t = gate.T
    gc_t = gc_all.T
    gtot_t = gtot_all.T

    for h in range(N_HEADS):
        lanes = slice(h * HEAD_DIM, (h + 1) * HEAD_DIM)
        qh = q_ref[:, lanes]
        kh = k_ref[:, lanes]
        vh = v_ref[:, lanes]
        gc_col = jnp.broadcast_to(gc_all[:, h:h + 1], (c, c))
        gc_row = gc_t[h:h + 1, :]
        gtot_row = gtot_t[h:h + 1, :]
        beta_row = gate_t[N_HEADS + h:N_HEADS + h + 1, :]

        decay = jnp.exp(jnp.where(incl, gc_col - gc_row, -jnp.inf))
        kk = lax.dot_general(kh, kh, (((1,), (1,)), ((), ())), preferred_element_type=F32)
        qk = lax.dot_general(qh, kh, (((1,), (1,)), ((), ())), preferred_element_type=F32)
        n_mat = jnp.where(strict, kk * decay, 0.0) * beta_row
        a_b = (eye + n_mat).astype(BF16)
        t_inv = eye - n_mat
        for _ in range(6):
            tb = t_inv.astype(BF16)
            err = eye - jnp.dot(a_b, tb, preferred_element_type=F32)
            t_inv = t_inv + jnp.dot(tb, err.astype(BF16), preferred_element_type=F32)
        u_val = jnp.dot(t_inv.astype(BF16), vh, preferred_element_type=F32)
        w_key = jnp.dot((t_inv * jnp.exp(gc_row)).astype(BF16), kh, preferred_element_type=F32)
        a_mat = (qk * decay * beta_row).astype(BF16)

        s_old = s_ref[h]
        sb = s_old.astype(BF16)
        v_new = u_val - jnp.dot(w_key.astype(BF16), sb, preferred_element_type=F32)
        vnb = v_new.astype(BF16)
        o_h = jnp.exp(gc_col) * jnp.dot(qh, sb, preferred_element_type=F32) \
            + jnp.dot(a_mat, vnb, preferred_element_type=F32)
        kd_t = (kh.astype(F32).T * (jnp.exp(gtot_row - gc_row) * beta_row)).astype(BF16)
        s_ref[h] = s_old * jnp.exp(gtot_row) + jnp.dot(kd_t, vnb, preferred_element_type=F32)
        o_ref[0, :, lanes] = o_h.astype(BF16)


def _gdn_call(q, k, v, gates, batch, ctx_len, seq_len):
    rows, d = q.shape
    c = GDN_CHUNK
    n_ctx = ctx_len // c
    n_steps = (ctx_len + seq_len) // c

    def chunk(b, dr, s):
        rev = jnp.where(s < n_ctx, n_ctx - 1 - s, n_steps - 1 - (s - n_ctx))
        return b * n_steps + jnp.where(dr == 0, s, rev)

    blk = pl.BlockSpec((c, d), lambda b, dr, s: (chunk(b, dr, s), 0))
    return pl.pallas_call(
        _gdn_kernel,
        grid=(batch, 2, n_steps),
        in_specs=[blk, blk, blk, pl.BlockSpec((1, c, 128), lambda b, dr, s: (dr, chunk(b, dr, s), 0))],
        out_specs=pl.BlockSpec((1, c, d), lambda b, dr, s: (dr, chunk(b, dr, s), 0)),
        out_shape=jax.ShapeDtypeStruct((2, rows, d), BF16),
        scratch_shapes=[pltpu.VMEM((N_HEADS, HEAD_DIM, HEAD_DIM), F32)],
        compiler_params=_cparams(("arbitrary", "arbitrary", "arbitrary")),
        name="gated_delta",
    )(q, k, v, gates)


def _mix_kernel(x_ref, u_ref, v_ref, of_ref, ob_ref, z_ref, ga_ref, gb_ref,
                g1_ref, sh2_ref, sc2_ref, n2g_ref, lng_ref, ws_ref, bs_ref, bng_ref,
                wpa_ref, wpb_ref, wout_ref, wr_ref, br_ref,
                xo_ref, h2_ref, comb_ref, sa_ref, sb_ref):
    d = D_MODEL
    ug = _gelu_tanh(u_ref[...].astype(F32))
    vg = _gelu_tanh(v_ref[...].astype(F32))
    mu = jnp.mean(vg, axis=-1, keepdims=True)
    vc = vg - mu
    var = jnp.mean(vc * vc, axis=-1, keepdims=True)
    vn = (vc * lax.rsqrt(var + EPS) * lng_ref[...]).astype(BF16)
    for ch in range(ROW_TILE // SGU_CHUNK):
        rws = slice(ch * SGU_CHUNK, (ch + 1) * SGU_CHUNK)
        for g in range(SGU_GROUPS):
            lanes = slice(g * 128, (g + 1) * 128)
            mixed = jnp.dot(ws_ref[g], vn[rws, lanes], preferred_element_type=F32) + bs_ref[:, lanes]
            sa_ref[rws, lanes] = (ug[rws, lanes] * mixed).astype(BF16)
    y_a = jnp.dot(sa_ref[...], wpa_ref[...], preferred_element_type=F32)
    for h in range(N_HEADS):
        lanes = slice(h * HEAD_DIM, (h + 1) * HEAD_DIM)
        o = of_ref[0, :, lanes].astype(F32) + ob_ref[0, :, lanes].astype(F32)
        ms = jnp.mean(o * o, axis=-1, keepdims=True)
        o = o * lax.rsqrt(ms + EPS) * bng_ref[...]
        sb_ref[:, lanes] = (o * _silu(z_ref[:, lanes].astype(F32))).astype(BF16)
    y_b = jnp.dot(sb_ref[...], wpb_ref[...], preferred_element_type=F32)
    merged = _sigmoid(ga_ref[...].astype(F32)) * y_a + _sigmoid(gb_ref[...].astype(F32)) * y_b
    y = jnp.dot(merged.astype(BF16), wout_ref[...], preferred_element_type=F32)
    xn = x_ref[...] + g1_ref[0] * y
    xo_ref[...] = xn
    ms = jnp.mean(xn * xn, axis=-1, keepdims=True)
    h2 = xn * lax.rsqrt(ms + EPS) * n2g_ref[...]
    h2 = h2 * (1.0 + sc2_ref[0]) + sh2_ref[0]
    h2_ref[...] = h2.astype(BF16)
    logits = lax.dot_general(wr_ref[...], h2, (((1,), (1,)), ((), ())),
                             precision=HIGHEST, preferred_element_type=F32)
    scores = _sigmoid(logits)
    sel = scores + br_ref[...]
    srow = [sel[e:e + 1, :] for e in range(N_EXPERTS)]
    grp = []
    for g in range(N_EXPERT_GROUPS):
        m = srow[4 * g:4 * g + 4]
        best2 = None
        for a in range(4):
            for b in range(a + 1, 4):
                pair = m[a] + m[b]
                best2 = pair if best2 is None else jnp.maximum(best2, pair)
        grp.append(best2)
    best_val = grp[0]
    best_idx = jnp.zeros_like(best_val, dtype=jnp.int32)
    for g in range(1, N_EXPERT_GROUPS):
        better = grp[g] > best_val
        best_val = jnp.where(better, grp[g], best_val)
        best_idx = jnp.where(better, g, best_idx)
    picked = []
    for e in range(N_EXPERTS):
        g = e // EXPERTS_PER_GROUP
        rank = jnp.zeros_like(best_idx)
        for o_e in range(4 * g, 4 * g + 4):
            if o_e == e:
                continue
            ahead = (srow[o_e] > srow[e]) if o_e > e else (srow[o_e] >= srow[e])
            rank = rank + ahead.astype(jnp.int32)
        chosen = jnp.logical_and(best_idx == g, rank < 2)
        picked.append(jnp.where(chosen, scores[e:e + 1, :], 0.0))
    total = picked[0]
    for e in range(1, N_EXPERTS):
        total = total + picked[e]
    inv = 1.0 / total
    for e in range(N_EXPERTS):
        comb_ref[e:e + 1, :] = picked[e] * inv


def _mix_call(x, u, v, o2, z, ga, gb, mods, layer, norm2_g, ln_g, ws, bs, bng,
              wpa, wpb, wout, wr_t, br, mod_row):
    rows, d = x.shape
    nt = rows // ROW_TILE
    g1_idx = (layer * 6 + 2) * COND_ROWS
    sh2_idx = (layer * 6 + 3) * COND_ROWS
    sc2_idx = (layer * 6 + 4) * COND_ROWS
    row_spec = pl.BlockSpec((ROW_TILE, d), lambda i: (i, 0))
    vec = pl.BlockSpec((1, d), lambda i: (0, 0))
    wspec = pl.BlockSpec((d, d), lambda i: (0, 0))

    def mod(idx):
        return pl.BlockSpec((1, 1, d), lambda i: (idx + mod_row(i), 0, 0))

    return pl.pallas_call(
        _mix_kernel,
        grid=(nt,),
        in_specs=[
            row_spec, row_spec, row_spec,
            pl.BlockSpec((1, ROW_TILE, d), lambda i: (0, i, 0)),
            pl.BlockSpec((1, ROW_TILE, d), lambda i: (1, i, 0)),
            row_spec, row_spec, row_spec,
            mod(g1_idx), mod(sh2_idx), mod(sc2_idx),
            vec, vec,
            pl.BlockSpec((SGU_GROUPS, SGU_CHUNK, SGU_CHUNK), lambda i: (0, 0, 0)),
            pl.BlockSpec((SGU_CHUNK, d), lambda i: (0, 0)),
            pl.BlockSpec((1, HEAD_DIM), lambda i: (0, 0)),
            wspec, wspec, wspec,
            pl.BlockSpec((N_EXPERTS, d), lambda i: (0, 0)),
            pl.BlockSpec((N_EXPERTS, 1), lambda i: (0, 0)),
        ],
        out_specs=[row_spec, row_spec, pl.BlockSpec((N_EXPERTS, ROW_TILE), lambda i: (0, i))],
        out_shape=[jax.ShapeDtypeStruct((rows, d), F32), jax.ShapeDtypeStruct((rows, d), BF16),
                   jax.ShapeDtypeStruct((N_EXPERTS, rows), F32)],
        scratch_shapes=[pltpu.VMEM((ROW_TILE, d), BF16), pltpu.VMEM((ROW_TILE, d), BF16)],
        compiler_params=_cparams(("arbitrary",)),
        name="mix_merge_router",
    )(x, u, v, o2, o2, z, ga, gb, mods, mods, mods, norm2_g, ln_g, ws, bs, bng,
      wpa, wpb, wout, wr_t, br)


def _moe_kernel(final, h_ref, comb_ref, x_ref, w1_ref, w3_ref, w2_ref, fg_ref, *rest):
    g2_refs = rest[:MOE_SUB]
    o_ref = rest[MOE_SUB]
    acc_ref = rest[MOE_SUB + 1]
    e = pl.program_id(1)

    @pl.when(e == 0)
    def _():
        acc_ref[...] = jnp.zeros_like(acc_ref)

    h = h_ref[...]
    a = jnp.dot(h, w1_ref[0], preferred_element_type=F32)
    b = jnp.dot(h, w3_ref[0], preferred_element_type=F32)
    comb = comb_ref[...]
    lane = lax.broadcasted_iota(jnp.int32, comb.shape, 1)
    c_col = jnp.sum(jnp.where(lane == e, comb, 0.0), axis=-1, keepdims=True)
    hid = (_silu(a) * b * c_col).astype(BF16)
    acc_ref[...] += jnp.dot(hid, w2_ref[0], preferred_element_type=F32)

    @pl.when(e == N_EXPERTS - 1)
    def _():
        for s in range(MOE_SUB):
            rws = slice(s * ROW_TILE, (s + 1) * ROW_TILE)
            xn = x_ref[rws, :] + g2_refs[s][0] * acc_ref[rws, :]
            if final:
                ms = jnp.mean(xn * xn, axis=-1, keepdims=True)
                xn = xn * lax.rsqrt(ms + EPS) * fg_ref[...]
            o_ref[rws, :] = xn


def _moe_call(h2, comb, x, w1, w3, w2, final_g, mods, layer, mod_row, final):
    rows, d = x.shape
    tm = MOE_SUB * ROW_TILE
    nt = rows // tm
    g2_idx = (layer * 6 + 5) * COND_ROWS
    row_spec = pl.BlockSpec((tm, d), lambda i, e: (i, 0))

    def mod(s):
        return pl.BlockSpec((1, 1, d), lambda i, e: (g2_idx + mod_row(i * MOE_SUB + s), 0, 0))

    return pl.pallas_call(
        functools.partial(_moe_kernel, final),
        grid=(nt, N_EXPERTS),
        in_specs=[
            row_spec,
            pl.BlockSpec((tm, 128), lambda i, e: (i, 0)),
            row_spec,
            pl.BlockSpec((1, d, D_EXPERT), lambda i, e: (e, 0, 0)),
            pl.BlockSpec((1, d, D_EXPERT), lambda i, e: (e, 0, 0)),
            pl.BlockSpec((1, D_EXPERT, d), lambda i, e: (e, 0, 0)),
            pl.BlockSpec((1, d), lambda i, e: (0, 0)),
        ] + [mod(s) for s in range(MOE_SUB)],
        out_specs=row_spec,
        out_shape=jax.ShapeDtypeStruct((rows, d), F32),
        scratch_shapes=[pltpu.VMEM((tm, d), F32)],
        compiler_params=_cparams(("arbitrary", "arbitrary")),
        name="moe_residual",
    )(h2, comb, x, w1, w3, w2, final_g, *([mods] * MOE_SUB))


def kernel(x, c, ctx, c_ctx, w_ada, b_ada, norm1_g, norm2_g, w_in, a_ln_g, a_w_s, a_b_s, b_conv_w, b_A_log, b_dt_bias, b_norm_g, w_proj_a, w_proj_b, w_out, w_router, b_router, w_e1, w_e3, w_e2, final_g):
    batch, seq_len, d = x.shape
    ctx_len = ctx.shape[1]
    n_layers = w_ada.shape[0]
    assert d == D_MODEL and batch + 1 <= COND_ROWS
    assert ctx_len % ROW_TILE == 0 and seq_len % ROW_TILE == 0
    per_batch = ctx_len + seq_len
    rows = batch * per_batch
    assert rows % (MOE_SUB * ROW_TILE) == 0
    tiles_per_batch = per_batch // ROW_TILE
    ctx_tiles = ctx_len // ROW_TILE

    def mod_row(tile):
        return jnp.where(tile % tiles_per_batch < ctx_tiles, batch, tile // tiles_per_batch)

    xs = jnp.concatenate([ctx, x], axis=1).reshape(rows, d)
    cond = jnp.concatenate([c, c_ctx[None, :], jnp.zeros((COND_ROWS - batch - 1, d), F32)], axis=0)
    mods = _ada_call(cond, w_ada, b_ada).reshape(n_layers * 6 * COND_ROWS, 1, d)

    n_ab = 4 * N_HEADS
    ab_lo = 6 * d
    wr_t = w_router.T
    br = b_router.reshape(N_EXPERTS, 1)
    fg = final_g.reshape(1, d)
    for l in range(n_layers):
        w_main = jnp.concatenate([w_in[l][:, :ab_lo], w_in[l][:, ab_lo + n_ab:]], axis=1).astype(BF16)
        w_ab = jnp.pad(w_in[l][:, ab_lo:ab_lo + n_ab], ((0, 0), (0, 128 - n_ab)))
        u, v, q, k, vv, z, ga, gb, ab = _in_call(xs, mods, l, norm1_g[l].reshape(1, d), w_main, w_ab, mod_row)

        conv_w = jnp.pad(b_conv_w[l], ((0, 8 - CONV_TAPS), (0, 0)))
        zeros8 = jnp.zeros((N_HEADS,), F32)
        alog_row = jnp.pad(jnp.concatenate([b_A_log[l, 0], zeros8, b_A_log[l, 1], zeros8]), (0, 128 - n_ab)).reshape(1, 128)
        dtb_row = jnp.pad(jnp.concatenate([b_dt_bias[l, 0], zeros8, b_dt_bias[l, 1], zeros8]), (0, 128 - n_ab)).reshape(1, 128)
        qn, kn, vs, gates = _conv_call(q, k, vv, conv_w, ab, alog_row, dtb_row, tiles_per_batch, ctx_tiles)
        o2 = _gdn_call(qn, kn, vs, gates, batch, ctx_len, seq_len)

        bs = jnp.repeat(a_b_s[l].T, SGU_CHUNK, axis=1)
        bng = b_norm_g[l].reshape(1, HEAD_DIM)
        xs, h2, comb_t = _mix_call(
            xs, u, v, o2, z, ga, gb, mods, l, norm2_g[l].reshape(1, d), a_ln_g[l].reshape(1, d),
            a_w_s[l].astype(BF16), bs, bng,
            w_proj_a[l].astype(BF16), w_proj_b[l].astype(BF16), w_out[l].astype(BF16), wr_t, br, mod_row)
        comb = jnp.pad(comb_t.T, ((0, 0), (0, 128 - N_EXPERTS)))
        xs = _moe_call(h2, comb, xs, w_e1[l].astype(BF16), w_e3[l].astype(BF16), w_e2[l].astype(BF16),
                       fg, mods, l, mod_row, final=(l == n_layers - 1))
    return xs.reshape(batch, per_batch, d)[:, ctx_len:, :]
```

```python
import functools

import jax
import jax.numpy as jnp
from jax import lax
from jax.experimental import pallas as pl
from jax.experimental.pallas import tpu as pltpu

F32 = jnp.float32
BF16 = jnp.bfloat16
HIGHEST = lax.Precision.HIGHEST

EPS = 1e-6
D_MODEL = 1024
N_HEADS = 8
HEAD_DIM = 128
SGU_CHUNK = 128
SGU_GROUPS = 8
CONV_TAPS = 5
N_EXPERTS = 16
EXPERTS_PER_GROUP = 4
N_EXPERT_GROUPS = 4
D_EXPERT = 512
ROW_TILE = 256
GDN_CHUNK = 128
MOE_SUB = 4
COND_ROWS = 8
HALO = 16
VMEM_LIMIT = 56 * 1024 * 1024


def _cparams(sem):
    return pltpu.CompilerParams(dimension_semantics=sem, vmem_limit_bytes=VMEM_LIMIT)


def _sigmoid(x):
    return 1.0 / (1.0 + jnp.exp(-x))


def _silu(x):
    return x * _sigmoid(x)


def _gelu_tanh(x):
    return 0.5 * x * (1.0 + jnp.tanh(0.7978845608028654 * (x + 0.044715 * (x * x * x))))


def _softplus(x):
    return jnp.maximum(x, 0.0) + jnp.log1p(jnp.exp(-jnp.abs(x)))


def _ada_kernel(cond_ref, w_ref, b_ref, o_ref):
    s = _silu(cond_ref[...])
    o_ref[0, 0] = jnp.dot(s, w_ref[0], precision=HIGHEST, preferred_element_type=F32) + b_ref[0, 0]


def _ada_call(cond, w_ada, b_ada):
    n_layers = w_ada.shape[0]
    d = D_MODEL
    return pl.pallas_call(
        _ada_kernel,
        grid=(n_layers, 6),
        in_specs=[
            pl.BlockSpec((COND_ROWS, d), lambda l, j: (0, 0)),
            pl.BlockSpec((1, d, d), lambda l, j: (l, 0, j)),
            pl.BlockSpec((1, 1, 1, d), lambda l, j: (l, j, 0, 0)),
        ],
        out_specs=pl.BlockSpec((1, 1, COND_ROWS, d), lambda l, j: (l, j, 0, 0)),
        out_shape=jax.ShapeDtypeStruct((n_layers, 6, COND_ROWS, d), F32),
        compiler_params=_cparams(("arbitrary", "arbitrary")),
        name="ada_params",
    )(cond, w_ada, b_ada.reshape(n_layers, 6, 1, d))


def _in_kernel(x_ref, sh_ref, sc_ref, g_ref, w_ref, wab_ref,
               u_ref, v_ref, q_ref, k_ref, vv_ref, z_ref, ga_ref, gb_ref, ab_ref):
    x = x_ref[...]
    ms = jnp.mean(x * x, axis=-1, keepdims=True)
    h = x * lax.rsqrt(ms + EPS) * g_ref[...]
    h = h * (1.0 + sc_ref[0]) + sh_ref[0]
    hb = h.astype(BF16)
    d = D_MODEL
    outs = (u_ref, v_ref, q_ref, k_ref, vv_ref, z_ref, ga_ref, gb_ref)
    for n, o_ref in enumerate(outs):
        o_ref[...] = jnp.dot(hb, w_ref[:, n * d:(n + 1) * d],
                             preferred_element_type=F32).astype(BF16)
    ab_ref[...] = jnp.dot(h, wab_ref[...], precision=HIGHEST, preferred_element_type=F32)


def _in_call(x, mods, layer, norm_g, w_main, w_ab, mod_row):
    rows, d = x.shape
    nt = rows // ROW_TILE
    sh_idx = (layer * 6 + 0) * COND_ROWS
    sc_idx = (layer * 6 + 1) * COND_ROWS
    row_spec = pl.BlockSpec((ROW_TILE, d), lambda i: (i, 0))
    out_shapes = [jax.ShapeDtypeStruct((rows, d), BF16)] * 8 + [jax.ShapeDtypeStruct((rows, 128), F32)]
    return pl.pallas_call(
        _in_kernel,
        grid=(nt,),
        in_specs=[
            row_spec,
            pl.BlockSpec((1, 1, d), lambda i: (sh_idx + mod_row(i), 0, 0)),
            pl.BlockSpec((1, 1, d), lambda i: (sc_idx + mod_row(i), 0, 0)),
            pl.BlockSpec((1, d), lambda i: (0, 0)),
            pl.BlockSpec((d, 8 * d), lambda i: (0, 0), pipeline_mode=pl.Buffered(1)),
            pl.BlockSpec((d, 128), lambda i: (0, 0)),
        ],
        out_specs=[row_spec] * 8 + [pl.BlockSpec((ROW_TILE, 128), lambda i: (i, 0))],
        out_shape=out_shapes,
        compiler_params=_cparams(("arbitrary",)),
        name="norm_in_proj",
    )(x, mods, mods, norm_g, w_main, w_ab)


def _conv_kernel(tiles_per_batch, ctx_tiles,
                 q_ref, qp_ref, qn_ref, k_ref, kp_ref, kn_ref, v_ref, vp_ref, vn_ref,
                 cw_ref, ab_ref, alog_ref, dtb_ref,
                 qo_ref, ko_ref, vo_ref, gate_ref, ext_ref):
    j = pl.program_id(0) % tiles_per_batch
    has_prev = jnp.logical_and(j != 0, j != ctx_tiles).astype(F32)
    has_next = jnp.logical_and(j != ctx_tiles - 1, j != tiles_per_batch - 1).astype(F32)
    pad = CONV_TAPS // 2
    srcs = ((q_ref, qp_ref, qn_ref, qo_ref, True, HEAD_DIM ** -0.5),
            (k_ref, kp_ref, kn_ref, ko_ref, True, 1.0),
            (v_ref, vp_ref, vn_ref, vo_ref, False, 1.0))
    for t, (m_ref, p_ref, n_ref, o_ref, l2, scale) in enumerate(srcs):
        ext_ref[0:8, :] = p_ref[HALO - 8:HALO, :].astype(F32) * has_prev
        ext_ref[8:8 + ROW_TILE, :] = m_ref[...].astype(F32)
        ext_ref[8 + ROW_TILE:16 + ROW_TILE, :] = n_ref[0:8, :].astype(F32) * has_next
        for h in range(N_HEADS):
            lanes = slice(h * HEAD_DIM, (h + 1) * HEAD_DIM)
            acc = None
            for tap in range(CONV_TAPS):
                w_row = cw_ref[tap:tap + 1, t * D_MODEL + h * HEAD_DIM:t * D_MODEL + (h + 1) * HEAD_DIM]
                start = 8 - pad + tap
                term = ext_ref[start:start + ROW_TILE, lanes] * w_row
                acc = term if acc is None else acc + term
            y = _silu(acc)
            if l2:
                ss = jnp.sum(y * y, axis=-1, keepdims=True)
                y = y * (lax.rsqrt(ss + EPS) * scale)
            o_ref[:, lanes] = y.astype(BF16)
    ab = ab_ref[...]
    lane = lax.broadcasted_iota(jnp.int32, ab.shape, 1)
    is_decay = ((lane // N_HEADS) % 2) == 0
    g = -jnp.exp(alog_ref[...]) * _softplus(ab + dtb_ref[...])
    gate = jnp.where(is_decay, g, _sigmoid(ab))
    gate = jnp.where(lane < 4 * N_HEADS, gate, 0.0)
    gate_ref[0] = gate
    gate_ref[1] = pltpu.roll(gate, 128 - 2 * N_HEADS, axis=1)


def _conv_call(q, k, v, conv_w, ab, alog_row, dtb_row, tiles_per_batch, ctx_tiles):
    rows, d = q.shape
    nt = rows // ROW_TILE
    per = ROW_TILE // HALO
    n_halo = rows // HALO
    main = pl.BlockSpec((ROW_TILE, d), lambda i: (i, 0))
    prev = pl.BlockSpec((HALO, d), lambda i: (jnp.maximum(i * per - 1, 0), 0))
    nxt = pl.BlockSpec((HALO, d), lambda i: (jnp.minimum((i + 1) * per, n_halo - 1), 0))
    small = pl.BlockSpec((1, 128), lambda i: (0, 0))
    return pl.pallas_call(
        functools.partial(_conv_kernel, tiles_per_batch, ctx_tiles),
        grid=(nt,),
        in_specs=[main, prev, nxt] * 3 + [
            pl.BlockSpec((8, 3 * d), lambda i: (0, 0)),
            pl.BlockSpec((ROW_TILE, 128), lambda i: (i, 0)),
            small, small,
        ],
        out_specs=[main, main, main, pl.BlockSpec((2, ROW_TILE, 128), lambda i: (0, i, 0))],
        out_shape=[jax.ShapeDtypeStruct((rows, d), BF16)] * 3 + [jax.ShapeDtypeStruct((2, rows, 128), F32)],
        scratch_shapes=[pltpu.VMEM((ROW_TILE + 16, d), F32)],
        compiler_params=_cparams(("arbitrary",)),
        name="conv_silu_gates",
    )(q, q, q, k, k, k, v, v, v, conv_w, ab, alog_row, dtb_row)


def _gdn_kernel(q_ref, k_ref, v_ref, gate_ref, o_ref, s_ref):
    c = GDN_CHUNK
    fwd = pl.program_id(1) == 0

    @pl.when(pl.program_id(2) == 0)
    def _():
        s_ref[...] = jnp.zeros_like(s_ref)

    row = lax.broadcasted_iota(jnp.int32, (c, c), 0)
    col = lax.broadcasted_iota(jnp.int32, (c, c), 1)
    later = jnp.where(fwd, row, col)
    earlier = jnp.where(fwd, col, row)
    incl = later >= earlier
    strict = later > earlier
    eye = (row == col).astype(F32)

    gate = gate_ref[0]
    gc_all = jnp.dot(incl.astype(F32), gate, precision=HIGHEST, preferred_element_type=F32)
    gtot_all = jnp.dot(jnp.ones((c, c), F32), gate, precision=HIGHEST, preferred_element_type=F32)
    gate_t = gate.T
    gc_t = gc_all.T
    gtot_t = gtot_all.T

    heads = range(N_HEADS)
    lanes = [slice(h * HEAD_DIM, (h + 1) * HEAD_DIM) for h in heads]
    nt_dims = (((1,), (1,)), ((), ()))
    q = [q_ref[:, lanes[h]] for h in heads]
    k = [k_ref[:, lanes[h]] for h in heads]
    v = [v_ref[:, lanes[h]] for h in heads]
    gc_col = [jnp.broadcast_to(gc_all[:, h:h + 1], (c, c)) for h in heads]
    gc_row = [gc_t[h:h + 1, :] for h in heads]
    gtot_row = [gtot_t[h:h + 1, :] for h in heads]
    beta_row = [gate_t[N_HEADS + h:N_HEADS + h + 1, :] for h in heads]

    decay = [jnp.exp(jnp.where(incl, gc_col[h] - gc_row[h], -jnp.inf)) * beta_row[h] for h in heads]
    kq = [lax.dot_general(jnp.concatenate([k[h], q[h]], axis=0), k[h], nt_dims, preferred_element_type=F32)
          for h in heads]
    n_mat = [jnp.where(strict, kq[h][:c] * decay[h], 0.0) for h in heads]
    a_mat = [(kq[h][c:] * decay[h]).astype(BF16) for h in heads]
    a_b = [(eye + n_mat[h]).astype(BF16) for h in heads]
    t_inv = [eye - n_mat[h] for h in heads]
    for _ in range(6):
        tb = [t_inv[h].astype(BF16) for h in heads]
        err = [(eye - jnp.dot(a_b[h], tb[h], preferred_element_type=F32)).astype(BF16) for h in heads]
        t_inv = [t_inv[h] + jnp.dot(tb[h], err[h], preferred_element_type=F32) for h in heads]
    u_val = [jnp.dot(t_inv[h].astype(BF16), v[h], preferred_element_type=F32) for h in heads]
    w_key = [jnp.dot((t_inv[h] * jnp.exp(gc_row[h])).astype(BF16), k[h], preferred_element_type=F32)
             for h in heads]

    s_old = [s_ref[h] for h in heads]
    sb = [s_old[h].astype(BF16) for h in heads]
    wq_s = [jnp.dot(jnp.concatenate([w_key[h].astype(BF16), q[h]], axis=0), sb[h], preferred_element_type=F32)
            for h in heads]
    v_new = [(u_val[h] - wq_s[h][:c]).astype(BF16) for h in heads]
    o_val = [jnp.exp(gc_col[h]) * wq_s[h][c:] + jnp.dot(a_mat[h], v_new[h], preferred_element_type=F32)
             for h in heads]
    kd_t = [(k[h].astype(F32).T * (jnp.exp(gtot_row[h] - gc_row[h]) * beta_row[h])).astype(BF16) for h in heads]
    for h in heads:
        s_ref[h] = s_old[h] * jnp.exp(gtot_row[h]) + jnp.dot(kd_t[h], v_new[h], preferred_element_type=F32)
        o_ref[0, :, lanes[h]] = o_val[h].astype(BF16)


def _gdn_call(q, k, v, gates, batch, ctx_len, seq_len):
    rows, d = q.shape
    c = GDN_CHUNK
    n_ctx = ctx_len // c
    n_steps = (ctx_len + seq_len) // c

    def chunk(b, dr, s):
        rev = jnp.where(s < n_ctx, n_ctx - 1 - s, n_steps - 1 - (s - n_ctx))
        return b * n_steps + jnp.where(dr == 0, s, rev)

    blk = pl.BlockSpec((c, d), lambda b, dr, s: (chunk(b, dr, s), 0))
    return pl.pallas_call(
        _gdn_kernel,
        grid=(batch, 2, n_steps),
        in_specs=[blk, blk, blk, pl.BlockSpec((1, c, 128), lambda b, dr, s: (dr, chunk(b, dr, s), 0))],
        out_specs=pl.BlockSpec((1, c, d), lambda b, dr, s: (dr, chunk(b, dr, s), 0)),
        out_shape=jax.ShapeDtypeStruct((2, rows, d), BF16),
        scratch_shapes=[pltpu.VMEM((N_HEADS, HEAD_DIM, HEAD_DIM), F32)],
        compiler_params=_cparams(("arbitrary", "arbitrary", "arbitrary")),
        name="gated_delta",
    )(q, k, v, gates)


def _mix_kernel(x_ref, u_ref, v_ref, of_ref, ob_ref, z_ref, ga_ref, gb_ref,
                g1_ref, sh2_ref, sc2_ref, n2g_ref, lng_ref, ws_ref, bs_ref, bng_ref,
                wpa_ref, wpb_ref, wout_ref, wr_ref, br_ref,
                xo_ref, h2_ref, comb_ref, sa_ref, sb_ref):
    d = D_MODEL
    ug = _gelu_tanh(u_ref[...].astype(F32))
    vg = _gelu_tanh(v_ref[...].astype(F32))
    mu = jnp.mean(vg, axis=-1, keepdims=True)
    vc = vg - mu
    var = jnp.mean(vc * vc, axis=-1, keepdims=True)
    vn = (vc * lax.rsqrt(var + EPS) * lng_ref[...]).astype(BF16)
    for ch in range(ROW_TILE // SGU_CHUNK):
        rws = slice(ch * SGU_CHUNK, (ch + 1) * SGU_CHUNK)
        for g in range(SGU_GROUPS):
            lanes = slice(g * 128, (g + 1) * 128)
            mixed = jnp.dot(ws_ref[g], vn[rws, lanes], preferred_element_type=F32) + bs_ref[:, lanes]
            sa_ref[rws, lanes] = (ug[rws, lanes] * mixed).astype(BF16)
    y_a = jnp.dot(sa_ref[...], wpa_ref[...], preferred_element_type=F32)
    for h in range(N_HEADS):
        lanes = slice(h * HEAD_DIM, (h + 1) * HEAD_DIM)
        o = of_ref[0, :, lanes].astype(F32) + ob_ref[0, :, lanes].astype(F32)
        ms = jnp.mean(o * o, axis=-1, keepdims=True)
        o = o * lax.rsqrt(ms + EPS) * bng_ref[...]
        sb_ref[:, lanes] = (o * _silu(z_ref[:, lanes].astype(F32))).astype(BF16)
    y_b = jnp.dot(sb_ref[...], wpb_ref[...], preferred_element_type=F32)
    merged = _sigmoid(ga_ref[...].astype(F32)) * y_a + _sigmoid(gb_ref[...].astype(F32)) * y_b
    y = jnp.dot(merged.astype(BF16), wout_ref[...], preferred_element_type=F32)
    xn = x_ref[...] + g1_ref[0] * y
    xo_ref[...] = xn
    ms = jnp.mean(xn * xn, axis=-1, keepdims=True)
    h2 = xn * lax.rsqrt(ms + EPS) * n2g_ref[...]
    h2 = h2 * (1.0 + sc2_ref[0]) + sh2_ref[0]
    h2_ref[...] = h2.astype(BF16)
    logits = lax.dot_general(wr_ref[...], h2, (((1,), (1,)), ((), ())),
                             precision=HIGHEST, preferred_element_type=F32)
    scores = _sigmoid(logits)
    sel = scores + br_ref[...]
    srow = [sel[e:e + 1, :] for e in range(N_EXPERTS)]
    grp = []
    for g in range(N_EXPERT_GROUPS):
        m = srow[4 * g:4 * g + 4]
        best2 = None
        for a in range(4):
            for b in range(a + 1, 4):
                pair = m[a] + m[b]
                best2 = pair if best2 is None else jnp.maximum(best2, pair)
        grp.append(best2)
    best_val = grp[0]
    best_idx = jnp.zeros_like(best_val, dtype=jnp.int32)
    for g in range(1, N_EXPERT_GROUPS):
        better = grp[g] > best_val
        best_val = jnp.where(better, grp[g], best_val)
        best_idx = jnp.where(better, g, best_idx)
    picked = []
    for e in range(N_EXPERTS):
        g = e // EXPERTS_PER_GROUP
        rank = jnp.zeros_like(best_idx)
        for o_e in range(4 * g, 4 * g + 4):
            if o_e == e:
                continue
            ahead = (srow[o_e] > srow[e]) if o_e > e else (srow[o_e] >= srow[e])
            rank = rank + ahead.astype(jnp.int32)
        chosen = jnp.logical_and(best_idx == g, rank < 2)
        picked.append(jnp.where(chosen, scores[e:e + 1, :], 0.0))
    total = picked[0]
    for e in range(1, N_EXPERTS):
        total = total + picked[e]
    inv = 1.0 / total
    for e in range(N_EXPERTS):
        comb_ref[e:e + 1, :] = picked[e] * inv


def _mix_call(x, u, v, o2, z, ga, gb, mods, layer, norm2_g, ln_g, ws, bs, bng,
              wpa, wpb, wout, wr_t, br, mod_row):
    rows, d = x.shape
    nt = rows // ROW_TILE
    g1_idx = (layer * 6 + 2) * COND_ROWS
    sh2_idx = (layer * 6 + 3) * COND_ROWS
    sc2_idx = (layer * 6 + 4) * COND_ROWS
    row_spec = pl.BlockSpec((ROW_TILE, d), lambda i: (i, 0))
    vec = pl.BlockSpec((1, d), lambda i: (0, 0))
    wspec = pl.BlockSpec((d, d), lambda i: (0, 0))

    def mod(idx):
        return pl.BlockSpec((1, 1, d), lambda i: (idx + mod_row(i), 0, 0))

    return pl.pallas_call(
        _mix_kernel,
        grid=(nt,),
        in_specs=[
            row_spec, row_spec, row_spec,
            pl.BlockSpec((1, ROW_TILE, d), lambda i: (0, i, 0)),
            pl.BlockSpec((1, ROW_TILE, d), lambda i: (1, i, 0)),
            row_spec, row_spec, row_spec,
            mod(g1_idx), mod(sh2_idx), mod(sc2_idx),
            vec, vec,
            pl.BlockSpec((SGU_GROUPS, SGU_CHUNK, SGU_CHUNK), lambda i: (0, 0, 0)),
            pl.BlockSpec((SGU_CHUNK, d), lambda i: (0, 0)),
            pl.BlockSpec((1, HEAD_DIM), lambda i: (0, 0)),
            wspec, wspec, wspec,
            pl.BlockSpec((N_EXPERTS, d), lambda i: (0, 0)),
            pl.BlockSpec((N_EXPERTS, 1), lambda i: (0, 0)),
        ],
        out_specs=[row_spec, row_spec, pl.BlockSpec((N_EXPERTS, ROW_TILE), lambda i: (0, i))],
        out_shape=[jax.ShapeDtypeStruct((rows, d), F32), jax.ShapeDtypeStruct((rows, d), BF16),
                   jax.ShapeDtypeStruct((N_EXPERTS, rows), F32)],
        scratch_shapes=[pltpu.VMEM((ROW_TILE, d), BF16), pltpu.VMEM((ROW_TILE, d), BF16)],
        compiler_params=_cparams(("arbitrary",)),
        name="mix_merge_router",
    )(x, u, v, o2, o2, z, ga, gb, mods, mods, mods, norm2_g, ln_g, ws, bs, bng,
      wpa, wpb, wout, wr_t, br)


def _moe_kernel(final, h_ref, comb_ref, x_ref, w1_ref, w3_ref, w2_ref, fg_ref, *rest):
    g2_refs = rest[:MOE_SUB]
    o_ref = rest[MOE_SUB]
    acc_ref = rest[MOE_SUB + 1]
    e = pl.program_id(1)

    @pl.when(e == 0)
    def _():
        acc_ref[...] = jnp.zeros_like(acc_ref)

    h = h_ref[...]
    a = jnp.dot(h, w1_ref[0], preferred_element_type=F32)
    b = jnp.dot(h, w3_ref[0], preferred_element_type=F32)
    comb = comb_ref[...]
    lane = lax.broadcasted_iota(jnp.int32, comb.shape, 1)
    c_col = jnp.sum(jnp.where(lane == e, comb, 0.0), axis=-1, keepdims=True)
    hid = (_silu(a) * b * c_col).astype(BF16)
    acc_ref[...] += jnp.dot(hid, w2_ref[0], preferred_element_type=F32)

    @pl.when(e == N_EXPERTS - 1)
    def _():
        for s in range(MOE_SUB):
            rws = slice(s * ROW_TILE, (s + 1) * ROW_TILE)
            xn = x_ref[rws, :] + g2_refs[s][0] * acc_ref[rws, :]
            if final:
                ms = jnp.mean(xn * xn, axis=-1, keepdims=True)
                xn = xn * lax.rsqrt(ms + EPS) * fg_ref[...]
            o_ref[rws, :] = xn


def _moe_call(h2, comb, x, w1, w3, w2, final_g, mods, layer, mod_row, final):
    rows, d = x.shape
    tm = MOE_SUB * ROW_TILE
    nt = rows // tm
    g2_idx = (layer * 6 + 5) * COND_ROWS
    row_spec = pl.BlockSpec((tm, d), lambda i, e: (i, 0))

    def mod(s):
        return pl.BlockSpec((1, 1, d), lambda i, e: (g2_idx + mod_row(i * MOE_SUB + s), 0, 0))

    return pl.pallas_call(
        functools.partial(_moe_kernel, final),
        grid=(nt, N_EXPERTS),
        in_specs=[
            row_spec,
            pl.BlockSpec((tm, 128), lambda i, e: (i, 0)),
            row_spec,
            pl.BlockSpec((1, d, D_EXPERT), lambda i, e: (e, 0, 0)),
            pl.BlockSpec((1, d, D_EXPERT), lambda i, e: (e, 0, 0)),
            pl.BlockSpec((1, D_EXPERT, d), lambda i, e: (e, 0, 0)),
            pl.BlockSpec((1, d), lambda i, e: (0, 0)),
        ] + [mod(s) for s in range(MOE_SUB)],
        out_specs=row_spec,
        out_shape=jax.ShapeDtypeStruct((rows, d), F32),
        scratch_shapes=[pltpu.VMEM((tm, d), F32)],
        compiler_params=_cparams(("arbitrary", "arbitrary")),
        name="moe_residual",
    )(h2, comb, x, w1, w3, w2, final_g, *([mods] * MOE_SUB))


def kernel(x, c, ctx, c_ctx, w_ada, b_ada, norm1_g, norm2_g, w_in, a_ln_g, a_w_s, a_b_s, b_conv_w, b_A_log, b_dt_bias, b_norm_g, w_proj_a, w_proj_b, w_out, w_router, b_router, w_e1, w_e3, w_e2, final_g):
    batch, seq_len, d = x.shape
    ctx_len = ctx.shape[1]
    n_layers = w_ada.shape[0]
    assert d == D_MODEL and batch + 1 <= COND_ROWS
    assert ctx_len % ROW_TILE == 0 and seq_len % ROW_TILE == 0
    per_batch = ctx_len + seq_len
    rows = batch * per_batch
    assert rows % (MOE_SUB * ROW_TILE) == 0
    tiles_per_batch = per_batch // ROW_TILE
    ctx_tiles = ctx_len // ROW_TILE

    def mod_row(tile):
        return jnp.where(tile % tiles_per_batch < ctx_tiles, batch, tile // tiles_per_batch)

    xs = jnp.concatenate([ctx, x], axis=1).reshape(rows, d)
    cond = jnp.concatenate([c, c_ctx[None, :], jnp.zeros((COND_ROWS - batch - 1, d), F32)], axis=0)
    mods = _ada_call(cond, w_ada, b_ada).reshape(n_layers * 6 * COND_ROWS, 1, d)

    n_ab = 4 * N_HEADS
    ab_lo = 6 * d
    wr_t = w_router.T
    br = b_router.reshape(N_EXPERTS, 1)
    fg = final_g.reshape(1, d)
    for l in range(n_layers):
        w_main = jnp.concatenate([w_in[l][:, :ab_lo], w_in[l][:, ab_lo + n_ab:]], axis=1).astype(BF16)
        w_ab = jnp.pad(w_in[l][:, ab_lo:ab_lo + n_ab], ((0, 0), (0, 128 - n_ab)))
        u, v, q, k, vv, z, ga, gb, ab = _in_call(xs, mods, l, norm1_g[l].reshape(1, d), w_main, w_ab, mod_row)

        conv_w = jnp.pad(b_conv_w[l], ((0, 8 - CONV_TAPS), (0, 0)))
        zeros8 = jnp.zeros((N_HEADS,), F32)
        alog_row = jnp.pad(jnp.concatenate([b_A_log[l, 0], zeros8, b_A_log[l, 1], zeros8]), (0, 128 - n_ab)).reshape(1, 128)
        dtb_row = jnp.pad(jnp.concatenate([b_dt_bias[l, 0], zeros8, b_dt_bias[l, 1], zeros8]), (0, 128 - n_ab)).reshape(1, 128)
        qn, kn, vs, gates = _conv_call(q, k, vv, conv_w, ab, alog_row, dtb_row, tiles_per_batch, ctx_tiles)
        o2 = _gdn_call(qn, kn, vs, gates, batch, ctx_len, seq_len)

        bs = jnp.repeat(a_b_s[l].T, SGU_CHUNK, axis=1)
        bng = b_norm_g[l].reshape(1, HEAD_DIM)
        xs, h2, comb_t = _mix_call(
            xs, u, v, o2, z, ga, gb, mods, l, norm2_g[l].reshape(1, d), a_ln_g[l].reshape(1, d),
            a_w_s[l].astype(BF16), bs, bng,
            w_proj_a[l].astype(BF16), w_proj_b[l].astype(BF16), w_out[l].astype(BF16), wr_t, br, mod_row)
        comb = jnp.pad(comb_t.T, ((0, 0), (0, 128 - N_EXPERTS)))
        xs = _moe_call(h2, comb, xs, w_e1[l].astype(BF16), w_e3[l].astype(BF16), w_e2[l].astype(BF16),
                       fg, mods, l, mod_row, final=(l == n_layers - 1))
    return xs.reshape(batch, per_batch, d)[:, ctx_len:, :]
```

```python
import functools

import jax
import jax.numpy as jnp
from jax import lax
from jax.experimental import pallas as pl
from jax.experimental.pallas import tpu as pltpu

F32 = jnp.float32
BF16 = jnp.bfloat16
HIGHEST = lax.Precision.HIGHEST

EPS = 1e-6
D_MODEL = 1024
N_HEADS = 8
HEAD_DIM = 128
SGU_CHUNK = 128
SGU_GROUPS = 8
CONV_TAPS = 5
N_EXPERTS = 16
EXPERTS_PER_GROUP = 4
N_EXPERT_GROUPS = 4
D_EXPERT = 512
ROW_TILE = 256
GDN_CHUNK = 128
GDN_BLOCK = 2
MOE_SUB = 4
COND_ROWS = 8
HALO = 16
VMEM_LIMIT = 56 * 1024 * 1024


def _cparams(sem):
    return pltpu.CompilerParams(dimension_semantics=sem, vmem_limit_bytes=VMEM_LIMIT)


def _sigmoid(x):
    return 0.5 * jnp.tanh(0.5 * x) + 0.5


def _silu(x):
    return x * _sigmoid(x)


def _gelu_tanh(x):
    return 0.5 * x * (1.0 + jnp.tanh(0.7978845608028654 * (x + 0.044715 * (x * x * x))))


def _softplus(x):
    return jnp.maximum(x, 0.0) + jnp.log1p(jnp.exp(-jnp.abs(x)))


def _ada_kernel(cond_ref, w_ref, b_ref, o_ref):
    s = _silu(cond_ref[...])
    o_ref[0, 0] = jnp.dot(s, w_ref[0], precision=HIGHEST, preferred_element_type=F32) + b_ref[0, 0]


def _ada_call(cond, w_ada, b_ada):
    n_layers = w_ada.shape[0]
    d = D_MODEL
    return pl.pallas_call(
        _ada_kernel,
        grid=(n_layers, 6),
        in_specs=[
            pl.BlockSpec((COND_ROWS, d), lambda l, j: (0, 0)),
            pl.BlockSpec((1, d, d), lambda l, j: (l, 0, j)),
            pl.BlockSpec((1, 1, 1, d), lambda l, j: (l, j, 0, 0)),
        ],
        out_specs=pl.BlockSpec((1, 1, COND_ROWS, d), lambda l, j: (l, j, 0, 0)),
        out_shape=jax.ShapeDtypeStruct((n_layers, 6, COND_ROWS, d), F32),
        compiler_params=_cparams(("arbitrary", "arbitrary")),
        name="ada_params",
    )(cond, w_ada, b_ada.reshape(n_layers, 6, 1, d))


def _load_rows(x_refs, tile, n_ctx_tiles):
    if len(x_refs) == 1:
        return x_refs[0][...]
    return jnp.where(tile < n_ctx_tiles, x_refs[0][...], x_refs[1][...])


def _row_specs(xs, n_ctx_tiles, tile_off, d):
    if len(xs) == 1:
        return [pl.BlockSpec((ROW_TILE, d), lambda i: (i + tile_off, 0))]
    return [pl.BlockSpec((ROW_TILE, d), lambda i: (jnp.minimum(i + tile_off, n_ctx_tiles - 1), 0)),
            pl.BlockSpec((ROW_TILE, d), lambda i: (jnp.maximum(i + tile_off - n_ctx_tiles, 0), 0))]


def _in_kernel(n_src, n_ctx_tiles, *refs):
    x_refs = refs[:n_src]
    (sh_ref, sc_ref, g_ref, w_ref,
     u_ref, v_ref, q_ref, k_ref, vv_ref, z_ref, ga_ref, gb_ref, ab_ref) = refs[n_src:]
    x = _load_rows(x_refs, pl.program_id(0), n_ctx_tiles)
    ms = jnp.mean(x * x, axis=-1, keepdims=True)
    h = x * lax.rsqrt(ms + EPS) * g_ref[...]
    h = h * (1.0 + sc_ref[0]) + sh_ref[0]
    hb = h.astype(BF16)
    d = D_MODEL
    outs = (u_ref, v_ref, q_ref, k_ref, vv_ref, z_ref, ga_ref, gb_ref)
    for n, o_ref in enumerate(outs):
        o_ref[...] = jnp.dot(hb, w_ref[:, n * d:(n + 1) * d],
                             preferred_element_type=F32).astype(BF16)
    ab_ref[...] = jnp.dot(hb, w_ref[:, 8 * d:8 * d + 128], preferred_element_type=F32)


def _in_call(xs, mods, layer, norm_g, w_all, mod_row, n_ctx_tiles):
    d = D_MODEL
    rows = sum(a.shape[0] for a in xs)
    nt = rows // ROW_TILE
    sh_idx = (layer * 6 + 0) * COND_ROWS
    sc_idx = (layer * 6 + 1) * COND_ROWS
    row_spec = pl.BlockSpec((ROW_TILE, d), lambda i: (i, 0))
    out_shapes = [jax.ShapeDtypeStruct((rows, d), BF16)] * 8 + [jax.ShapeDtypeStruct((rows, 128), F32)]
    return pl.pallas_call(
        functools.partial(_in_kernel, len(xs), n_ctx_tiles),
        grid=(nt,),
        in_specs=_row_specs(xs, n_ctx_tiles, 0, d) + [
            pl.BlockSpec((1, 1, d), lambda i: (sh_idx + mod_row(i), 0, 0)),
            pl.BlockSpec((1, 1, d), lambda i: (sc_idx + mod_row(i), 0, 0)),
            pl.BlockSpec((1, d), lambda i: (0, 0)),
            pl.BlockSpec((d, 8 * d + 128), lambda i: (0, 0), pipeline_mode=pl.Buffered(1)),
        ],
        out_specs=[row_spec] * 8 + [pl.BlockSpec((ROW_TILE, 128), lambda i: (i, 0))],
        out_shape=out_shapes,
        compiler_params=_cparams(("arbitrary",)),
        name="norm_in_proj",
    )(*xs, mods, mods, norm_g, w_all)


def _conv_kernel(n_ctx_tiles, ctx_tiles, lat_tiles,
                 q_ref, qp_ref, qn_ref, k_ref, kp_ref, kn_ref, v_ref, vp_ref, vn_ref,
                 cw_ref, ab_ref, alog_ref, dtb_ref,
                 qo_ref, ko_ref, vo_ref, gate_ref, ext_ref):
    i = pl.program_id(0)
    j = jnp.where(i < n_ctx_tiles, i % ctx_tiles, (i - n_ctx_tiles) % lat_tiles)
    last = jnp.where(i < n_ctx_tiles, ctx_tiles - 1, lat_tiles - 1)
    has_prev = (j != 0).astype(F32)
    has_next = (j != last).astype(F32)
    pad = CONV_TAPS // 2
    srcs = ((q_ref, qp_ref, qn_ref, qo_ref, True, HEAD_DIM ** -0.5),
            (k_ref, kp_ref, kn_ref, ko_ref, True, 1.0),
            (v_ref, vp_ref, vn_ref, vo_ref, False, 1.0))
    for t, (m_ref, p_ref, n_ref, o_ref, l2, scale) in enumerate(srcs):
        ext_ref[0:8, :] = p_ref[HALO - 8:HALO, :].astype(F32) * has_prev
        ext_ref[8:8 + ROW_TILE, :] = m_ref[...].astype(F32)
        ext_ref[8 + ROW_TILE:16 + ROW_TILE, :] = n_ref[0:8, :].astype(F32) * has_next
        for h in range(N_HEADS):
            lanes = slice(h * HEAD_DIM, (h + 1) * HEAD_DIM)
            acc = None
            for tap in range(CONV_TAPS):
                w_row = cw_ref[tap:tap + 1, t * D_MODEL + h * HEAD_DIM:t * D_MODEL + (h + 1) * HEAD_DIM]
                start = 8 - pad + tap
                term = ext_ref[start:start + ROW_TILE, lanes] * w_row
                acc = term if acc is None else acc + term
            y = _silu(acc)
            if l2:
                ss = jnp.sum(y * y, axis=-1, keepdims=True)
                y = y * (lax.rsqrt(ss + EPS) * scale)
            o_ref[:, lanes] = y.astype(BF16)
    ab = ab_ref[...]
    lane = lax.broadcasted_iota(jnp.int32, ab.shape, 1)
    is_decay = ((lane // N_HEADS) % 2) == 0
    g = -jnp.exp(alog_ref[...]) * _softplus(ab + dtb_ref[...])
    gate = jnp.where(is_decay, g, _sigmoid(ab))
    gate = jnp.where(lane < 4 * N_HEADS, gate, 0.0)
    gate_ref[0] = gate
    gate_ref[1] = pltpu.roll(gate, 128 - 2 * N_HEADS, axis=1)


def _conv_call(q, k, v, conv_w, ab, alog_row, dtb_row, n_ctx_tiles, ctx_tiles, lat_tiles):
    rows, d = q.shape
    nt = rows // ROW_TILE
    per = ROW_TILE // HALO
    n_halo = rows // HALO
    main = pl.BlockSpec((ROW_TILE, d), lambda i: (i, 0))
    prev = pl.BlockSpec((HALO, d), lambda i: (jnp.maximum(i * per - 1, 0), 0))
    nxt = pl.BlockSpec((HALO, d), lambda i: (jnp.minimum((i + 1) * per, n_halo - 1), 0))
    small = pl.BlockSpec((1, 128), lambda i: (0, 0))
    return pl.pallas_call(
        functools.partial(_conv_kernel, n_ctx_tiles, ctx_tiles, lat_tiles),
        grid=(nt,),
        in_specs=[main, prev, nxt] * 3 + [
            pl.BlockSpec((8, 3 * d), lambda i: (0, 0)),
            pl.BlockSpec((ROW_TILE, 128), lambda i: (i, 0)),
            small, small,
        ],
        out_specs=[main, main, main, pl.BlockSpec((2, ROW_TILE, 128), lambda i: (0, i, 0))],
        out_shape=[jax.ShapeDtypeStruct((rows, d), BF16)] * 3 + [jax.ShapeDtypeStruct((2, rows, 128), F32)],
        scratch_shapes=[pltpu.VMEM((ROW_TILE + 16, d), F32)],
        compiler_params=_cparams(("arbitrary",)),
        name="conv_silu_gates",
    )(q, q, q, k, k, k, v, v, v, conv_w, ab, alog_row, dtb_row)


def _gdn_kernel(q_ref, k_ref, v_ref, gate_ref, o_ref, s_ref):
    fwd = pl.program_id(1) == 0

    @pl.when(pl.program_id(2) == 0)
    def _():
        s_ref[...] = jnp.zeros_like(s_ref)

    for n in range(GDN_BLOCK):
        start = jnp.where(fwd, n, GDN_BLOCK - 1 - n) * GDN_CHUNK
        _gdn_chunk(fwd, pl.ds(pl.multiple_of(start, GDN_CHUNK), GDN_CHUNK), q_ref, k_ref, v_ref, gate_ref, o_ref, s_ref)


def _gdn_chunk(fwd, rows, q_ref, k_ref, v_ref, gate_ref, o_ref, s_ref):
    c = GDN_CHUNK
    row = lax.broadcasted_iota(jnp.int32, (c, c), 0)
    col = lax.broadcasted_iota(jnp.int32, (c, c), 1)
    later = jnp.where(fwd, row, col)
    earlier = jnp.where(fwd, col, row)
    incl = later >= earlier
    strict = later > earlier
    eye = (row == col).astype(F32)

    gate = gate_ref[0, rows, :]
    gc_all = jnp.dot(incl.astype(F32), gate, precision=HIGHEST, preferred_element_type=F32)
    gtot_all = jnp.dot(jnp.ones((c, c), F32), gate, precision=HIGHEST, preferred_element_type=F32)
    gate_t = gate.T
    gc_t = gc_all.T
    gtot_t = gtot_all.T

    heads = range(N_HEADS)
    lanes = [slice(h * HEAD_DIM, (h + 1) * HEAD_DIM) for h in heads]
    nt_dims = (((1,), (1,)), ((), ()))
    q = [q_ref[rows, lanes[h]] for h in heads]
    k = [k_ref[rows, lanes[h]] for h in heads]
    v = [v_ref[rows, lanes[h]] for h in heads]
    gc_col = [jnp.broadcast_to(gc_all[:, h:h + 1], (c, c)) for h in heads]
    gc_row = [gc_t[h:h + 1, :] for h in heads]
    gtot_row = [gtot_t[h:h + 1, :] for h in heads]
    beta_row = [gate_t[N_HEADS + h:N_HEADS + h + 1, :] for h in heads]

    decay = [jnp.exp(jnp.where(incl, gc_col[h] - gc_row[h], -jnp.inf)) * beta_row[h] for h in heads]
    kq = [lax.dot_general(jnp.concatenate([k[h], q[h]], axis=0), k[h], nt_dims, preferred_element_type=F32)
          for h in heads]
    n_mat = [jnp.where(strict, kq[h][:c] * decay[h], 0.0) for h in heads]
    a_mat = [(kq[h][c:] * decay[h]).astype(BF16) for h in heads]
    a_b = [(eye + n_mat[h]).astype(BF16) for h in heads]
    t_inv = [eye - n_mat[h] for h in heads]
    for _ in range(6):
        tb = [t_inv[h].astype(BF16) for h in heads]
        err = [(eye - jnp.dot(a_b[h], tb[h], preferred_element_type=F32)).astype(BF16) for h in heads]
        t_inv = [t_inv[h] + jnp.dot(tb[h], err[h], preferred_element_type=F32) for h in heads]
    u_val = [jnp.dot(t_inv[h].astype(BF16), v[h], preferred_element_type=F32) for h in heads]
    w_key = [jnp.dot((t_inv[h] * jnp.exp(gc_row[h])).astype(BF16), k[h], preferred_element_type=F32)
             for h in heads]

    s_old = [s_ref[h] for h in heads]
    sb = [s_old[h].astype(BF16) for h in heads]
    wq_s = [jnp.dot(jnp.concatenate([w_key[h].astype(BF16), q[h]], axis=0), sb[h], preferred_element_type=F32)
            for h in heads]
    v_new = [(u_val[h] - wq_s[h][:c]).astype(BF16) for h in heads]
    o_val = [jnp.exp(gc_col[h]) * wq_s[h][c:] + jnp.dot(a_mat[h], v_new[h], preferred_element_type=F32)
             for h in heads]
    kd_t = [(k[h].astype(F32).T * (jnp.exp(gtot_row[h] - gc_row[h]) * beta_row[h])).astype(BF16) for h in heads]
    for h in heads:
        s_ref[h] = s_old[h] * jnp.exp(gtot_row[h]) + jnp.dot(kd_t[h], v_new[h], preferred_element_type=F32)
        o_ref[0, rows, lanes[h]] = o_val[h].astype(BF16)


def _gdn_call(q, k, v, gates, batch, ctx_len, seq_len):
    rows, d = q.shape
    c = GDN_BLOCK * GDN_CHUNK
    assert ctx_len % c == 0 and seq_len % c == 0
    n_ctx = ctx_len // c
    n_lat = seq_len // c
    n_steps = n_ctx + n_lat

    def chunk(b, dr, s):
        rev = jnp.where(s < n_ctx, n_ctx - 1 - s, n_ctx + n_steps - 1 - s)
        pos = jnp.where(dr == 0, s, rev)
        return jnp.where(pos < n_ctx, b * n_ctx + pos, batch * n_ctx + b * n_lat + pos - n_ctx)

    blk = pl.BlockSpec((c, d), lambda b, dr, s: (chunk(b, dr, s), 0))
    return pl.pallas_call(
        _gdn_kernel,
        grid=(batch, 2, n_steps),
        in_specs=[blk, blk, blk, pl.BlockSpec((1, c, 128), lambda b, dr, s: (dr, chunk(b, dr, s), 0))],
        out_specs=pl.BlockSpec((1, c, d), lambda b, dr, s: (dr, chunk(b, dr, s), 0)),
        out_shape=jax.ShapeDtypeStruct((2, rows, d), BF16),
        scratch_shapes=[pltpu.VMEM((N_HEADS, HEAD_DIM, HEAD_DIM), F32)],
        compiler_params=_cparams(("arbitrary", "arbitrary", "arbitrary")),
        name="gated_delta",
    )(q, k, v, gates)


def _mix_kernel(n_src, n_ctx_tiles, tile_off, *refs):
    x_refs = refs[:n_src]
    (u_ref, v_ref, of_ref, ob_ref, z_ref, ga_ref, gb_ref,
     g1_ref, sh2_ref, sc2_ref, n2g_ref, lng_ref, ws_ref, bs_ref, bng_ref,
     wpa_ref, wpb_ref, wout_ref, wr_ref, br_ref,
     xo_ref, h2_ref, comb_ref, sa_ref, sb_ref) = refs[n_src:]
    d = D_MODEL
    ug = _gelu_tanh(u_ref[...].astype(F32))
    vg = _gelu_tanh(v_ref[...].astype(F32))
    mu = jnp.mean(vg, axis=-1, keepdims=True)
    vc = vg - mu
    var = jnp.mean(vc * vc, axis=-1, keepdims=True)
    vn = (vc * lax.rsqrt(var + EPS) * lng_ref[...]).astype(BF16)
    for ch in range(ROW_TILE // SGU_CHUNK):
        rws = slice(ch * SGU_CHUNK, (ch + 1) * SGU_CHUNK)
        for g in range(SGU_GROUPS):
            lanes = slice(g * 128, (g + 1) * 128)
            mixed = jnp.dot(ws_ref[g], vn[rws, lanes], preferred_element_type=F32) + bs_ref[:, lanes]
            sa_ref[rws, lanes] = (ug[rws, lanes] * mixed).astype(BF16)
    y_a = jnp.dot(sa_ref[...], wpa_ref[...], preferred_element_type=F32)
    for h in range(N_HEADS):
        lanes = slice(h * HEAD_DIM, (h + 1) * HEAD_DIM)
        o = of_ref[0, :, lanes].astype(F32) + ob_ref[0, :, lanes].astype(F32)
        ms = jnp.mean(o * o, axis=-1, keepdims=True)
        o = o * lax.rsqrt(ms + EPS) * bng_ref[...]
        sb_ref[:, lanes] = (o * _silu(z_ref[:, lanes].astype(F32))).astype(BF16)
    y_b = jnp.dot(sb_ref[...], wpb_ref[...], preferred_element_type=F32)
    merged = _sigmoid(ga_ref[...].astype(F32)) * y_a + _sigmoid(gb_ref[...].astype(F32)) * y_b
    y = jnp.dot(merged.astype(BF16), wout_ref[...], preferred_element_type=F32)
    xn = _load_rows(x_refs, pl.program_id(0) + tile_off, n_ctx_tiles) + g1_ref[0] * y
    xo_ref[...] = xn
    ms = jnp.mean(xn * xn, axis=-1, keepdims=True)
    h2 = xn * lax.rsqrt(ms + EPS) * n2g_ref[...]
    h2 = h2 * (1.0 + sc2_ref[0]) + sh2_ref[0]
    h2_ref[...] = h2.astype(BF16)
    logits = lax.dot_general(wr_ref[...], h2, (((1,), (1,)), ((), ())),
                             precision=HIGHEST, preferred_element_type=F32)
    scores = _sigmoid(logits)
    sel = scores + br_ref[...]
    srow = [sel[e:e + 1, :] for e in range(N_EXPERTS)]
    grp = []
    for g in range(N_EXPERT_GROUPS):
        m = srow[4 * g:4 * g + 4]
        best2 = None
        for a in range(4):
            for b in range(a + 1, 4):
                pair = m[a] + m[b]
                best2 = pair if best2 is None else jnp.maximum(best2, pair)
        grp.append(best2)
    best_val = grp[0]
    best_idx = jnp.zeros_like(best_val, dtype=jnp.int32)
    for g in range(1, N_EXPERT_GROUPS):
        better = grp[g] > best_val
        best_val = jnp.where(better, grp[g], best_val)
        best_idx = jnp.where(better, g, best_idx)
    picked = []
    for e in range(N_EXPERTS):
        g = e // EXPERTS_PER_GROUP
        rank = jnp.zeros_like(best_idx)
        for o_e in range(4 * g, 4 * g + 4):
            if o_e == e:
                continue
            ahead = (srow[o_e] > srow[e]) if o_e > e else (srow[o_e] >= srow[e])
            rank = rank + ahead.astype(jnp.int32)
        chosen = jnp.logical_and(best_idx == g, rank < 2)
        picked.append(jnp.where(chosen, scores[e:e + 1, :], 0.0))
    total = picked[0]
    for e in range(1, N_EXPERTS):
        total = total + picked[e]
    inv = 1.0 / total
    for e in range(N_EXPERTS):
        comb_ref[e:e + 1, :] = picked[e] * inv


def _mix_call(xs, u, v, o2, z, ga, gb, mods, layer, norm2_g, ln_g, ws, bs, bng,
              wpa, wpb, wout, wr_t, br, mod_row, n_ctx_tiles, tile_off):
    d = D_MODEL
    nt = sum(a.shape[0] for a in xs) // ROW_TILE - tile_off
    rows = nt * ROW_TILE
    g1_idx = (layer * 6 + 2) * COND_ROWS
    sh2_idx = (layer * 6 + 3) * COND_ROWS
    sc2_idx = (layer * 6 + 4) * COND_ROWS
    row_spec = pl.BlockSpec((ROW_TILE, d), lambda i: (i + tile_off, 0))
    out_spec = pl.BlockSpec((ROW_TILE, d), lambda i: (i, 0))
    vec = pl.BlockSpec((1, d), lambda i: (0, 0))
    wspec = pl.BlockSpec((d, d), lambda i: (0, 0))

    def mod(idx):
        return pl.BlockSpec((1, 1, d), lambda i: (idx + mod_row(i + tile_off), 0, 0))

    return pl.pallas_call(
        functools.partial(_mix_kernel, len(xs), n_ctx_tiles, tile_off),
        grid=(nt,),
        in_specs=_row_specs(xs, n_ctx_tiles, tile_off, d) + [
            row_spec, row_spec,
            pl.BlockSpec((1, ROW_TILE, d), lambda i: (0, i + tile_off, 0)),
            pl.BlockSpec((1, ROW_TILE, d), lambda i: (1, i + tile_off, 0)),
            row_spec, row_spec, row_spec,
            mod(g1_idx), mod(sh2_idx), mod(sc2_idx),
            vec, vec,
            pl.BlockSpec((SGU_GROUPS, SGU_CHUNK, SGU_CHUNK), lambda i: (0, 0, 0)),
            pl.BlockSpec((SGU_CHUNK, d), lambda i: (0, 0)),
            pl.BlockSpec((1, HEAD_DIM), lambda i: (0, 0)),
            wspec, wspec, wspec,
            pl.BlockSpec((N_EXPERTS, d), lambda i: (0, 0)),
            pl.BlockSpec((N_EXPERTS, 1), lambda i: (0, 0)),
        ],
        out_specs=[out_spec, out_spec, pl.BlockSpec((N_EXPERTS, ROW_TILE), lambda i: (0, i))],
        out_shape=[jax.ShapeDtypeStruct((rows, d), F32), jax.ShapeDtypeStruct((rows, d), BF16),
                   jax.ShapeDtypeStruct((N_EXPERTS, rows), F32)],
        scratch_shapes=[pltpu.VMEM((ROW_TILE, d), BF16), pltpu.VMEM((ROW_TILE, d), BF16)],
        compiler_params=_cparams(("arbitrary",)),
        name="mix_merge_router",
    )(*xs, u, v, o2, o2, z, ga, gb, mods, mods, mods, norm2_g, ln_g, ws, bs, bng,
      wpa, wpb, wout, wr_t, br)


def _moe_kernel(final, h_ref, comb_ref, x_ref, w1_ref, w3_ref, w2_ref, fg_ref, *rest):
    g2_refs = rest[:MOE_SUB]
    o_ref = rest[MOE_SUB]
    acc_ref = rest[MOE_SUB + 1]
    e = pl.program_id(1)

    @pl.when(e == 0)
    def _():
        acc_ref[...] = jnp.zeros_like(acc_ref)

    h = h_ref[...]
    a = jnp.dot(h, w1_ref[0].astype(BF16), preferred_element_type=F32)
    b = jnp.dot(h, w3_ref[0].astype(BF16), preferred_element_type=F32)
    comb = comb_ref[...]
    lane = lax.broadcasted_iota(jnp.int32, comb.shape, 1)
    c_col = jnp.sum(jnp.where(lane == e, comb, 0.0), axis=-1, keepdims=True)
    hid = (_silu(a) * b * c_col).astype(BF16)
    acc_ref[...] += jnp.dot(hid, w2_ref[0].astype(BF16), preferred_element_type=F32)

    @pl.when(e == N_EXPERTS - 1)
    def _():
        for s in range(MOE_SUB):
            rws = slice(s * ROW_TILE, (s + 1) * ROW_TILE)
            xn = x_ref[rws, :] + g2_refs[s][0] * acc_ref[rws, :]
            if final:
                ms = jnp.mean(xn * xn, axis=-1, keepdims=True)
                xn = xn * lax.rsqrt(ms + EPS) * fg_ref[...]
            o_ref[rws, :] = xn


def _moe_call(h2, comb, x, w1, w3, w2, final_g, mods, layer, mod_row, tile_off, final):
    rows, d = x.shape
    tm = MOE_SUB * ROW_TILE
    nt = rows // tm
    g2_idx = (layer * 6 + 5) * COND_ROWS
    row_spec = pl.BlockSpec((tm, d), lambda i, e: (i, 0))

    def mod(s):
        return pl.BlockSpec((1, 1, d), lambda i, e: (g2_idx + mod_row(i * MOE_SUB + s + tile_off), 0, 0))

    return pl.pallas_call(
        functools.partial(_moe_kernel, final),
        grid=(nt, N_EXPERTS),
        in_specs=[
            row_spec,
            pl.BlockSpec((tm, 128), lambda i, e: (i, 0)),
            row_spec,
            pl.BlockSpec((1, d, D_EXPERT), lambda i, e: (e, 0, 0)),
            pl.BlockSpec((1, d, D_EXPERT), lambda i, e: (e, 0, 0)),
            pl.BlockSpec((1, D_EXPERT, d), lambda i, e: (e, 0, 0)),
            pl.BlockSpec((1, d), lambda i, e: (0, 0)),
        ] + [mod(s) for s in range(MOE_SUB)],
        out_specs=row_spec,
        out_shape=jax.ShapeDtypeStruct((rows, d), F32),
        scratch_shapes=[pltpu.VMEM((tm, d), F32)],
        compiler_params=_cparams(("arbitrary", "arbitrary")),
        name="moe_residual",
    )(h2, comb, x, w1, w3, w2, final_g, *([mods] * MOE_SUB))


def kernel(x, c, ctx, c_ctx, w_ada, b_ada, norm1_g, norm2_g, w_in, a_ln_g, a_w_s, a_b_s, b_conv_w, b_A_log, b_dt_bias, b_norm_g, w_proj_a, w_proj_b, w_out, w_router, b_router, w_e1, w_e3, w_e2, final_g):
    batch, seq_len, d = x.shape
    ctx_len = ctx.shape[1]
    n_layers = w_ada.shape[0]
    assert d == D_MODEL and batch + 1 <= COND_ROWS
    assert ctx_len % ROW_TILE == 0 and seq_len % ROW_TILE == 0
    moe_tile = MOE_SUB * ROW_TILE
    assert (batch * ctx_len) % moe_tile == 0 and (batch * seq_len) % moe_tile == 0
    ctx_tiles = ctx_len // ROW_TILE
    lat_tiles = seq_len // ROW_TILE
    n_ctx_tiles = batch * ctx_tiles

    def mod_row(tile):
        return jnp.where(tile < n_ctx_tiles, batch, (tile - n_ctx_tiles) // lat_tiles)

    xs = (ctx.reshape(batch * ctx_len, d), x.reshape(batch * seq_len, d))
    cond = jnp.concatenate([c, c_ctx[None, :], jnp.zeros((COND_ROWS - batch - 1, d), F32)], axis=0)
    mods = _ada_call(cond, w_ada, b_ada).reshape(n_layers * 6 * COND_ROWS, 1, d)

    n_ab = 4 * N_HEADS
    ab_lo = 6 * d
    wr_t = w_router.T
    br = b_router.reshape(N_EXPERTS, 1)
    fg = final_g.reshape(1, d)
    for l in range(n_layers):
        w_all = jnp.concatenate([w_in[l][:, :ab_lo], w_in[l][:, ab_lo + n_ab:], w_in[l][:, ab_lo:ab_lo + n_ab],
                                 jnp.zeros((d, 128 - n_ab), F32)], axis=1).astype(BF16)
        u, v, q, k, vv, z, ga, gb, ab = _in_call(xs, mods, l, norm1_g[l].reshape(1, d), w_all, mod_row, n_ctx_tiles)

        conv_w = jnp.pad(b_conv_w[l], ((0, 8 - CONV_TAPS), (0, 0)))
        zeros8 = jnp.zeros((N_HEADS,), F32)
        alog_row = jnp.pad(jnp.concatenate([b_A_log[l, 0], zeros8, b_A_log[l, 1], zeros8]), (0, 128 - n_ab)).reshape(1, 128)
        dtb_row = jnp.pad(jnp.concatenate([b_dt_bias[l, 0], zeros8, b_dt_bias[l, 1], zeros8]), (0, 128 - n_ab)).reshape(1, 128)
        qn, kn, vs, gates = _conv_call(q, k, vv, conv_w, ab, alog_row, dtb_row, n_ctx_tiles, ctx_tiles, lat_tiles)
        o2 = _gdn_call(qn, kn, vs, gates, batch, ctx_len, seq_len)

        bs = jnp.repeat(a_b_s[l].T, SGU_CHUNK, axis=1)
        bng = b_norm_g[l].reshape(1, HEAD_DIM)
        last = l == n_layers - 1
        tile_off = n_ctx_tiles if last else 0
        x_new, h2, comb_t = _mix_call(
            xs, u, v, o2, z, ga, gb, mods, l, norm2_g[l].reshape(1, d), a_ln_g[l].reshape(1, d),
            a_w_s[l].astype(BF16), bs, bng,
            w_proj_a[l].astype(BF16), w_proj_b[l].astype(BF16), w_out[l].astype(BF16), wr_t, br, mod_row,
            n_ctx_tiles, tile_off)
        comb = jnp.pad(comb_t.T, ((0, 0), (0, 128 - N_EXPERTS)))
        xs = (_moe_call(h2, comb, x_new, w_e1[l], w_e3[l], w_e2[l],
                        fg, mods, l, mod_row, tile_off, final=last),)
    return xs[0].reshape(batch, seq_len, d)
```

```python
import functools

import jax
import jax.numpy as jnp
from jax import lax
from jax.experimental import pallas as pl
from jax.experimental.pallas import tpu as pltpu

F32 = jnp.float32
BF16 = jnp.bfloat16
HIGHEST = lax.Precision.HIGHEST

EPS = 1e-6
D_MODEL = 1024
N_HEADS = 8
HEAD_DIM = 128
SGU_CHUNK = 128
SGU_GROUPS = 8
CONV_TAPS = 5
N_EXPERTS = 16
EXPERTS_PER_GROUP = 4
N_EXPERT_GROUPS = 4
D_EXPERT = 512
ROW_TILE = 256
GDN_CHUNK = 128
GDN_BLOCK = 2
SORT_TILE = 512
SEG_ALIGN = 16
SORT_CAP = 1280
MOE_BLOCK = 512
SEG_BITS = (32, 16, 8, 4, 2, 1)
COND_ROWS = 8
HALO = 16
VMEM_LIMIT = 56 * 1024 * 1024


def _cparams(sem):
    return pltpu.CompilerParams(dimension_semantics=sem, vmem_limit_bytes=VMEM_LIMIT)


def _sigmoid(x):
    return 0.5 * jnp.tanh(0.5 * x) + 0.5


def _silu(x):
    return x * _sigmoid(x)


def _gelu_tanh(x):
    return 0.5 * x * (1.0 + jnp.tanh(0.7978845608028654 * (x + 0.044715 * (x * x * x))))


def _softplus(x):
    return jnp.maximum(x, 0.0) + jnp.log1p(jnp.exp(-jnp.abs(x)))


def _ada_kernel(cond_ref, w_ref, b_ref, o_ref):
    s = _silu(cond_ref[...])
    o_ref[0, 0] = jnp.dot(s, w_ref[0], precision=HIGHEST, preferred_element_type=F32) + b_ref[0, 0]


def _ada_call(cond, w_ada, b_ada):
    n_layers = w_ada.shape[0]
    d = D_MODEL
    return pl.pallas_call(
        _ada_kernel,
        grid=(n_layers, 6),
        in_specs=[
            pl.BlockSpec((COND_ROWS, d), lambda l, j: (0, 0)),
            pl.BlockSpec((1, d, d), lambda l, j: (l, 0, j)),
            pl.BlockSpec((1, 1, 1, d), lambda l, j: (l, j, 0, 0)),
        ],
        out_specs=pl.BlockSpec((1, 1, COND_ROWS, d), lambda l, j: (l, j, 0, 0)),
        out_shape=jax.ShapeDtypeStruct((n_layers, 6, COND_ROWS, d), F32),
        compiler_params=_cparams(("arbitrary", "arbitrary")),
        name="ada_params",
    )(cond, w_ada, b_ada.reshape(n_layers, 6, 1, d))


def _load_rows(x_refs, tile, n_ctx_tiles):
    if len(x_refs) == 1:
        return x_refs[0][...]
    return jnp.where(tile < n_ctx_tiles, x_refs[0][...], x_refs[1][...])


def _row_specs(xs, n_ctx_tiles, tile_off, d):
    if len(xs) == 1:
        return [pl.BlockSpec((ROW_TILE, d), lambda i: (i + tile_off, 0))]
    return [pl.BlockSpec((ROW_TILE, d), lambda i: (jnp.minimum(i + tile_off, n_ctx_tiles - 1), 0)),
            pl.BlockSpec((ROW_TILE, d), lambda i: (jnp.maximum(i + tile_off - n_ctx_tiles, 0), 0))]


def _in_kernel(n_src, n_ctx_tiles, *refs):
    x_refs = refs[:n_src]
    (sh_ref, sc_ref, g_ref, w_ref,
     u_ref, v_ref, q_ref, k_ref, vv_ref, z_ref, ga_ref, gb_ref, ab_ref) = refs[n_src:]
    x = _load_rows(x_refs, pl.program_id(0), n_ctx_tiles)
    ms = jnp.mean(x * x, axis=-1, keepdims=True)
    h = x * lax.rsqrt(ms + EPS) * g_ref[...]
    h = h * (1.0 + sc_ref[0]) + sh_ref[0]
    hb = h.astype(BF16)
    d = D_MODEL
    outs = (u_ref, v_ref, q_ref, k_ref, vv_ref, z_ref, ga_ref, gb_ref)
    for n, o_ref in enumerate(outs):
        o_ref[...] = jnp.dot(hb, w_ref[:, n * d:(n + 1) * d],
                             preferred_element_type=F32).astype(BF16)
    ab_ref[...] = jnp.dot(hb, w_ref[:, 8 * d:8 * d + 128], preferred_element_type=F32)


def _in_call(xs, mods, layer, norm_g, w_all, mod_row, n_ctx_tiles):
    d = D_MODEL
    rows = sum(a.shape[0] for a in xs)
    nt = rows // ROW_TILE
    sh_idx = (layer * 6 + 0) * COND_ROWS
    sc_idx = (layer * 6 + 1) * COND_ROWS
    row_spec = pl.BlockSpec((ROW_TILE, d), lambda i: (i, 0))
    out_shapes = [jax.ShapeDtypeStruct((rows, d), BF16)] * 8 + [jax.ShapeDtypeStruct((rows, 128), F32)]
    return pl.pallas_call(
        functools.partial(_in_kernel, len(xs), n_ctx_tiles),
        grid=(nt,),
        in_specs=_row_specs(xs, n_ctx_tiles, 0, d) + [
            pl.BlockSpec((1, 1, d), lambda i: (sh_idx + mod_row(i), 0, 0)),
            pl.BlockSpec((1, 1, d), lambda i: (sc_idx + mod_row(i), 0, 0)),
            pl.BlockSpec((1, d), lambda i: (0, 0)),
            pl.BlockSpec((d, 8 * d + 128), lambda i: (0, 0), pipeline_mode=pl.Buffered(1)),
        ],
        out_specs=[row_spec] * 8 + [pl.BlockSpec((ROW_TILE, 128), lambda i: (i, 0))],
        out_shape=out_shapes,
        compiler_params=_cparams(("arbitrary",)),
        name="norm_in_proj",
    )(*xs, mods, mods, norm_g, w_all)


def _conv_kernel(n_ctx_tiles, ctx_tiles, lat_tiles,
                 q_ref, qp_ref, qn_ref, k_ref, kp_ref, kn_ref, v_ref, vp_ref, vn_ref,
                 cw_ref, ab_ref, alog_ref, dtb_ref,
                 qo_ref, ko_ref, vo_ref, gate_ref, ext_ref):
    i = pl.program_id(0)
    j = jnp.where(i < n_ctx_tiles, i % ctx_tiles, (i - n_ctx_tiles) % lat_tiles)
    last = jnp.where(i < n_ctx_tiles, ctx_tiles - 1, lat_tiles - 1)
    has_prev = (j != 0).astype(F32)
    has_next = (j != last).astype(F32)
    pad = CONV_TAPS // 2
    srcs = ((q_ref, qp_ref, qn_ref, qo_ref, True, HEAD_DIM ** -0.5),
            (k_ref, kp_ref, kn_ref, ko_ref, True, 1.0),
            (v_ref, vp_ref, vn_ref, vo_ref, False, 1.0))
    for t, (m_ref, p_ref, n_ref, o_ref, l2, scale) in enumerate(srcs):
        ext_ref[0:8, :] = p_ref[HALO - 8:HALO, :].astype(F32) * has_prev
        ext_ref[8:8 + ROW_TILE, :] = m_ref[...].astype(F32)
        ext_ref[8 + ROW_TILE:16 + ROW_TILE, :] = n_ref[0:8, :].astype(F32) * has_next
        for h in range(N_HEADS):
            lanes = slice(h * HEAD_DIM, (h + 1) * HEAD_DIM)
            acc = None
            for tap in range(CONV_TAPS):
                w_row = cw_ref[tap:tap + 1, t * D_MODEL + h * HEAD_DIM:t * D_MODEL + (h + 1) * HEAD_DIM]
                start = 8 - pad + tap
                term = ext_ref[start:start + ROW_TILE, lanes] * w_row
                acc = term if acc is None else acc + term
            y = _silu(acc)
            if l2:
                ss = jnp.sum(y * y, axis=-1, keepdims=True)
                y = y * (lax.rsqrt(ss + EPS) * scale)
            o_ref[:, lanes] = y.astype(BF16)
    ab = ab_ref[...]
    lane = lax.broadcasted_iota(jnp.int32, ab.shape, 1)
    is_decay = ((lane // N_HEADS) % 2) == 0
    g = -jnp.exp(alog_ref[...]) * _softplus(ab + dtb_ref[...])
    gate = jnp.where(is_decay, g, _sigmoid(ab))
    gate = jnp.where(lane < 4 * N_HEADS, gate, 0.0)
    gate_ref[0] = gate
    gate_ref[1] = pltpu.roll(gate, 128 - 2 * N_HEADS, axis=1)


def _conv_call(q, k, v, conv_w, ab, alog_row, dtb_row, n_ctx_tiles, ctx_tiles, lat_tiles):
    rows, d = q.shape
    nt = rows // ROW_TILE
    per = ROW_TILE // HALO
    n_halo = rows // HALO
    main = pl.BlockSpec((ROW_TILE, d), lambda i: (i, 0))
    prev = pl.BlockSpec((HALO, d), lambda i: (jnp.maximum(i * per - 1, 0), 0))
    nxt = pl.BlockSpec((HALO, d), lambda i: (jnp.minimum((i + 1) * per, n_halo - 1), 0))
    small = pl.BlockSpec((1, 128), lambda i: (0, 0))
    return pl.pallas_call(
        functools.partial(_conv_kernel, n_ctx_tiles, ctx_tiles, lat_tiles),
        grid=(nt,),
        in_specs=[main, prev, nxt] * 3 + [
            pl.BlockSpec((8, 3 * d), lambda i: (0, 0)),
            pl.BlockSpec((ROW_TILE, 128), lambda i: (i, 0)),
            small, small,
        ],
        out_specs=[main, main, main, pl.BlockSpec((2, ROW_TILE, 128), lambda i: (0, i, 0))],
        out_shape=[jax.ShapeDtypeStruct((rows, d), BF16)] * 3 + [jax.ShapeDtypeStruct((2, rows, 128), F32)],
        scratch_shapes=[pltpu.VMEM((ROW_TILE + 16, d), F32)],
        compiler_params=_cparams(("arbitrary",)),
        name="conv_silu_gates",
    )(q, q, q, k, k, k, v, v, v, conv_w, ab, alog_row, dtb_row)


def _gdn_kernel(q_ref, k_ref, v_ref, gate_ref, o_ref, s_ref):
    fwd = pl.program_id(1) == 0

    @pl.when(pl.program_id(2) == 0)
    def _():
        s_ref[...] = jnp.zeros_like(s_ref)

    for n in range(GDN_BLOCK):
        start = jnp.where(fwd, n, GDN_BLOCK - 1 - n) * GDN_CHUNK
        _gdn_chunk(fwd, pl.ds(pl.multiple_of(start, GDN_CHUNK), GDN_CHUNK), q_ref, k_ref, v_ref, gate_ref, o_ref, s_ref)


def _gdn_chunk(fwd, rows, q_ref, k_ref, v_ref, gate_ref, o_ref, s_ref):
    c = GDN_CHUNK
    row = lax.broadcasted_iota(jnp.int32, (c, c), 0)
    col = lax.broadcasted_iota(jnp.int32, (c, c), 1)
    later = jnp.where(fwd, row, col)
    earlier = jnp.where(fwd, col, row)
    incl = later >= earlier
    strict = later > earlier
    eye = (row == col).astype(F32)

    gate = gate_ref[0, rows, :]
    gc_all = jnp.dot(incl.astype(F32), gate, precision=HIGHEST, preferred_element_type=F32)
    gtot_all = jnp.dot(jnp.ones((c, c), F32), gate, precision=HIGHEST, preferred_element_type=F32)
    gate_t = gate.T
    gc_t = gc_all.T
    gtot_t = gtot_all.T

    heads = range(N_HEADS)
    lanes = [slice(h * HEAD_DIM, (h + 1) * HEAD_DIM) for h in heads]
    nt_dims = (((1,), (1,)), ((), ()))
    q = [q_ref[rows, lanes[h]] for h in heads]
    k = [k_ref[rows, lanes[h]] for h in heads]
    v = [v_ref[rows, lanes[h]] for h in heads]
    gc_col = [jnp.broadcast_to(gc_all[:, h:h + 1], (c, c)) for h in heads]
    gc_row = [gc_t[h:h + 1, :] for h in heads]
    gtot_row = [gtot_t[h:h + 1, :] for h in heads]
    beta_row = [gate_t[N_HEADS + h:N_HEADS + h + 1, :] for h in heads]

    decay = [jnp.exp(jnp.where(incl, gc_col[h] - gc_row[h], -jnp.inf)) * beta_row[h] for h in heads]
    kq = [lax.dot_general(jnp.concatenate([k[h], q[h]], axis=0), k[h], nt_dims, preferred_element_type=F32)
          for h in heads]
    n_mat = [jnp.where(strict, kq[h][:c] * decay[h], 0.0) for h in heads]
    a_mat = [(kq[h][c:] * decay[h]).astype(BF16) for h in heads]
    a_b = [(eye + n_mat[h]).astype(BF16) for h in heads]
    t_inv = [eye - n_mat[h] for h in heads]
    for _ in range(6):
        tb = [t_inv[h].astype(BF16) for h in heads]
        err = [(eye - jnp.dot(a_b[h], tb[h], preferred_element_type=F32)).astype(BF16) for h in heads]
        t_inv = [t_inv[h] + jnp.dot(tb[h], err[h], preferred_element_type=F32) for h in heads]
    u_val = [jnp.dot(t_inv[h].astype(BF16), v[h], preferred_element_type=F32) for h in heads]
    w_key = [jnp.dot((t_inv[h] * jnp.exp(gc_row[h])).astype(BF16), k[h], preferred_element_type=F32)
             for h in heads]

    s_old = [s_ref[h] for h in heads]
    sb = [s_old[h].astype(BF16) for h in heads]
    wq_s = [jnp.dot(jnp.concatenate([w_key[h].astype(BF16), q[h]], axis=0), sb[h], preferred_element_type=F32)
            for h in heads]
    v_new = [(u_val[h] - wq_s[h][:c]).astype(BF16) for h in heads]
    o_val = [jnp.exp(gc_col[h]) * wq_s[h][c:] + jnp.dot(a_mat[h], v_new[h], preferred_element_type=F32)
             for h in heads]
    kd_t = [(k[h].astype(F32).T * (jnp.exp(gtot_row[h] - gc_row[h]) * beta_row[h])).astype(BF16) for h in heads]
    for h in heads:
        s_ref[h] = s_old[h] * jnp.exp(gtot_row[h]) + jnp.dot(kd_t[h], v_new[h], preferred_element_type=F32)
        o_ref[0, rows, lanes[h]] = o_val[h].astype(BF16)


def _gdn_call(q, k, v, gates, batch, ctx_len, seq_len):
    rows, d = q.shape
    c = GDN_BLOCK * GDN_CHUNK
    assert ctx_len % c == 0 and seq_len % c == 0
    n_ctx = ctx_len // c
    n_lat = seq_len // c
    n_steps = n_ctx + n_lat

    def chunk(b, dr, s):
        rev = jnp.where(s < n_ctx, n_ctx - 1 - s, n_ctx + n_steps - 1 - s)
        pos = jnp.where(dr == 0, s, rev)
        return jnp.where(pos < n_ctx, b * n_ctx + pos, batch * n_ctx + b * n_lat + pos - n_ctx)

    blk = pl.BlockSpec((c, d), lambda b, dr, s: (chunk(b, dr, s), 0))
    return pl.pallas_call(
        _gdn_kernel,
        grid=(batch, 2, n_steps),
        in_specs=[blk, blk, blk, pl.BlockSpec((1, c, 128), lambda b, dr, s: (dr, chunk(b, dr, s), 0))],
        out_specs=pl.BlockSpec((1, c, d), lambda b, dr, s: (dr, chunk(b, dr, s), 0)),
        out_shape=jax.ShapeDtypeStruct((2, rows, d), BF16),
        scratch_shapes=[pltpu.VMEM((N_HEADS, HEAD_DIM, HEAD_DIM), F32)],
        compiler_params=_cparams(("arbitrary", "arbitrary", "arbitrary")),
        name="gated_delta",
    )(q, k, v, gates)


def _mix_kernel(n_src, n_ctx_tiles, tile_off, *refs):
    x_refs = refs[:n_src]
    (u_ref, v_ref, of_ref, ob_ref, z_ref, ga_ref, gb_ref,
     g1_ref, sh2_ref, sc2_ref, n2g_ref, lng_ref, ws_ref, bs_ref, bng_ref,
     wpa_ref, wpb_ref, wout_ref, wr_ref, br_ref,
     xo_ref, h2_ref, comb_ref, cnt_ref, sa_ref, sb_ref) = refs[n_src:]
    d = D_MODEL
    ug = _gelu_tanh(u_ref[...].astype(F32))
    vg = _gelu_tanh(v_ref[...].astype(F32))
    mu = jnp.mean(vg, axis=-1, keepdims=True)
    vc = vg - mu
    var = jnp.mean(vc * vc, axis=-1, keepdims=True)
    vn = (vc * lax.rsqrt(var + EPS) * lng_ref[...]).astype(BF16)
    for ch in range(ROW_TILE // SGU_CHUNK):
        rws = slice(ch * SGU_CHUNK, (ch + 1) * SGU_CHUNK)
        for g in range(SGU_GROUPS):
            lanes = slice(g * 128, (g + 1) * 128)
            mixed = jnp.dot(ws_ref[g], vn[rws, lanes], preferred_element_type=F32) + bs_ref[:, lanes]
            sa_ref[rws, lanes] = (ug[rws, lanes] * mixed).astype(BF16)
    y_a = jnp.dot(sa_ref[...], wpa_ref[...], preferred_element_type=F32)
    for h in range(N_HEADS):
        lanes = slice(h * HEAD_DIM, (h + 1) * HEAD_DIM)
        o = of_ref[0, :, lanes].astype(F32) + ob_ref[0, :, lanes].astype(F32)
        ms = jnp.mean(o * o, axis=-1, keepdims=True)
        o = o * lax.rsqrt(ms + EPS) * bng_ref[...]
        sb_ref[:, lanes] = (o * _silu(z_ref[:, lanes].astype(F32))).astype(BF16)
    y_b = jnp.dot(sb_ref[...], wpb_ref[...], preferred_element_type=F32)
    merged = _sigmoid(ga_ref[...].astype(F32)) * y_a + _sigmoid(gb_ref[...].astype(F32)) * y_b
    y = jnp.dot(merged.astype(BF16), wout_ref[...], preferred_element_type=F32)
    xn = _load_rows(x_refs, pl.program_id(0) + tile_off, n_ctx_tiles) + g1_ref[0] * y
    xo_ref[...] = xn
    ms = jnp.mean(xn * xn, axis=-1, keepdims=True)
    h2 = xn * lax.rsqrt(ms + EPS) * n2g_ref[...]
    h2 = h2 * (1.0 + sc2_ref[0]) + sh2_ref[0]
    h2_ref[...] = h2.astype(BF16)
    logits = lax.dot_general(wr_ref[...], h2, (((1,), (1,)), ((), ())),
                             precision=HIGHEST, preferred_element_type=F32)
    scores = _sigmoid(logits)
    sel = scores + br_ref[...]
    srow = [sel[e:e + 1, :] for e in range(N_EXPERTS)]
    grp = []
    for g in range(N_EXPERT_GROUPS):
        m = srow[4 * g:4 * g + 4]
        best2 = None
        for a in range(4):
            for b in range(a + 1, 4):
                pair = m[a] + m[b]
                best2 = pair if best2 is None else jnp.maximum(best2, pair)
        grp.append(best2)
    best_val = grp[0]
    best_idx = jnp.zeros_like(best_val, dtype=jnp.int32)
    for g in range(1, N_EXPERT_GROUPS):
        better = grp[g] > best_val
        best_val = jnp.where(better, grp[g], best_val)
        best_idx = jnp.where(better, g, best_idx)
    picked = []
    for e in range(N_EXPERTS):
        g = e // EXPERTS_PER_GROUP
        rank = jnp.zeros_like(best_idx)
        for o_e in range(4 * g, 4 * g + 4):
            if o_e == e:
                continue
            ahead = (srow[o_e] > srow[e]) if o_e > e else (srow[o_e] >= srow[e])
            rank = rank + ahead.astype(jnp.int32)
        chosen = jnp.logical_and(best_idx == g, rank < 2)
        picked.append(jnp.where(chosen, scores[e:e + 1, :], 0.0))
    total = picked[0]
    for e in range(1, N_EXPERTS):
        total = total + picked[e]
    inv = 1.0 / total
    for e in range(N_EXPERTS):
        w_e = picked[e] * inv
        comb_ref[e:e + 1, :] = w_e
        n_e = jnp.sum(jnp.where(w_e > 0.0, 1.0, 0.0), axis=-1, keepdims=True)
        cnt_ref[0, e:e + 1, :] = jnp.broadcast_to(n_e, (1, 128))


def _mix_call(xs, u, v, o2, z, ga, gb, mods, layer, norm2_g, ln_g, ws, bs, bng,
              wpa, wpb, wout, wr_t, br, mod_row, n_ctx_tiles, tile_off):
    d = D_MODEL
    nt = sum(a.shape[0] for a in xs) // ROW_TILE - tile_off
    rows = nt * ROW_TILE
    g1_idx = (layer * 6 + 2) * COND_ROWS
    sh2_idx = (layer * 6 + 3) * COND_ROWS
    sc2_idx = (layer * 6 + 4) * COND_ROWS
    row_spec = pl.BlockSpec((ROW_TILE, d), lambda i: (i + tile_off, 0))
    out_spec = pl.BlockSpec((ROW_TILE, d), lambda i: (i, 0))
    vec = pl.BlockSpec((1, d), lambda i: (0, 0))
    wspec = pl.BlockSpec((d, d), lambda i: (0, 0))

    def mod(idx):
        return pl.BlockSpec((1, 1, d), lambda i: (idx + mod_row(i + tile_off), 0, 0))

    return pl.pallas_call(
        functools.partial(_mix_kernel, len(xs), n_ctx_tiles, tile_off),
        grid=(nt,),
        in_specs=_row_specs(xs, n_ctx_tiles, tile_off, d) + [
            row_spec, row_spec,
            pl.BlockSpec((1, ROW_TILE, d), lambda i: (0, i + tile_off, 0)),
            pl.BlockSpec((1, ROW_TILE, d), lambda i: (1, i + tile_off, 0)),
            row_spec, row_spec, row_spec,
            mod(g1_idx), mod(sh2_idx), mod(sc2_idx),
            vec, vec,
            pl.BlockSpec((SGU_GROUPS, SGU_CHUNK, SGU_CHUNK), lambda i: (0, 0, 0)),
            pl.BlockSpec((SGU_CHUNK, d), lambda i: (0, 0)),
            pl.BlockSpec((1, HEAD_DIM), lambda i: (0, 0)),
            wspec, wspec, wspec,
            pl.BlockSpec((N_EXPERTS, d), lambda i: (0, 0)),
            pl.BlockSpec((N_EXPERTS, 1), lambda i: (0, 0)),
        ],
        out_specs=[out_spec, out_spec, pl.BlockSpec((N_EXPERTS, ROW_TILE), lambda i: (0, i)),
                   pl.BlockSpec((1, N_EXPERTS, 128), lambda i: (i, 0, 0))],
        out_shape=[jax.ShapeDtypeStruct((rows, d), F32), jax.ShapeDtypeStruct((rows, d), BF16),
                   jax.ShapeDtypeStruct((N_EXPERTS, rows), F32),
                   jax.ShapeDtypeStruct((nt, N_EXPERTS, 128), F32)],
        scratch_shapes=[pltpu.VMEM((ROW_TILE, d), BF16), pltpu.VMEM((ROW_TILE, d), BF16)],
        compiler_params=_cparams(("arbitrary",)),
        name="mix_merge_router",
    )(*xs, u, v, o2, o2, z, ga, gb, mods, mods, mods, norm2_g, ln_g, ws, bs, bng,
      wpa, wpb, wout, wr_t, br)


def _segment_copies(t, off_s, gs_s, l16_s, local_ref, global_ref, sem, to_global, wait):
    for e in range(N_EXPERTS):
        idx = t * N_EXPERTS + e
        lo = off_s[idx]
        go = gs_s[idx]
        n16 = l16_s[idx]
        done = jnp.int32(0)
        for bit in SEG_BITS:
            size = bit * SEG_ALIGN
            present = (n16 & bit) != 0
            l_at = local_ref.at[pl.ds(pl.multiple_of(lo + done, SEG_ALIGN), size)]
            g_at = global_ref.at[pl.ds(pl.multiple_of(go + done, SEG_ALIGN), size)]
            cp = pltpu.make_async_copy(l_at, g_at, sem) if to_global else pltpu.make_async_copy(g_at, l_at, sem)

            @pl.when(present)
            def _():
                if wait:
                    cp.wait()
                else:
                    cp.start()

            done = done + jnp.where(present, size, 0)


def _sort_kernel(off_s, gs_s, l16_s, h_ref, comb_ref, offcol_ref, xs_in_ref, slots_ref, xs_ref, hs_ref, sem):
    del xs_in_ref
    t = pl.program_id(0)
    ts = SORT_TILE
    comb = comb_ref[...]
    asg = comb > 0.0
    row = lax.broadcasted_iota(jnp.int32, (ts, ts), 0)
    col = lax.broadcasted_iota(jnp.int32, (ts, ts), 1)
    before = jnp.where(row < col, 1.0, 0.0).astype(BF16)
    rank = jnp.dot(jnp.where(asg, 1.0, 0.0).astype(BF16), before, preferred_element_type=F32)
    pos = offcol_ref[0][:, 0:1] + rank
    p_lo = jnp.min(jnp.where(asg, pos, 1e9), axis=0, keepdims=True)
    p_hi = jnp.max(jnp.where(asg, pos, -1.0), axis=0, keepdims=True)
    w_lo = jnp.sum(jnp.where(jnp.logical_and(asg, pos == p_lo), comb, 0.0), axis=0, keepdims=True)
    w_hi = jnp.sum(jnp.where(jnp.logical_and(asg, pos == p_hi), comb, 0.0), axis=0, keepdims=True)
    w_hi = jnp.where(p_hi > p_lo, w_hi, 0.0)
    dest = lax.broadcasted_iota(jnp.int32, (SORT_CAP, ts), 0).astype(F32)
    onehot = jnp.where(jnp.logical_or(dest == p_lo, dest == p_hi), 1.0, 0.0).astype(BF16)
    hs_ref[...] = jnp.dot(onehot, h_ref[...], preferred_element_type=F32).astype(BF16)
    slot_rows = jnp.concatenate([p_lo, p_hi, w_lo, w_hi, jnp.zeros((124, ts), F32)], axis=0)
    slots_ref[...] = slot_rows.T
    _segment_copies(t, off_s, gs_s, l16_s, hs_ref, xs_ref, sem, to_global=True, wait=False)
    _segment_copies(t, off_s, gs_s, l16_s, hs_ref, xs_ref, sem, to_global=True, wait=True)


def _sort_call(h2, comb_t, tables, offcol, n_rows_sorted):
    rows, d = h2.shape
    nt = rows // SORT_TILE
    xs0 = jnp.zeros((n_rows_sorted, d), BF16)
    grid_spec = pltpu.PrefetchScalarGridSpec(
        num_scalar_prefetch=3,
        grid=(nt,),
        in_specs=[
            pl.BlockSpec((SORT_TILE, d), lambda i, *_: (i, 0)),
            pl.BlockSpec((N_EXPERTS, SORT_TILE), lambda i, *_: (0, i)),
            pl.BlockSpec((1, N_EXPERTS, 128), lambda i, *_: (i, 0, 0)),
            pl.BlockSpec(memory_space=pl.ANY),
        ],
        out_specs=[
            pl.BlockSpec((SORT_TILE, 128), lambda i, *_: (i, 0)),
            pl.BlockSpec(memory_space=pl.ANY),
        ],
        scratch_shapes=[pltpu.VMEM((SORT_CAP, d), BF16), pltpu.SemaphoreType.DMA],
    )
    slots, xs = pl.pallas_call(
        _sort_kernel,
        grid_spec=grid_spec,
        out_shape=[jax.ShapeDtypeStruct((rows, 128), F32), jax.ShapeDtypeStruct((n_rows_sorted, d), BF16)],
        input_output_aliases={6: 1},
        compiler_params=_cparams(("arbitrary",)),
        name="moe_sort",
    )(*tables, h2, comb_t, offcol, xs0)
    return slots, xs


def _experts_kernel(be_s, bv_s, x_ref, w1_ref, w3_ref, w2_ref, y_ref):
    b = pl.program_id(0)

    @pl.when(bv_s[b] != 0)
    def _():
        x = x_ref[...]
        a = jnp.dot(x, w1_ref[0, 0].astype(BF16), preferred_element_type=F32)
        g = jnp.dot(x, w3_ref[0, 0].astype(BF16), preferred_element_type=F32)
        hid = (_silu(a) * g).astype(BF16)
        y_ref[...] = jnp.dot(hid, w2_ref[0, 0].astype(BF16), preferred_element_type=F32).astype(BF16)

    @pl.when(bv_s[b] == 0)
    def _():
        y_ref[...] = jnp.zeros_like(y_ref)


def _experts_call(xs, blk_expert, blk_valid, w1, w3, w2, layer):
    rows, d = xs.shape
    nb = rows // MOE_BLOCK
    grid_spec = pltpu.PrefetchScalarGridSpec(
        num_scalar_prefetch=2,
        grid=(nb,),
        in_specs=[
            pl.BlockSpec((MOE_BLOCK, d), lambda b, be, bv: (b, 0)),
            pl.BlockSpec((1, 1, d, D_EXPERT), lambda b, be, bv: (layer, be[b], 0, 0)),
            pl.BlockSpec((1, 1, d, D_EXPERT), lambda b, be, bv: (layer, be[b], 0, 0)),
            pl.BlockSpec((1, 1, D_EXPERT, d), lambda b, be, bv: (layer, be[b], 0, 0)),
        ],
        out_specs=pl.BlockSpec((MOE_BLOCK, d), lambda b, be, bv: (b, 0)),
    )
    return pl.pallas_call(
        _experts_kernel,
        grid_spec=grid_spec,
        out_shape=jax.ShapeDtypeStruct((rows, d), BF16),
        compiler_params=_cparams(("arbitrary",)),
        name="moe_experts",
    )(blk_expert, blk_valid, xs, w1, w3, w2)


def _unsort_kernel(final, off_s, gs_s, l16_s, ys_ref, slots_ref, x_ref, fg_ref, g2a_ref, g2b_ref,
                   o_ref, yt_ref, sem):
    t = pl.program_id(0)
    ts = SORT_TILE
    yt_ref[...] = jnp.zeros_like(yt_ref)
    _segment_copies(t, off_s, gs_s, l16_s, yt_ref, ys_ref, sem, to_global=False, wait=False)
    _segment_copies(t, off_s, gs_s, l16_s, yt_ref, ys_ref, sem, to_global=False, wait=True)
    slots = slots_ref[...]
    src = lax.broadcasted_iota(jnp.int32, (ts, SORT_CAP), 1).astype(F32)
    weights = jnp.where(src == slots[:, 0:1], slots[:, 2:3], 0.0) + jnp.where(src == slots[:, 1:2], slots[:, 3:4], 0.0)
    y = jnp.dot(weights.astype(BF16), yt_ref[...], preferred_element_type=F32)
    for s, g2_ref in enumerate((g2a_ref, g2b_ref)):
        rws = slice(s * ROW_TILE, (s + 1) * ROW_TILE)
        xn = x_ref[rws, :] + g2_ref[0] * y[rws, :]
        if final:
            ms = jnp.mean(xn * xn, axis=-1, keepdims=True)
            xn = xn * lax.rsqrt(ms + EPS) * fg_ref[...]
        o_ref[rws, :] = xn


def _unsort_call(ys, slots, x, tables, final_g, mods, layer, mod_row, tile_off, final):
    rows, d = x.shape
    nt = rows // SORT_TILE
    sub = SORT_TILE // ROW_TILE
    g2_idx = (layer * 6 + 5) * COND_ROWS
    row_spec = pl.BlockSpec((SORT_TILE, d), lambda i, *_: (i, 0))

    def mod(s):
        return pl.BlockSpec((1, 1, d), lambda i, *_: (g2_idx + mod_row(i * sub + s + tile_off), 0, 0))

    grid_spec = pltpu.PrefetchScalarGridSpec(
        num_scalar_prefetch=3,
        grid=(nt,),
        in_specs=[
            pl.BlockSpec(memory_space=pl.ANY),
            pl.BlockSpec((SORT_TILE, 128), lambda i, *_: (i, 0)),
            row_spec,
            pl.BlockSpec((1, d), lambda i, *_: (0, 0)),
            mod(0), mod(1),
        ],
        out_specs=row_spec,
        scratch_shapes=[pltpu.VMEM((SORT_CAP, d), BF16), pltpu.SemaphoreType.DMA],
    )
    return pl.pallas_call(
        functools.partial(_unsort_kernel, final),
        grid_spec=grid_spec,
        out_shape=jax.ShapeDtypeStruct((rows, d), F32),
        compiler_params=_cparams(("arbitrary",)),
        name="moe_unsort_residual",
    )(*tables, ys, slots, x, final_g, mods, mods)


def _moe_tables(cnt_blocks, n_blocks):
    cnt = cnt_blocks[:, :, 0].astype(jnp.int32)
    per = SORT_TILE // ROW_TILE
    cnt = cnt.reshape(cnt.shape[0] // per, per, N_EXPERTS).sum(axis=1)
    seg = (cnt + SEG_ALIGN - 1) // SEG_ALIGN * SEG_ALIGN
    off = jnp.cumsum(seg, axis=1) - seg
    blocks_e = (seg.sum(axis=0) + MOE_BLOCK - 1) // MOE_BLOCK
    first_block = jnp.cumsum(blocks_e) - blocks_e
    gstart = first_block[None, :] * MOE_BLOCK + jnp.cumsum(seg, axis=0) - seg
    blk = jnp.arange(n_blocks, dtype=jnp.int32)
    blk_expert = jnp.minimum(jnp.searchsorted(jnp.cumsum(blocks_e), blk, side="right"), N_EXPERTS - 1)
    blk_valid = (blk < blocks_e.sum()).astype(jnp.int32)
    tables = (off.reshape(-1).astype(jnp.int32), gstart.reshape(-1).astype(jnp.int32),
              (seg // SEG_ALIGN).reshape(-1).astype(jnp.int32))
    offcol = jnp.broadcast_to(off.astype(F32)[:, :, None], off.shape + (128,))
    return tables, offcol, blk_expert.astype(jnp.int32), blk_valid


def _moe_call(h2, comb_t, cnt_blocks, x, w1, w3, w2, final_g, mods, layer, mod_row, tile_off, final):
    rows = x.shape[0]
    n_tiles = rows // SORT_TILE
    max_rows = 2 * rows + n_tiles * N_EXPERTS * (SEG_ALIGN - 1)
    n_blocks = max_rows // MOE_BLOCK + N_EXPERTS
    tables, offcol, blk_expert, blk_valid = _moe_tables(cnt_blocks, n_blocks)
    slots, xs = _sort_call(h2, comb_t, tables, offcol, n_blocks * MOE_BLOCK)
    ys = _experts_call(xs, blk_expert, blk_valid, w1, w3, w2, layer)
    return _unsort_call(ys, slots, x, tables, final_g, mods, layer, mod_row, tile_off, final)


def kernel(x, c, ctx, c_ctx, w_ada, b_ada, norm1_g, norm2_g, w_in, a_ln_g, a_w_s, a_b_s, b_conv_w, b_A_log, b_dt_bias, b_norm_g, w_proj_a, w_proj_b, w_out, w_router, b_router, w_e1, w_e3, w_e2, final_g):
    batch, seq_len, d = x.shape
    ctx_len = ctx.shape[1]
    n_layers = w_ada.shape[0]
    assert d == D_MODEL and batch + 1 <= COND_ROWS
    assert ctx_len % ROW_TILE == 0 and seq_len % ROW_TILE == 0
    assert (batch * ctx_len) % SORT_TILE == 0 and (batch * seq_len) % SORT_TILE == 0
    ctx_tiles = ctx_len // ROW_TILE
    lat_tiles = seq_len // ROW_TILE
    n_ctx_tiles = batch * ctx_tiles

    def mod_row(tile):
        return jnp.where(tile < n_ctx_tiles, batch, (tile - n_ctx_tiles) // lat_tiles)

    xs = (ctx.reshape(batch * ctx_len, d), x.reshape(batch * seq_len, d))
    cond = jnp.concatenate([c, c_ctx[None, :], jnp.zeros((COND_ROWS - batch - 1, d), F32)], axis=0)
    mods = _ada_call(cond, w_ada, b_ada).reshape(n_layers * 6 * COND_ROWS, 1, d)

    n_ab = 4 * N_HEADS
    ab_lo = 6 * d
    wr_t = w_router.T
    br = b_router.reshape(N_EXPERTS, 1)
    fg = final_g.reshape(1, d)
    for l in range(n_layers):
        w_all = jnp.concatenate([w_in[l][:, :ab_lo], w_in[l][:, ab_lo + n_ab:], w_in[l][:, ab_lo:ab_lo + n_ab],
                                 jnp.zeros((d, 128 - n_ab), F32)], axis=1).astype(BF16)
        u, v, q, k, vv, z, ga, gb, ab = _in_call(xs, mods, l, norm1_g[l].reshape(1, d), w_all, mod_row, n_ctx_tiles)

        conv_w = jnp.pad(b_conv_w[l], ((0, 8 - CONV_TAPS), (0, 0)))
        zeros8 = jnp.zeros((N_HEADS,), F32)
        alog_row = jnp.pad(jnp.concatenate([b_A_log[l, 0], zeros8, b_A_log[l, 1], zeros8]), (0, 128 - n_ab)).reshape(1, 128)
        dtb_row = jnp.pad(jnp.concatenate([b_dt_bias[l, 0], zeros8, b_dt_bias[l, 1], zeros8]), (0, 128 - n_ab)).reshape(1, 128)
        qn, kn, vs, gates = _conv_call(q, k, vv, conv_w, ab, alog_row, dtb_row, n_ctx_tiles, ctx_tiles, lat_tiles)
        o2 = _gdn_call(qn, kn, vs, gates, batch, ctx_len, seq_len)

        bs = jnp.repeat(a_b_s[l].T, SGU_CHUNK, axis=1)
        bng = b_norm_g[l].reshape(1, HEAD_DIM)
        last = l == n_layers - 1
        tile_off = n_ctx_tiles if last else 0
        x_new, h2, comb_t, cnt = _mix_call(
            xs, u, v, o2, z, ga, gb, mods, l, norm2_g[l].reshape(1, d), a_ln_g[l].reshape(1, d),
            a_w_s[l].astype(BF16), bs, bng,
            w_proj_a[l].astype(BF16), w_proj_b[l].astype(BF16), w_out[l].astype(BF16), wr_t, br, mod_row,
            n_ctx_tiles, tile_off)
        xs = (_moe_call(h2, comb_t, cnt, x_new, w_e1, w_e3, w_e2,
                        fg, mods, l, mod_row, tile_off, final=last),)
    return xs[0].reshape(batch, seq_len, d)
```

```python
import functools

import jax
import jax.numpy as jnp
from jax import lax
from jax.experimental import pallas as pl
from jax.experimental.pallas import tpu as pltpu

F32 = jnp.float32
BF16 = jnp.bfloat16
HIGHEST = lax.Precision.HIGHEST

EPS = 1e-6
D_MODEL = 1024
N_HEADS = 8
HEAD_DIM = 128
SGU_CHUNK = 128
SGU_GROUPS = 8
CONV_TAPS = 5
N_EXPERTS = 16
EXPERTS_PER_GROUP = 4
N_EXPERT_GROUPS = 4
D_EXPERT = 512
ROW_TILE = 256
IN_TILE = 512
GDN_CHUNK = 128
GDN_BLOCK = 2
SORT_TILE = 512
SEG_ALIGN = 16
SORT_CAP = 1280
MOE_BLOCK = 512
SEG_BITS = (32, 16, 8, 4, 2, 1)
COND_ROWS = 8
HALO = 16
VMEM_LIMIT = 56 * 1024 * 1024


def _cparams(sem):
    return pltpu.CompilerParams(dimension_semantics=sem, vmem_limit_bytes=VMEM_LIMIT)


def _sigmoid(x):
    return 0.5 * jnp.tanh(0.5 * x) + 0.5


def _silu(x):
    return x * _sigmoid(x)


def _gelu_tanh(x):
    return 0.5 * x * (1.0 + jnp.tanh(0.7978845608028654 * (x + 0.044715 * (x * x * x))))


def _softplus(x):
    return jnp.maximum(x, 0.0) + jnp.log1p(jnp.exp(-jnp.abs(x)))


def _ada_kernel(cond_ref, w_ref, b_ref, o_ref):
    s = _silu(cond_ref[...])
    o_ref[0, 0] = jnp.dot(s, w_ref[0], precision=HIGHEST, preferred_element_type=F32) + b_ref[0, 0]


def _ada_call(cond, w_ada, b_ada):
    n_layers = w_ada.shape[0]
    d = D_MODEL
    return pl.pallas_call(
        _ada_kernel,
        grid=(n_layers, 6),
        in_specs=[
            pl.BlockSpec((COND_ROWS, d), lambda l, j: (0, 0)),
            pl.BlockSpec((1, d, d), lambda l, j: (l, 0, j)),
            pl.BlockSpec((1, 1, 1, d), lambda l, j: (l, j, 0, 0)),
        ],
        out_specs=pl.BlockSpec((1, 1, COND_ROWS, d), lambda l, j: (l, j, 0, 0)),
        out_shape=jax.ShapeDtypeStruct((n_layers, 6, COND_ROWS, d), F32),
        compiler_params=_cparams(("arbitrary", "arbitrary")),
        name="ada_params",
    )(cond, w_ada, b_ada.reshape(n_layers, 6, 1, d))


def _load_rows(x_refs, tile, n_ctx_tiles):
    if len(x_refs) == 1:
        return x_refs[0][...]
    return jnp.where(tile < n_ctx_tiles, x_refs[0][...], x_refs[1][...])


def _row_specs(xs, n_ctx_tiles, tile_off, d, tile=ROW_TILE):
    if len(xs) == 1:
        return [pl.BlockSpec((tile, d), lambda i: (i + tile_off, 0))]
    return [pl.BlockSpec((tile, d), lambda i: (jnp.minimum(i + tile_off, n_ctx_tiles - 1), 0)),
            pl.BlockSpec((tile, d), lambda i: (jnp.maximum(i + tile_off - n_ctx_tiles, 0), 0))]


def _in_kernel(n_src, n_ctx_tiles, *refs):
    x_refs = refs[:n_src]
    (sh_ref, sc_ref, g_ref, w_ref,
     u_ref, v_ref, q_ref, k_ref, vv_ref, z_ref, ga_ref, gb_ref, ab_ref) = refs[n_src:]
    x = _load_rows(x_refs, pl.program_id(0), n_ctx_tiles)
    ms = jnp.mean(x * x, axis=-1, keepdims=True)
    h = x * lax.rsqrt(ms + EPS) * g_ref[...]
    h = h * (1.0 + sc_ref[0]) + sh_ref[0]
    hb = h.astype(BF16)
    d = D_MODEL
    outs = (u_ref, v_ref, q_ref, k_ref, vv_ref, z_ref, ga_ref, gb_ref)
    for n, o_ref in enumerate(outs):
        o_ref[...] = jnp.dot(hb, w_ref[:, n * d:(n + 1) * d],
                             preferred_element_type=F32).astype(BF16)
    ab_ref[...] = jnp.dot(hb, w_ref[:, 8 * d:8 * d + 128], preferred_element_type=F32)


def _in_call(xs, mods, layer, norm_g, w_all, mod_row, n_ctx_tiles):
    d = D_MODEL
    rows = sum(a.shape[0] for a in xs)
    nt = rows // IN_TILE
    per = IN_TILE // ROW_TILE
    sh_idx = (layer * 6 + 0) * COND_ROWS
    sc_idx = (layer * 6 + 1) * COND_ROWS
    row_spec = pl.BlockSpec((IN_TILE, d), lambda i: (i, 0))
    out_shapes = [jax.ShapeDtypeStruct((rows, d), BF16)] * 8 + [jax.ShapeDtypeStruct((rows, 128), F32)]
    return pl.pallas_call(
        functools.partial(_in_kernel, len(xs), n_ctx_tiles),
        grid=(nt,),
        in_specs=_row_specs(xs, n_ctx_tiles, 0, d, IN_TILE) + [
            pl.BlockSpec((1, 1, d), lambda i: (sh_idx + mod_row(i * per), 0, 0)),
            pl.BlockSpec((1, 1, d), lambda i: (sc_idx + mod_row(i * per), 0, 0)),
            pl.BlockSpec((1, d), lambda i: (0, 0)),
            pl.BlockSpec((d, 8 * d + 128), lambda i: (0, 0), pipeline_mode=pl.Buffered(1)),
        ],
        out_specs=[row_spec] * 8 + [pl.BlockSpec((IN_TILE, 128), lambda i: (i, 0))],
        out_shape=out_shapes,
        compiler_params=_cparams(("arbitrary",)),
        name="norm_in_proj",
    )(*xs, mods, mods, norm_g, w_all)


def _conv_kernel(n_ctx_tiles, ctx_tiles, lat_tiles,
                 q_ref, qp_ref, qn_ref, k_ref, kp_ref, kn_ref, v_ref, vp_ref, vn_ref,
                 cw_ref, ab_ref, alog_ref, dtb_ref,
                 qo_ref, ko_ref, vo_ref, gate_ref, ext_ref):
    i = pl.program_id(0)
    j = jnp.where(i < n_ctx_tiles, i % ctx_tiles, (i - n_ctx_tiles) % lat_tiles)
    last = jnp.where(i < n_ctx_tiles, ctx_tiles - 1, lat_tiles - 1)
    has_prev = (j != 0).astype(F32)
    has_next = (j != last).astype(F32)
    pad = CONV_TAPS // 2
    srcs = ((q_ref, qp_ref, qn_ref, qo_ref, True, HEAD_DIM ** -0.5),
            (k_ref, kp_ref, kn_ref, ko_ref, True, 1.0),
            (v_ref, vp_ref, vn_ref, vo_ref, False, 1.0))
    for t, (m_ref, p_ref, n_ref, o_ref, l2, scale) in enumerate(srcs):
        ext_ref[0:8, :] = p_ref[HALO - 8:HALO, :].astype(F32) * has_prev
        ext_ref[8:8 + ROW_TILE, :] = m_ref[...].astype(F32)
        ext_ref[8 + ROW_TILE:16 + ROW_TILE, :] = n_ref[0:8, :].astype(F32) * has_next
        for h in range(N_HEADS):
            lanes = slice(h * HEAD_DIM, (h + 1) * HEAD_DIM)
            acc = None
            for tap in range(CONV_TAPS):
                w_row = cw_ref[tap:tap + 1, t * D_MODEL + h * HEAD_DIM:t * D_MODEL + (h + 1) * HEAD_DIM]
                start = 8 - pad + tap
                term = ext_ref[start:start + ROW_TILE, lanes] * w_row
                acc = term if acc is None else acc + term
            y = _silu(acc)
            if l2:
                ss = jnp.sum(y * y, axis=-1, keepdims=True)
                y = y * (lax.rsqrt(ss + EPS) * scale)
            o_ref[:, lanes] = y.astype(BF16)
    ab = ab_ref[...]
    lane = lax.broadcasted_iota(jnp.int32, ab.shape, 1)
    is_decay = ((lane // N_HEADS) % 2) == 0
    g = -jnp.exp(alog_ref[...]) * _softplus(ab + dtb_ref[...])
    gate = jnp.where(is_decay, g, _sigmoid(ab))
    gate = jnp.where(lane < 4 * N_HEADS, gate, 0.0)
    gate_ref[0] = gate
    gate_ref[1] = pltpu.roll(gate, 128 - 2 * N_HEADS, axis=1)


def _conv_call(q, k, v, conv_w, ab, alog_row, dtb_row, n_ctx_tiles, ctx_tiles, lat_tiles):
    rows, d = q.shape
    nt = rows // ROW_TILE
    per = ROW_TILE // HALO
    n_halo = rows // HALO
    main = pl.BlockSpec((ROW_TILE, d), lambda i: (i, 0))
    prev = pl.BlockSpec((HALO, d), lambda i: (jnp.maximum(i * per - 1, 0), 0))
    nxt = pl.BlockSpec((HALO, d), lambda i: (jnp.minimum((i + 1) * per, n_halo - 1), 0))
    small = pl.BlockSpec((1, 128), lambda i: (0, 0))
    return pl.pallas_call(
        functools.partial(_conv_kernel, n_ctx_tiles, ctx_tiles, lat_tiles),
        grid=(nt,),
        in_specs=[main, prev, nxt] * 3 + [
            pl.BlockSpec((8, 3 * d), lambda i: (0, 0)),
            pl.BlockSpec((ROW_TILE, 128), lambda i: (i, 0)),
            small, small,
        ],
        out_specs=[main, main, main, pl.BlockSpec((2, ROW_TILE, 128), lambda i: (0, i, 0))],
        out_shape=[jax.ShapeDtypeStruct((rows, d), BF16)] * 3 + [jax.ShapeDtypeStruct((2, rows, 128), F32)],
        scratch_shapes=[pltpu.VMEM((ROW_TILE + 16, d), F32)],
        compiler_params=_cparams(("arbitrary",)),
        name="conv_silu_gates",
    )(q, q, q, k, k, k, v, v, v, conv_w, ab, alog_row, dtb_row)


def _gdn_kernel(q_ref, k_ref, v_ref, gate_ref, o_ref, s_ref):
    fwd = pl.program_id(1) == 0

    @pl.when(pl.program_id(2) == 0)
    def _():
        s_ref[...] = jnp.zeros_like(s_ref)

    for n in range(GDN_BLOCK):
        start = jnp.where(fwd, n, GDN_BLOCK - 1 - n) * GDN_CHUNK
        _gdn_chunk(fwd, pl.ds(pl.multiple_of(start, GDN_CHUNK), GDN_CHUNK), q_ref, k_ref, v_ref, gate_ref, o_ref, s_ref)


def _gdn_chunk(fwd, rows, q_ref, k_ref, v_ref, gate_ref, o_ref, s_ref):
    c = GDN_CHUNK
    row = lax.broadcasted_iota(jnp.int32, (c, c), 0)
    col = lax.broadcasted_iota(jnp.int32, (c, c), 1)
    later = jnp.where(fwd, row, col)
    earlier = jnp.where(fwd, col, row)
    incl = later >= earlier
    strict = later > earlier
    eye = (row == col).astype(F32)

    gate = gate_ref[0, rows, :]
    gc_all = jnp.dot(incl.astype(F32), gate, precision=HIGHEST, preferred_element_type=F32)
    gtot_all = jnp.dot(jnp.ones((c, c), F32), gate, precision=HIGHEST, preferred_element_type=F32)
    gate_t = gate.T
    gc_t = gc_all.T
    gtot_t = gtot_all.T

    heads = range(N_HEADS)
    lanes = [slice(h * HEAD_DIM, (h + 1) * HEAD_DIM) for h in heads]
    nt_dims = (((1,), (1,)), ((), ()))
    q = [q_ref[rows, lanes[h]] for h in heads]
    k = [k_ref[rows, lanes[h]] for h in heads]
    v = [v_ref[rows, lanes[h]] for h in heads]
    gc_col = [jnp.broadcast_to(gc_all[:, h:h + 1], (c, c)) for h in heads]
    gc_row = [gc_t[h:h + 1, :] for h in heads]
    gtot_row = [gtot_t[h:h + 1, :] for h in heads]
    beta_row = [gate_t[N_HEADS + h:N_HEADS + h + 1, :] for h in heads]

    decay = [jnp.exp(jnp.where(incl, gc_col[h] - gc_row[h], -jnp.inf)) * beta_row[h] for h in heads]
    kq = [lax.dot_general(jnp.concatenate([k[h], q[h]], axis=0), k[h], nt_dims, preferred_element_type=F32)
          for h in heads]
    n_mat = [jnp.where(strict, kq[h][:c] * decay[h], 0.0) for h in heads]
    a_mat = [(kq[h][c:] * decay[h]).astype(BF16) for h in heads]
    a_b = [(eye + n_mat[h]).astype(BF16) for h in heads]
    t_inv = [eye - n_mat[h] for h in heads]
    for _ in range(6):
        tb = [t_inv[h].astype(BF16) for h in heads]
        err = [(eye - jnp.dot(a_b[h], tb[h], preferred_element_type=F32)).astype(BF16) for h in heads]
        t_inv = [t_inv[h] + jnp.dot(tb[h], err[h], preferred_element_type=F32) for h in heads]
    u_val = [jnp.dot(t_inv[h].astype(BF16), v[h], preferred_element_type=F32) for h in heads]
    w_key = [jnp.dot((t_inv[h] * jnp.exp(gc_row[h])).astype(BF16), k[h], preferred_element_type=F32)
             for h in heads]

    s_old = [s_ref[h] for h in heads]
    sb = [s_old[h].astype(BF16) for h in heads]
    wq_s = [jnp.dot(jnp.concatenate([w_key[h].astype(BF16), q[h]], axis=0), sb[h], preferred_element_type=F32)
            for h in heads]
    v_new = [(u_val[h] - wq_s[h][:c]).astype(BF16) for h in heads]
    o_val = [jnp.exp(gc_col[h]) * wq_s[h][c:] + jnp.dot(a_mat[h], v_new[h], preferred_element_type=F32)
             for h in heads]
    kd_t = [(k[h].astype(F32).T * (jnp.exp(gtot_row[h] - gc_row[h]) * beta_row[h])).astype(BF16) for h in heads]
    for h in heads:
        s_ref[h] = s_old[h] * jnp.exp(gtot_row[h]) + jnp.dot(kd_t[h], v_new[h], preferred_element_type=F32)
        o_ref[0, rows, lanes[h]] = o_val[h].astype(BF16)


def _gdn_call(q, k, v, gates, batch, ctx_len, seq_len):
    rows, d = q.shape
    c = GDN_BLOCK * GDN_CHUNK
    assert ctx_len % c == 0 and seq_len % c == 0
    n_ctx = ctx_len // c
    n_lat = seq_len // c
    n_steps = n_ctx + n_lat

    def chunk(b, dr, s):
        rev = jnp.where(s < n_ctx, n_ctx - 1 - s, n_ctx + n_steps - 1 - s)
        pos = jnp.where(dr == 0, s, rev)
        return jnp.where(pos < n_ctx, b * n_ctx + pos, batch * n_ctx + b * n_lat + pos - n_ctx)

    blk = pl.BlockSpec((c, d), lambda b, dr, s: (chunk(b, dr, s), 0))
    return pl.pallas_call(
        _gdn_kernel,
        grid=(batch, 2, n_steps),
        in_specs=[blk, blk, blk, pl.BlockSpec((1, c, 128), lambda b, dr, s: (dr, chunk(b, dr, s), 0))],
        out_specs=pl.BlockSpec((1, c, d), lambda b, dr, s: (dr, chunk(b, dr, s), 0)),
        out_shape=jax.ShapeDtypeStruct((2, rows, d), BF16),
        scratch_shapes=[pltpu.VMEM((N_HEADS, HEAD_DIM, HEAD_DIM), F32)],
        compiler_params=_cparams(("arbitrary", "arbitrary", "arbitrary")),
        name="gated_delta",
    )(q, k, v, gates)


def _mix_kernel(n_src, n_ctx_tiles, tile_off, *refs):
    x_refs = refs[:n_src]
    (u_ref, v_ref, of_ref, ob_ref, z_ref, ga_ref, gb_ref,
     g1_ref, sh2_ref, sc2_ref, n2g_ref, lng_ref, ws_ref, bs_ref, bng_ref,
     wpa_ref, wpb_ref, wout_ref, wr_ref, br_ref,
     xo_ref, h2_ref, comb_ref, cnt_ref, sa_ref, sb_ref) = refs[n_src:]
    d = D_MODEL
    ug = _gelu_tanh(u_ref[...].astype(F32))
    vg = _gelu_tanh(v_ref[...].astype(F32))
    mu = jnp.mean(vg, axis=-1, keepdims=True)
    vc = vg - mu
    var = jnp.mean(vc * vc, axis=-1, keepdims=True)
    vn = (vc * lax.rsqrt(var + EPS) * lng_ref[...]).astype(BF16)
    for ch in range(ROW_TILE // SGU_CHUNK):
        rws = slice(ch * SGU_CHUNK, (ch + 1) * SGU_CHUNK)
        for g in range(SGU_GROUPS):
            lanes = slice(g * 128, (g + 1) * 128)
            mixed = jnp.dot(ws_ref[g], vn[rws, lanes], preferred_element_type=F32) + bs_ref[:, lanes]
            sa_ref[rws, lanes] = (ug[rws, lanes] * mixed).astype(BF16)
    y_a = jnp.dot(sa_ref[...], wpa_ref[...], preferred_element_type=F32)
    for h in range(N_HEADS):
        lanes = slice(h * HEAD_DIM, (h + 1) * HEAD_DIM)
        o = of_ref[0, :, lanes].astype(F32) + ob_ref[0, :, lanes].astype(F32)
        ms = jnp.mean(o * o, axis=-1, keepdims=True)
        o = o * lax.rsqrt(ms + EPS) * bng_ref[...]
        sb_ref[:, lanes] = (o * _silu(z_ref[:, lanes].astype(F32))).astype(BF16)
    y_b = jnp.dot(sb_ref[...], wpb_ref[...], preferred_element_type=F32)
    merged = _sigmoid(ga_ref[...].astype(F32)) * y_a + _sigmoid(gb_ref[...].astype(F32)) * y_b
    y = jnp.dot(merged.astype(BF16), wout_ref[...], preferred_element_type=F32)
    xn = _load_rows(x_refs, pl.program_id(0) + tile_off, n_ctx_tiles) + g1_ref[0] * y
    xo_ref[...] = xn
    ms = jnp.mean(xn * xn, axis=-1, keepdims=True)
    h2 = xn * lax.rsqrt(ms + EPS) * n2g_ref[...]
    h2 = h2 * (1.0 + sc2_ref[0]) + sh2_ref[0]
    h2_ref[...] = h2.astype(BF16)
    logits = lax.dot_general(wr_ref[...], h2, (((1,), (1,)), ((), ())),
                             precision=HIGHEST, preferred_element_type=F32)
    scores = _sigmoid(logits)
    sel = scores + br_ref[...]
    srow = [sel[e:e + 1, :] for e in range(N_EXPERTS)]
    grp = []
    for g in range(N_EXPERT_GROUPS):
        m = srow[4 * g:4 * g + 4]
        best2 = None
        for a in range(4):
            for b in range(a + 1, 4):
                pair = m[a] + m[b]
                best2 = pair if best2 is None else jnp.maximum(best2, pair)
        grp.append(best2)
    best_val = grp[0]
    best_idx = jnp.zeros_like(best_val, dtype=jnp.int32)
    for g in range(1, N_EXPERT_GROUPS):
        better = grp[g] > best_val
        best_val = jnp.where(better, grp[g], best_val)
        best_idx = jnp.where(better, g, best_idx)
    picked = []
    for e in range(N_EXPERTS):
        g = e // EXPERTS_PER_GROUP
        rank = jnp.zeros_like(best_idx)
        for o_e in range(4 * g, 4 * g + 4):
            if o_e == e:
                continue
            ahead = (srow[o_e] > srow[e]) if o_e > e else (srow[o_e] >= srow[e])
            rank = rank + ahead.astype(jnp.int32)
        chosen = jnp.logical_and(best_idx == g, rank < 2)
        picked.append(jnp.where(chosen, scores[e:e + 1, :], 0.0))
    total = picked[0]
    for e in range(1, N_EXPERTS):
        total = total + picked[e]
    inv = 1.0 / total
    for e in range(N_EXPERTS):
        w_e = picked[e] * inv
        comb_ref[e:e + 1, :] = w_e
        n_e = jnp.sum(jnp.where(w_e > 0.0, 1.0, 0.0), axis=-1, keepdims=True)
        cnt_ref[0, e:e + 1, :] = jnp.broadcast_to(n_e, (1, 128))


def _mix_call(xs, u, v, o2, z, ga, gb, mods, layer, norm2_g, ln_g, ws, bs, bng,
              wpa, wpb, wout, wr_t, br, mod_row, n_ctx_tiles, tile_off):
    d = D_MODEL
    nt = sum(a.shape[0] for a in xs) // ROW_TILE - tile_off
    rows = nt * ROW_TILE
    g1_idx = (layer * 6 + 2) * COND_ROWS
    sh2_idx = (layer * 6 + 3) * COND_ROWS
    sc2_idx = (layer * 6 + 4) * COND_ROWS
    row_spec = pl.BlockSpec((ROW_TILE, d), lambda i: (i + tile_off, 0))
    out_spec = pl.BlockSpec((ROW_TILE, d), lambda i: (i, 0))
    vec = pl.BlockSpec((1, d), lambda i: (0, 0))
    wspec = pl.BlockSpec((d, d), lambda i: (0, 0))

    def mod(idx):
        return pl.BlockSpec((1, 1, d), lambda i: (idx + mod_row(i + tile_off), 0, 0))

    return pl.pallas_call(
        functools.partial(_mix_kernel, len(xs), n_ctx_tiles, tile_off),
        grid=(nt,),
        in_specs=_row_specs(xs, n_ctx_tiles, tile_off, d) + [
            row_spec, row_spec,
            pl.BlockSpec((1, ROW_TILE, d), lambda i: (0, i + tile_off, 0)),
            pl.BlockSpec((1, ROW_TILE, d), lambda i: (1, i + tile_off, 0)),
            row_spec, row_spec, row_spec,
            mod(g1_idx), mod(sh2_idx), mod(sc2_idx),
            vec, vec,
            pl.BlockSpec((SGU_GROUPS, SGU_CHUNK, SGU_CHUNK), lambda i: (0, 0, 0)),
            pl.BlockSpec((SGU_CHUNK, d), lambda i: (0, 0)),
            pl.BlockSpec((1, HEAD_DIM), lambda i: (0, 0)),
            wspec, wspec, wspec,
            pl.BlockSpec((N_EXPERTS, d), lambda i: (0, 0)),
            pl.BlockSpec((N_EXPERTS, 1), lambda i: (0, 0)),
        ],
        out_specs=[out_spec, out_spec, pl.BlockSpec((N_EXPERTS, ROW_TILE), lambda i: (0, i)),
                   pl.BlockSpec((1, N_EXPERTS, 128), lambda i: (i, 0, 0))],
        out_shape=[jax.ShapeDtypeStruct((rows, d), F32), jax.ShapeDtypeStruct((rows, d), BF16),
                   jax.ShapeDtypeStruct((N_EXPERTS, rows), F32),
                   jax.ShapeDtypeStruct((nt, N_EXPERTS, 128), F32)],
        scratch_shapes=[pltpu.VMEM((ROW_TILE, d), BF16), pltpu.VMEM((ROW_TILE, d), BF16)],
        compiler_params=_cparams(("arbitrary",)),
        name="mix_merge_router",
    )(*xs, u, v, o2, o2, z, ga, gb, mods, mods, mods, norm2_g, ln_g, ws, bs, bng,
      wpa, wpb, wout, wr_t, br)


def _segment_copies(t, off_s, gs_s, l16_s, local_ref, global_ref, sem, to_global, wait):
    for e in range(N_EXPERTS):
        idx = t * N_EXPERTS + e
        lo = off_s[idx]
        go = gs_s[idx]
        n16 = l16_s[idx]
        done = jnp.int32(0)
        for bit in SEG_BITS:
            size = bit * SEG_ALIGN
            present = (n16 & bit) != 0
            l_at = local_ref.at[pl.ds(pl.multiple_of(lo + done, SEG_ALIGN), size)]
            g_at = global_ref.at[pl.ds(pl.multiple_of(go + done, SEG_ALIGN), size)]
            cp = pltpu.make_async_copy(l_at, g_at, sem) if to_global else pltpu.make_async_copy(g_at, l_at, sem)

            @pl.when(present)
            def _():
                if wait:
                    cp.wait()
                else:
                    cp.start()

            done = done + jnp.where(present, size, 0)


def _sort_kernel(off_s, gs_s, l16_s, h_ref, comb_ref, offcol_ref, xs_in_ref, slots_ref, xs_ref, hs_ref, sem):
    del xs_in_ref
    t = pl.program_id(0)
    slot = t % 2
    other = 1 - slot
    ts = SORT_TILE
    comb = comb_ref[...]
    asg = comb > 0.0
    row = lax.broadcasted_iota(jnp.int32, (ts, ts), 0)
    col = lax.broadcasted_iota(jnp.int32, (ts, ts), 1)
    before = jnp.where(row < col, 1.0, 0.0).astype(BF16)
    rank = jnp.dot(jnp.where(asg, 1.0, 0.0).astype(BF16), before, preferred_element_type=F32)
    pos = offcol_ref[0][:, 0:1] + rank
    p_lo = jnp.min(jnp.where(asg, pos, 1e9), axis=0, keepdims=True)
    p_hi = jnp.max(jnp.where(asg, pos, -1.0), axis=0, keepdims=True)
    w_lo = jnp.sum(jnp.where(jnp.logical_and(asg, pos == p_lo), comb, 0.0), axis=0, keepdims=True)
    w_hi = jnp.sum(jnp.where(jnp.logical_and(asg, pos == p_hi), comb, 0.0), axis=0, keepdims=True)
    w_hi = jnp.where(p_hi > p_lo, w_hi, 0.0)
    dest = lax.broadcasted_iota(jnp.int32, (SORT_CAP, ts), 0).astype(F32)
    onehot = jnp.where(jnp.logical_or(dest == p_lo, dest == p_hi), 1.0, 0.0).astype(BF16)
    hs_ref[slot] = jnp.dot(onehot, h_ref[...], preferred_element_type=F32).astype(BF16)
    slot_rows = jnp.concatenate([p_lo, p_hi, w_lo, w_hi, jnp.zeros((124, ts), F32)], axis=0)
    slots_ref[...] = slot_rows.T
    _segment_copies(t, off_s, gs_s, l16_s, hs_ref.at[slot], xs_ref, sem.at[slot], to_global=True, wait=False)

    @pl.when(t > 0)
    def _():
        _segment_copies(t - 1, off_s, gs_s, l16_s, hs_ref.at[other], xs_ref, sem.at[other], to_global=True, wait=True)

    @pl.when(t == pl.num_programs(0) - 1)
    def _():
        _segment_copies(t, off_s, gs_s, l16_s, hs_ref.at[slot], xs_ref, sem.at[slot], to_global=True, wait=True)


def _sort_call(h2, comb_t, tables, offcol, n_rows_sorted):
    rows, d = h2.shape
    nt = rows // SORT_TILE
    xs0 = jnp.zeros((n_rows_sorted, d), BF16)
    grid_spec = pltpu.PrefetchScalarGridSpec(
        num_scalar_prefetch=3,
        grid=(nt,),
        in_specs=[
            pl.BlockSpec((SORT_TILE, d), lambda i, *_: (i, 0)),
            pl.BlockSpec((N_EXPERTS, SORT_TILE), lambda i, *_: (0, i)),
            pl.BlockSpec((1, N_EXPERTS, 128), lambda i, *_: (i, 0, 0)),
            pl.BlockSpec(memory_space=pl.ANY),
        ],
        out_specs=[
            pl.BlockSpec((SORT_TILE, 128), lambda i, *_: (i, 0)),
            pl.BlockSpec(memory_space=pl.ANY),
        ],
        scratch_shapes=[pltpu.VMEM((2, SORT_CAP, d), BF16), pltpu.SemaphoreType.DMA((2,))],
    )
    slots, xs = pl.pallas_call(
        _sort_kernel,
        grid_spec=grid_spec,
        out_shape=[jax.ShapeDtypeStruct((rows, 128), F32), jax.ShapeDtypeStruct((n_rows_sorted, d), BF16)],
        input_output_aliases={6: 1},
        compiler_params=_cparams(("arbitrary",)),
        name="moe_sort",
    )(*tables, h2, comb_t, offcol, xs0)
    return slots, xs


def _experts_kernel(be_s, bv_s, x_ref, w1_ref, w3_ref, w2_ref, y_ref):
    b = pl.program_id(0)

    @pl.when(bv_s[b] != 0)
    def _():
        x = x_ref[...]
        a = jnp.dot(x, w1_ref[0, 0].astype(BF16), preferred_element_type=F32)
        g = jnp.dot(x, w3_ref[0, 0].astype(BF16), preferred_element_type=F32)
        hid = (_silu(a) * g).astype(BF16)
        y_ref[...] = jnp.dot(hid, w2_ref[0, 0].astype(BF16), preferred_element_type=F32).astype(BF16)

    @pl.when(bv_s[b] == 0)
    def _():
        y_ref[...] = jnp.zeros_like(y_ref)


def _experts_call(xs, blk_expert, blk_valid, w1, w3, w2, layer):
    rows, d = xs.shape
    nb = rows // MOE_BLOCK
    grid_spec = pltpu.PrefetchScalarGridSpec(
        num_scalar_prefetch=2,
        grid=(nb,),
        in_specs=[
            pl.BlockSpec((MOE_BLOCK, d), lambda b, be, bv: (b, 0)),
            pl.BlockSpec((1, 1, d, D_EXPERT), lambda b, be, bv: (layer, be[b], 0, 0)),
            pl.BlockSpec((1, 1, d, D_EXPERT), lambda b, be, bv: (layer, be[b], 0, 0)),
            pl.BlockSpec((1, 1, D_EXPERT, d), lambda b, be, bv: (layer, be[b], 0, 0)),
        ],
        out_specs=pl.BlockSpec((MOE_BLOCK, d), lambda b, be, bv: (b, 0)),
    )
    return pl.pallas_call(
        _experts_kernel,
        grid_spec=grid_spec,
        out_shape=jax.ShapeDtypeStruct((rows, d), BF16),
        compiler_params=_cparams(("arbitrary",)),
        name="moe_experts",
    )(blk_expert, blk_valid, xs, w1, w3, w2)


def _unsort_kernel(final, off_s, gs_s, l16_s, ys_ref, slots_ref, x_ref, fg_ref, g2a_ref, g2b_ref,
                   o_ref, yt_ref, sem):
    t = pl.program_id(0)
    ts = SORT_TILE
    slot = t % 2
    other = 1 - slot

    def fetch(tile, buf):
        yt_ref[buf] = jnp.zeros(yt_ref.shape[1:], yt_ref.dtype)
        _segment_copies(tile, off_s, gs_s, l16_s, yt_ref.at[buf], ys_ref, sem.at[buf], to_global=False, wait=False)

    @pl.when(t == 0)
    def _():
        fetch(t, slot)

    @pl.when(t + 1 < pl.num_programs(0))
    def _():
        fetch(t + 1, other)

    _segment_copies(t, off_s, gs_s, l16_s, yt_ref.at[slot], ys_ref, sem.at[slot], to_global=False, wait=True)
    slots = slots_ref[...]
    src = lax.broadcasted_iota(jnp.int32, (ts, SORT_CAP), 1).astype(F32)
    weights = jnp.where(src == slots[:, 0:1], slots[:, 2:3], 0.0) + jnp.where(src == slots[:, 1:2], slots[:, 3:4], 0.0)
    y = jnp.dot(weights.astype(BF16), yt_ref[slot], preferred_element_type=F32)
    for s, g2_ref in enumerate((g2a_ref, g2b_ref)):
        rws = slice(s * ROW_TILE, (s + 1) * ROW_TILE)
        xn = x_ref[rws, :] + g2_ref[0] * y[rws, :]
        if final:
            ms = jnp.mean(xn * xn, axis=-1, keepdims=True)
            xn = xn * lax.rsqrt(ms + EPS) * fg_ref[...]
        o_ref[rws, :] = xn


def _unsort_call(ys, slots, x, tables, final_g, mods, layer, mod_row, tile_off, final):
    rows, d = x.shape
    nt = rows // SORT_TILE
    sub = SORT_TILE // ROW_TILE
    g2_idx = (layer * 6 + 5) * COND_ROWS
    row_spec = pl.BlockSpec((SORT_TILE, d), lambda i, *_: (i, 0))

    def mod(s):
        return pl.BlockSpec((1, 1, d), lambda i, *_: (g2_idx + mod_row(i * sub + s + tile_off), 0, 0))

    grid_spec = pltpu.PrefetchScalarGridSpec(
        num_scalar_prefetch=3,
        grid=(nt,),
        in_specs=[
            pl.BlockSpec(memory_space=pl.ANY),
            pl.BlockSpec((SORT_TILE, 128), lambda i, *_: (i, 0)),
            row_spec,
            pl.BlockSpec((1, d), lambda i, *_: (0, 0)),
            mod(0), mod(1),
        ],
        out_specs=row_spec,
        scratch_shapes=[pltpu.VMEM((2, SORT_CAP, d), BF16), pltpu.SemaphoreType.DMA((2,))],
    )
    return pl.pallas_call(
        functools.partial(_unsort_kernel, final),
        grid_spec=grid_spec,
        out_shape=jax.ShapeDtypeStruct((rows, d), F32),
        compiler_params=_cparams(("arbitrary",)),
        name="moe_unsort_residual",
    )(*tables, ys, slots, x, final_g, mods, mods)


def _sum_before(a, axis):
    i = jnp.arange(a.shape[axis])
    mask = (i[None, :] < i[:, None]).astype(a.dtype)
    moved = jnp.moveaxis(a, axis, -1)
    return jnp.moveaxis(jnp.sum(moved[..., None, :] * mask, axis=-1), -1, axis)


def _moe_tables(cnt_blocks, n_blocks):
    cnt = cnt_blocks[:, :, 0].astype(jnp.int32)
    per = SORT_TILE // ROW_TILE
    cnt = cnt.reshape(cnt.shape[0] // per, per, N_EXPERTS).sum(axis=1)
    seg = (cnt + SEG_ALIGN - 1) // SEG_ALIGN * SEG_ALIGN
    off = _sum_before(seg, 1)
    blocks_e = (seg.sum(axis=0) + MOE_BLOCK - 1) // MOE_BLOCK
    first_block = _sum_before(blocks_e, 0)
    gstart = first_block[None, :] * MOE_BLOCK + _sum_before(seg, 0)
    blk = jnp.arange(n_blocks, dtype=jnp.int32)
    last_block = first_block + blocks_e
    blk_expert = jnp.minimum(jnp.sum((blk[:, None] >= last_block[None, :]).astype(jnp.int32), axis=1),
                             N_EXPERTS - 1)
    blk_valid = (blk < blocks_e.sum()).astype(jnp.int32)
    tables = (off.reshape(-1).astype(jnp.int32), gstart.reshape(-1).astype(jnp.int32),
              (seg // SEG_ALIGN).reshape(-1).astype(jnp.int32))
    offcol = jnp.broadcast_to(off.astype(F32)[:, :, None], off.shape + (128,))
    return tables, offcol, blk_expert.astype(jnp.int32), blk_valid


def _moe_call(h2, comb_t, cnt_blocks, x, w1, w3, w2, final_g, mods, layer, mod_row, tile_off, final):
    rows = x.shape[0]
    n_tiles = rows // SORT_TILE
    max_rows = 2 * rows + n_tiles * N_EXPERTS * (SEG_ALIGN - 1)
    n_blocks = max_rows // MOE_BLOCK + N_EXPERTS
    tables, offcol, blk_expert, blk_valid = _moe_tables(cnt_blocks, n_blocks)
    slots, xs = _sort_call(h2, comb_t, tables, offcol, n_blocks * MOE_BLOCK)
    ys = _experts_call(xs, blk_expert, blk_valid, w1, w3, w2, layer)
    return _unsort_call(ys, slots, x, tables, final_g, mods, layer, mod_row, tile_off, final)


def kernel(x, c, ctx, c_ctx, w_ada, b_ada, norm1_g, norm2_g, w_in, a_ln_g, a_w_s, a_b_s, b_conv_w, b_A_log, b_dt_bias, b_norm_g, w_proj_a, w_proj_b, w_out, w_router, b_router, w_e1, w_e3, w_e2, final_g):
    batch, seq_len, d = x.shape
    ctx_len = ctx.shape[1]
    n_layers = w_ada.shape[0]
    assert d == D_MODEL and batch + 1 <= COND_ROWS
    assert ctx_len % ROW_TILE == 0 and seq_len % ROW_TILE == 0
    assert (batch * ctx_len) % SORT_TILE == 0 and (batch * seq_len) % SORT_TILE == 0
    assert (batch * ctx_len) % IN_TILE == 0 and seq_len % IN_TILE == 0
    ctx_tiles = ctx_len // ROW_TILE
    lat_tiles = seq_len // ROW_TILE
    n_ctx_tiles = batch * ctx_tiles

    def mod_row(tile):
        return jnp.where(tile < n_ctx_tiles, batch, (tile - n_ctx_tiles) // lat_tiles)

    xs = (ctx.reshape(batch * ctx_len, d), x.reshape(batch * seq_len, d))
    cond = jnp.concatenate([c, c_ctx[None, :], jnp.zeros((COND_ROWS - batch - 1, d), F32)], axis=0)
    mods = _ada_call(cond, w_ada, b_ada).reshape(n_layers * 6 * COND_ROWS, 1, d)

    n_ab = 4 * N_HEADS
    ab_lo = 6 * d
    wr_t = w_router.T
    br = b_router.reshape(N_EXPERTS, 1)
    fg = final_g.reshape(1, d)
    for l in range(n_layers):
        w_all = jnp.concatenate([w_in[l][:, :ab_lo], w_in[l][:, ab_lo + n_ab:], w_in[l][:, ab_lo:ab_lo + n_ab],
                                 jnp.zeros((d, 128 - n_ab), F32)], axis=1).astype(BF16)
        u, v, q, k, vv, z, ga, gb, ab = _in_call(xs, mods, l, norm1_g[l].reshape(1, d), w_all, mod_row,
                                                 batch * ctx_len // IN_TILE)

        conv_w = jnp.pad(b_conv_w[l], ((0, 8 - CONV_TAPS), (0, 0)))
        zeros8 = jnp.zeros((N_HEADS,), F32)
        alog_row = jnp.pad(jnp.concatenate([b_A_log[l, 0], zeros8, b_A_log[l, 1], zeros8]), (0, 128 - n_ab)).reshape(1, 128)
        dtb_row = jnp.pad(jnp.concatenate([b_dt_bias[l, 0], zeros8, b_dt_bias[l, 1], zeros8]), (0, 128 - n_ab)).reshape(1, 128)
        qn, kn, vs, gates = _conv_call(q, k, vv, conv_w, ab, alog_row, dtb_row, n_ctx_tiles, ctx_tiles, lat_tiles)
        o2 = _gdn_call(qn, kn, vs, gates, batch, ctx_len, seq_len)

        bs = jnp.repeat(a_b_s[l].T, SGU_CHUNK, axis=1)
        bng = b_norm_g[l].reshape(1, HEAD_DIM)
        last = l == n_layers - 1
        tile_off = n_ctx_tiles if last else 0
        x_new, h2, comb_t, cnt = _mix_call(
            xs, u, v, o2, z, ga, gb, mods, l, norm2_g[l].reshape(1, d), a_ln_g[l].reshape(1, d),
            a_w_s[l].astype(BF16), bs, bng,
            w_proj_a[l].astype(BF16), w_proj_b[l].astype(BF16), w_out[l].astype(BF16), wr_t, br, mod_row,
            n_ctx_tiles, tile_off)
        xs = (_moe_call(h2, comb_t, cnt, x_new, w_e1, w_e3, w_e2,
                        fg, mods, l, mod_row, tile_off, final=last),)
    return xs[0].reshape(batch, seq_len, d)
```

```python
import functools

import jax
import jax.numpy as jnp
from jax import lax
from jax.experimental import pallas as pl
from jax.experimental.pallas import tpu as pltpu

F32 = jnp.float32
BF16 = jnp.bfloat16
HIGHEST = lax.Precision.HIGHEST

EPS = 1e-6
D_MODEL = 1024
N_HEADS = 8
HEAD_DIM = 128
SGU_CHUNK = 128
SGU_GROUPS = 8
CONV_TAPS = 5
N_EXPERTS = 16
EXPERTS_PER_GROUP = 4
N_EXPERT_GROUPS = 4
D_EXPERT = 512
ROW_TILE = 256
IN_TILE = 512
MIX_TILE = 512
GDN_CHUNK = 128
GDN_BLOCK = 2
SORT_TILE = 512
SEG_ALIGN = 16
SORT_CAP = 1280
MOE_BLOCK = 512
SEG_BITS = (32, 16, 8, 4, 2, 1)
COND_ROWS = 8
HALO = 16
VMEM_LIMIT = 56 * 1024 * 1024


def _cparams(sem):
    return pltpu.CompilerParams(dimension_semantics=sem, vmem_limit_bytes=VMEM_LIMIT)


def _sigmoid(x):
    return 0.5 * jnp.tanh(0.5 * x) + 0.5


def _silu(x):
    return x * _sigmoid(x)


def _gelu_tanh(x):
    return 0.5 * x * (1.0 + jnp.tanh(0.7978845608028654 * (x + 0.044715 * (x * x * x))))


def _softplus(x):
    return jnp.maximum(x, 0.0) + jnp.log1p(jnp.exp(-jnp.abs(x)))


def _ada_kernel(cond_ref, w_ref, b_ref, o_ref):
    s = _silu(cond_ref[...])
    o_ref[0, 0] = jnp.dot(s, w_ref[0], precision=HIGHEST, preferred_element_type=F32) + b_ref[0, 0]


def _ada_call(cond, w_ada, b_ada):
    n_layers = w_ada.shape[0]
    d = D_MODEL
    return pl.pallas_call(
        _ada_kernel,
        grid=(n_layers, 6),
        in_specs=[
            pl.BlockSpec((COND_ROWS, d), lambda l, j: (0, 0)),
            pl.BlockSpec((1, d, d), lambda l, j: (l, 0, j)),
            pl.BlockSpec((1, 1, 1, d), lambda l, j: (l, j, 0, 0)),
        ],
        out_specs=pl.BlockSpec((1, 1, COND_ROWS, d), lambda l, j: (l, j, 0, 0)),
        out_shape=jax.ShapeDtypeStruct((n_layers, 6, COND_ROWS, d), F32),
        compiler_params=_cparams(("arbitrary", "arbitrary")),
        name="ada_params",
    )(cond, w_ada, b_ada.reshape(n_layers, 6, 1, d))


def _load_rows(x_refs, tile, n_ctx_tiles):
    if len(x_refs) == 1:
        return x_refs[0][...]
    return jnp.where(tile < n_ctx_tiles, x_refs[0][...], x_refs[1][...])


def _row_specs(xs, n_ctx_tiles, tile_off, d, tile=ROW_TILE):
    if len(xs) == 1:
        return [pl.BlockSpec((tile, d), lambda i: (i + tile_off, 0))]
    return [pl.BlockSpec((tile, d), lambda i: (jnp.minimum(i + tile_off, n_ctx_tiles - 1), 0)),
            pl.BlockSpec((tile, d), lambda i: (jnp.maximum(i + tile_off - n_ctx_tiles, 0), 0))]


def _in_kernel(n_src, n_ctx_tiles, *refs):
    x_refs = refs[:n_src]
    (sh_ref, sc_ref, g_ref, w_ref,
     u_ref, v_ref, q_ref, k_ref, vv_ref, z_ref, ga_ref, gb_ref, ab_ref) = refs[n_src:]
    x = _load_rows(x_refs, pl.program_id(0), n_ctx_tiles)
    ms = jnp.mean(x * x, axis=-1, keepdims=True)
    h = x * lax.rsqrt(ms + EPS) * g_ref[...]
    h = h * (1.0 + sc_ref[0]) + sh_ref[0]
    hb = h.astype(BF16)
    d = D_MODEL
    outs = (u_ref, v_ref, q_ref, k_ref, vv_ref, z_ref, ga_ref, gb_ref)
    for n, o_ref in enumerate(outs):
        o_ref[...] = jnp.dot(hb, w_ref[:, n * d:(n + 1) * d],
                             preferred_element_type=F32).astype(BF16)
    ab_ref[...] = jnp.dot(hb, w_ref[:, 8 * d:8 * d + 128], preferred_element_type=F32)


def _in_call(xs, mods, layer, norm_g, w_all, mod_row, n_ctx_tiles):
    d = D_MODEL
    rows = sum(a.shape[0] for a in xs)
    nt = rows // IN_TILE
    per = IN_TILE // ROW_TILE
    sh_idx = (layer * 6 + 0) * COND_ROWS
    sc_idx = (layer * 6 + 1) * COND_ROWS
    row_spec = pl.BlockSpec((IN_TILE, d), lambda i: (i, 0))
    out_shapes = [jax.ShapeDtypeStruct((rows, d), BF16)] * 8 + [jax.ShapeDtypeStruct((rows, 128), F32)]
    return pl.pallas_call(
        functools.partial(_in_kernel, len(xs), n_ctx_tiles),
        grid=(nt,),
        in_specs=_row_specs(xs, n_ctx_tiles, 0, d, IN_TILE) + [
            pl.BlockSpec((1, 1, d), lambda i: (sh_idx + mod_row(i * per), 0, 0)),
            pl.BlockSpec((1, 1, d), lambda i: (sc_idx + mod_row(i * per), 0, 0)),
            pl.BlockSpec((1, d), lambda i: (0, 0)),
            pl.BlockSpec((d, 8 * d + 128), lambda i: (0, 0), pipeline_mode=pl.Buffered(1)),
        ],
        out_specs=[row_spec] * 8 + [pl.BlockSpec((IN_TILE, 128), lambda i: (i, 0))],
        out_shape=out_shapes,
        compiler_params=_cparams(("arbitrary",)),
        name="norm_in_proj",
    )(*xs, mods, mods, norm_g, w_all)


def _conv_kernel(n_ctx_tiles, ctx_tiles, lat_tiles,
                 q_ref, qp_ref, qn_ref, k_ref, kp_ref, kn_ref, v_ref, vp_ref, vn_ref,
                 cw_ref, ab_ref, alog_ref, dtb_ref,
                 qo_ref, ko_ref, vo_ref, gate_ref, ext_ref):
    i = pl.program_id(0)
    j = jnp.where(i < n_ctx_tiles, i % ctx_tiles, (i - n_ctx_tiles) % lat_tiles)
    last = jnp.where(i < n_ctx_tiles, ctx_tiles - 1, lat_tiles - 1)
    has_prev = (j != 0).astype(F32)
    has_next = (j != last).astype(F32)
    pad = CONV_TAPS // 2
    srcs = ((q_ref, qp_ref, qn_ref, qo_ref, True, HEAD_DIM ** -0.5),
            (k_ref, kp_ref, kn_ref, ko_ref, True, 1.0),
            (v_ref, vp_ref, vn_ref, vo_ref, False, 1.0))
    for t, (m_ref, p_ref, n_ref, o_ref, l2, scale) in enumerate(srcs):
        ext_ref[0:8, :] = p_ref[HALO - 8:HALO, :].astype(F32) * has_prev
        ext_ref[8:8 + ROW_TILE, :] = m_ref[...].astype(F32)
        ext_ref[8 + ROW_TILE:16 + ROW_TILE, :] = n_ref[0:8, :].astype(F32) * has_next
        for h in range(N_HEADS):
            lanes = slice(h * HEAD_DIM, (h + 1) * HEAD_DIM)
            acc = None
            for tap in range(CONV_TAPS):
                w_row = cw_ref[tap:tap + 1, t * D_MODEL + h * HEAD_DIM:t * D_MODEL + (h + 1) * HEAD_DIM]
                start = 8 - pad + tap
                term = ext_ref[start:start + ROW_TILE, lanes] * w_row
                acc = term if acc is None else acc + term
            y = _silu(acc)
            if l2:
                ss = jnp.sum(y * y, axis=-1, keepdims=True)
                y = y * (lax.rsqrt(ss + EPS) * scale)
            o_ref[:, lanes] = y.astype(BF16)
    ab = ab_ref[...]
    lane = lax.broadcasted_iota(jnp.int32, ab.shape, 1)
    is_decay = ((lane // N_HEADS) % 2) == 0
    g = -jnp.exp(alog_ref[...]) * _softplus(ab + dtb_ref[...])
    gate = jnp.where(is_decay, g, _sigmoid(ab))
    gate = jnp.where(lane < 4 * N_HEADS, gate, 0.0)
    gate_ref[0] = gate
    gate_ref[1] = pltpu.roll(gate, 128 - 2 * N_HEADS, axis=1)


def _conv_call(q, k, v, conv_w, ab, alog_row, dtb_row, n_ctx_tiles, ctx_tiles, lat_tiles):
    rows, d = q.shape
    nt = rows // ROW_TILE
    per = ROW_TILE // HALO
    n_halo = rows // HALO
    main = pl.BlockSpec((ROW_TILE, d), lambda i: (i, 0))
    prev = pl.BlockSpec((HALO, d), lambda i: (jnp.maximum(i * per - 1, 0), 0))
    nxt = pl.BlockSpec((HALO, d), lambda i: (jnp.minimum((i + 1) * per, n_halo - 1), 0))
    small = pl.BlockSpec((1, 128), lambda i: (0, 0))
    return pl.pallas_call(
        functools.partial(_conv_kernel, n_ctx_tiles, ctx_tiles, lat_tiles),
        grid=(nt,),
        in_specs=[main, prev, nxt] * 3 + [
            pl.BlockSpec((8, 3 * d), lambda i: (0, 0)),
            pl.BlockSpec((ROW_TILE, 128), lambda i: (i, 0)),
            small, small,
        ],
        out_specs=[main, main, main, pl.BlockSpec((2, ROW_TILE, 128), lambda i: (0, i, 0))],
        out_shape=[jax.ShapeDtypeStruct((rows, d), BF16)] * 3 + [jax.ShapeDtypeStruct((2, rows, 128), F32)],
        scratch_shapes=[pltpu.VMEM((ROW_TILE + 16, d), F32)],
        compiler_params=_cparams(("arbitrary",)),
        name="conv_silu_gates",
    )(q, q, q, k, k, k, v, v, v, conv_w, ab, alog_row, dtb_row)


def _gdn_kernel(q_ref, k_ref, v_ref, gate_ref, o_ref, s_ref):
    fwd = pl.program_id(1) == 0

    @pl.when(pl.program_id(2) == 0)
    def _():
        s_ref[...] = jnp.zeros_like(s_ref)

    for n in range(GDN_BLOCK):
        start = jnp.where(fwd, n, GDN_BLOCK - 1 - n) * GDN_CHUNK
        _gdn_chunk(fwd, pl.ds(pl.multiple_of(start, GDN_CHUNK), GDN_CHUNK), q_ref, k_ref, v_ref, gate_ref, o_ref, s_ref)


def _gdn_chunk(fwd, rows, q_ref, k_ref, v_ref, gate_ref, o_ref, s_ref):
    c = GDN_CHUNK
    row = lax.broadcasted_iota(jnp.int32, (c, c), 0)
    col = lax.broadcasted_iota(jnp.int32, (c, c), 1)
    later = jnp.where(fwd, row, col)
    earlier = jnp.where(fwd, col, row)
    incl = later >= earlier
    strict = later > earlier
    eye = (row == col).astype(F32)

    gate = gate_ref[0, rows, :]
    gc_all = jnp.dot(incl.astype(F32), gate, precision=HIGHEST, preferred_element_type=F32)
    gtot_all = jnp.dot(jnp.ones((c, c), F32), gate, precision=HIGHEST, preferred_element_type=F32)
    gate_t = gate.T
    gc_t = gc_all.T
    gtot_t = gtot_all.T

    heads = range(N_HEADS)
    lanes = [slice(h * HEAD_DIM, (h + 1) * HEAD_DIM) for h in heads]
    nt_dims = (((1,), (1,)), ((), ()))
    q = [q_ref[rows, lanes[h]] for h in heads]
    k = [k_ref[rows, lanes[h]] for h in heads]
    v = [v_ref[rows, lanes[h]] for h in heads]
    gc_col = [jnp.broadcast_to(gc_all[:, h:h + 1], (c, c)) for h in heads]
    gc_row = [gc_t[h:h + 1, :] for h in heads]
    gtot_row = [gtot_t[h:h + 1, :] for h in heads]
    beta_row = [gate_t[N_HEADS + h:N_HEADS + h + 1, :] for h in heads]

    decay = [jnp.exp(jnp.where(incl, gc_col[h] - gc_row[h], -jnp.inf)) * beta_row[h] for h in heads]
    kq = [lax.dot_general(jnp.concatenate([k[h], q[h]], axis=0), k[h], nt_dims, preferred_element_type=F32)
          for h in heads]
    n_mat = [jnp.where(strict, kq[h][:c] * decay[h], 0.0) for h in heads]
    a_mat = [(kq[h][c:] * decay[h]).astype(BF16) for h in heads]
    a_b = [(eye + n_mat[h]).astype(BF16) for h in heads]
    t_inv = [eye - n_mat[h] for h in heads]
    for _ in range(6):
        tb = [t_inv[h].astype(BF16) for h in heads]
        err = [(eye - jnp.dot(a_b[h], tb[h], preferred_element_type=F32)).astype(BF16) for h in heads]
        t_inv = [t_inv[h] + jnp.dot(tb[h], err[h], preferred_element_type=F32) for h in heads]
    u_val = [jnp.dot(t_inv[h].astype(BF16), v[h], preferred_element_type=F32) for h in heads]
    w_key = [jnp.dot((t_inv[h] * jnp.exp(gc_row[h])).astype(BF16), k[h], preferred_element_type=F32)
             for h in heads]

    s_old = [s_ref[h] for h in heads]
    sb = [s_old[h].astype(BF16) for h in heads]
    wq_s = [jnp.dot(jnp.concatenate([w_key[h].astype(BF16), q[h]], axis=0), sb[h], preferred_element_type=F32)
            for h in heads]
    v_new = [(u_val[h] - wq_s[h][:c]).astype(BF16) for h in heads]
    kd_t = [(k[h].astype(F32).T * (jnp.exp(gtot_row[h] - gc_row[h]) * beta_row[h])).astype(BF16) for h in heads]
    ak_v = [jnp.dot(jnp.concatenate([a_mat[h], kd_t[h]], axis=0), v_new[h], preferred_element_type=F32)
            for h in heads]
    for h in heads:
        s_ref[h] = s_old[h] * jnp.exp(gtot_row[h]) + ak_v[h][c:]
        o_ref[0, rows, lanes[h]] = (jnp.exp(gc_col[h]) * wq_s[h][c:] + ak_v[h][:c]).astype(BF16)


def _gdn_call(q, k, v, gates, batch, ctx_len, seq_len):
    rows, d = q.shape
    c = GDN_BLOCK * GDN_CHUNK
    assert ctx_len % c == 0 and seq_len % c == 0
    n_ctx = ctx_len // c
    n_lat = seq_len // c
    n_steps = n_ctx + n_lat

    def chunk(b, dr, s):
        rev = jnp.where(s < n_ctx, n_ctx - 1 - s, n_ctx + n_steps - 1 - s)
        pos = jnp.where(dr == 0, s, rev)
        return jnp.where(pos < n_ctx, b * n_ctx + pos, batch * n_ctx + b * n_lat + pos - n_ctx)

    blk = pl.BlockSpec((c, d), lambda b, dr, s: (chunk(b, dr, s), 0))
    return pl.pallas_call(
        _gdn_kernel,
        grid=(batch, 2, n_steps),
        in_specs=[blk, blk, blk, pl.BlockSpec((1, c, 128), lambda b, dr, s: (dr, chunk(b, dr, s), 0))],
        out_specs=pl.BlockSpec((1, c, d), lambda b, dr, s: (dr, chunk(b, dr, s), 0)),
        out_shape=jax.ShapeDtypeStruct((2, rows, d), BF16),
        scratch_shapes=[pltpu.VMEM((N_HEADS, HEAD_DIM, HEAD_DIM), F32)],
        compiler_params=_cparams(("arbitrary", "arbitrary", "arbitrary")),
        name="gated_delta",
    )(q, k, v, gates)


def _mix_kernel(n_src, n_ctx_tiles, tile_off, *refs):
    x_refs = refs[:n_src]
    (u_ref, v_ref, of_ref, ob_ref, z_ref, ga_ref, gb_ref,
     g1_ref, sh2_ref, sc2_ref, n2g_ref, lng_ref, ws_ref, bs_ref, bng_ref,
     wpa_ref, wpb_ref, wout_ref, wr_ref, br_ref,
     xo_ref, h2_ref, comb_ref, cnt_ref, sa_ref, sb_ref) = refs[n_src:]
    d = D_MODEL
    ug = _gelu_tanh(u_ref[...].astype(F32))
    vg = _gelu_tanh(v_ref[...].astype(F32))
    mu = jnp.mean(vg, axis=-1, keepdims=True)
    vc = vg - mu
    var = jnp.mean(vc * vc, axis=-1, keepdims=True)
    vn = (vc * lax.rsqrt(var + EPS) * lng_ref[...]).astype(BF16)
    for ch in range(MIX_TILE // SGU_CHUNK):
        rws = slice(ch * SGU_CHUNK, (ch + 1) * SGU_CHUNK)
        for g in range(SGU_GROUPS):
            lanes = slice(g * 128, (g + 1) * 128)
            mixed = jnp.dot(ws_ref[g], vn[rws, lanes], preferred_element_type=F32) + bs_ref[:, lanes]
            sa_ref[rws, lanes] = (ug[rws, lanes] * mixed).astype(BF16)
    y_a = jnp.dot(sa_ref[...], wpa_ref[...], preferred_element_type=F32)
    for h in range(N_HEADS):
        lanes = slice(h * HEAD_DIM, (h + 1) * HEAD_DIM)
        o = of_ref[0, :, lanes].astype(F32) + ob_ref[0, :, lanes].astype(F32)
        ms = jnp.mean(o * o, axis=-1, keepdims=True)
        o = o * lax.rsqrt(ms + EPS) * bng_ref[...]
        sb_ref[:, lanes] = (o * _silu(z_ref[:, lanes].astype(F32))).astype(BF16)
    y_b = jnp.dot(sb_ref[...], wpb_ref[...], preferred_element_type=F32)
    merged = _sigmoid(ga_ref[...].astype(F32)) * y_a + _sigmoid(gb_ref[...].astype(F32)) * y_b
    y = jnp.dot(merged.astype(BF16), wout_ref[...], preferred_element_type=F32)
    xn = _load_rows(x_refs, pl.program_id(0) + tile_off, n_ctx_tiles) + g1_ref[0] * y
    xo_ref[...] = xn
    ms = jnp.mean(xn * xn, axis=-1, keepdims=True)
    h2 = xn * lax.rsqrt(ms + EPS) * n2g_ref[...]
    h2 = h2 * (1.0 + sc2_ref[0]) + sh2_ref[0]
    h2_ref[...] = h2.astype(BF16)
    logits = lax.dot_general(wr_ref[...], h2, (((1,), (1,)), ((), ())),
                             precision=HIGHEST, preferred_element_type=F32)
    scores = _sigmoid(logits)
    sel = scores + br_ref[...]
    srow = [sel[e:e + 1, :] for e in range(N_EXPERTS)]
    grp = []
    for g in range(N_EXPERT_GROUPS):
        m = srow[4 * g:4 * g + 4]
        best2 = None
        for a in range(4):
            for b in range(a + 1, 4):
                pair = m[a] + m[b]
                best2 = pair if best2 is None else jnp.maximum(best2, pair)
        grp.append(best2)
    best_val = grp[0]
    best_idx = jnp.zeros_like(best_val, dtype=jnp.int32)
    for g in range(1, N_EXPERT_GROUPS):
        better = grp[g] > best_val
        best_val = jnp.where(better, grp[g], best_val)
        best_idx = jnp.where(better, g, best_idx)
    picked = []
    for e in range(N_EXPERTS):
        g = e // EXPERTS_PER_GROUP
        rank = jnp.zeros_like(best_idx)
        for o_e in range(4 * g, 4 * g + 4):
            if o_e == e:
                continue
            ahead = (srow[o_e] > srow[e]) if o_e > e else (srow[o_e] >= srow[e])
            rank = rank + ahead.astype(jnp.int32)
        chosen = jnp.logical_and(best_idx == g, rank < 2)
        picked.append(jnp.where(chosen, scores[e:e + 1, :], 0.0))
    total = picked[0]
    for e in range(1, N_EXPERTS):
        total = total + picked[e]
    inv = 1.0 / total
    for e in range(N_EXPERTS):
        w_e = picked[e] * inv
        comb_ref[e:e + 1, :] = w_e
        n_e = jnp.sum(jnp.where(w_e > 0.0, 1.0, 0.0), axis=-1, keepdims=True)
        cnt_ref[0, e:e + 1, :] = jnp.broadcast_to(n_e, (1, 128))


def _mix_call(xs, u, v, o2, z, ga, gb, mods, layer, norm2_g, ln_g, ws, bs, bng,
              wpa, wpb, wout, wr_t, br, mod_row, n_ctx_tiles, tile_off):
    d = D_MODEL
    nt = sum(a.shape[0] for a in xs) // MIX_TILE - tile_off
    rows = nt * MIX_TILE
    per = MIX_TILE // ROW_TILE
    g1_idx = (layer * 6 + 2) * COND_ROWS
    sh2_idx = (layer * 6 + 3) * COND_ROWS
    sc2_idx = (layer * 6 + 4) * COND_ROWS
    row_spec = pl.BlockSpec((MIX_TILE, d), lambda i: (i + tile_off, 0))
    out_spec = pl.BlockSpec((MIX_TILE, d), lambda i: (i, 0))
    vec = pl.BlockSpec((1, d), lambda i: (0, 0))
    wspec = pl.BlockSpec((d, d), lambda i: (0, 0))

    def mod(idx):
        return pl.BlockSpec((1, 1, d), lambda i: (idx + mod_row((i + tile_off) * per), 0, 0))

    return pl.pallas_call(
        functools.partial(_mix_kernel, len(xs), n_ctx_tiles, tile_off),
        grid=(nt,),
        in_specs=_row_specs(xs, n_ctx_tiles, tile_off, d, MIX_TILE) + [
            row_spec, row_spec,
            pl.BlockSpec((1, MIX_TILE, d), lambda i: (0, i + tile_off, 0)),
            pl.BlockSpec((1, MIX_TILE, d), lambda i: (1, i + tile_off, 0)),
            row_spec, row_spec, row_spec,
            mod(g1_idx), mod(sh2_idx), mod(sc2_idx),
            vec, vec,
            pl.BlockSpec((SGU_GROUPS, SGU_CHUNK, SGU_CHUNK), lambda i: (0, 0, 0)),
            pl.BlockSpec((SGU_CHUNK, d), lambda i: (0, 0)),
            pl.BlockSpec((1, HEAD_DIM), lambda i: (0, 0)),
            wspec, wspec, wspec,
            pl.BlockSpec((N_EXPERTS, d), lambda i: (0, 0)),
            pl.BlockSpec((N_EXPERTS, 1), lambda i: (0, 0)),
        ],
        out_specs=[out_spec, out_spec, pl.BlockSpec((N_EXPERTS, MIX_TILE), lambda i: (0, i)),
                   pl.BlockSpec((1, N_EXPERTS, 128), lambda i: (i, 0, 0))],
        out_shape=[jax.ShapeDtypeStruct((rows, d), F32), jax.ShapeDtypeStruct((rows, d), BF16),
                   jax.ShapeDtypeStruct((N_EXPERTS, rows), F32),
                   jax.ShapeDtypeStruct((nt, N_EXPERTS, 128), F32)],
        scratch_shapes=[pltpu.VMEM((MIX_TILE, d), BF16), pltpu.VMEM((MIX_TILE, d), BF16)],
        compiler_params=_cparams(("arbitrary",)),
        name="mix_merge_router",
    )(*xs, u, v, o2, o2, z, ga, gb, mods, mods, mods, norm2_g, ln_g, ws, bs, bng,
      wpa, wpb, wout, wr_t, br)


def _segment_copies(t, off_s, gs_s, l16_s, local_ref, global_ref, sem, to_global, wait):
    for e in range(N_EXPERTS):
        idx = t * N_EXPERTS + e
        lo = off_s[idx]
        go = gs_s[idx]
        n16 = l16_s[idx]
        done = jnp.int32(0)
        for bit in SEG_BITS:
            size = bit * SEG_ALIGN
            present = (n16 & bit) != 0
            l_at = local_ref.at[pl.ds(pl.multiple_of(lo + done, SEG_ALIGN), size)]
            g_at = global_ref.at[pl.ds(pl.multiple_of(go + done, SEG_ALIGN), size)]
            cp = pltpu.make_async_copy(l_at, g_at, sem) if to_global else pltpu.make_async_copy(g_at, l_at, sem)

            @pl.when(present)
            def _():
                if wait:
                    cp.wait()
                else:
                    cp.start()

            done = done + jnp.where(present, size, 0)


def _sort_kernel(off_s, gs_s, l16_s, h_ref, comb_ref, offcol_ref, xs_in_ref, slots_ref, xs_ref, hs_ref, sem):
    del xs_in_ref
    t = pl.program_id(0)
    slot = t % 2
    other = 1 - slot
    ts = SORT_TILE
    comb = comb_ref[...]
    asg = comb > 0.0
    row = lax.broadcasted_iota(jnp.int32, (ts, ts), 0)
    col = lax.broadcasted_iota(jnp.int32, (ts, ts), 1)
    before = jnp.where(row < col, 1.0, 0.0).astype(BF16)
    rank = jnp.dot(jnp.where(asg, 1.0, 0.0).astype(BF16), before, preferred_element_type=F32)
    pos = offcol_ref[0][:, 0:1] + rank
    p_lo = jnp.min(jnp.where(asg, pos, 1e9), axis=0, keepdims=True)
    p_hi = jnp.max(jnp.where(asg, pos, -1.0), axis=0, keepdims=True)
    w_lo = jnp.sum(jnp.where(jnp.logical_and(asg, pos == p_lo), comb, 0.0), axis=0, keepdims=True)
    w_hi = jnp.sum(jnp.where(jnp.logical_and(asg, pos == p_hi), comb, 0.0), axis=0, keepdims=True)
    w_hi = jnp.where(p_hi > p_lo, w_hi, 0.0)
    dest = lax.broadcasted_iota(jnp.int32, (SORT_CAP, ts), 0).astype(F32)
    onehot = jnp.where(jnp.logical_or(dest == p_lo, dest == p_hi), 1.0, 0.0).astype(BF16)
    hs_ref[slot] = jnp.dot(onehot, h_ref[...], preferred_element_type=F32).astype(BF16)
    slot_rows = jnp.concatenate([p_lo, p_hi, w_lo, w_hi, jnp.zeros((124, ts), F32)], axis=0)
    slots_ref[...] = slot_rows.T
    _segment_copies(t, off_s, gs_s, l16_s, hs_ref.at[slot], xs_ref, sem.at[slot], to_global=True, wait=False)

    @pl.when(t > 0)
    def _():
        _segment_copies(t - 1, off_s, gs_s, l16_s, hs_ref.at[other], xs_ref, sem.at[other], to_global=True, wait=True)

    @pl.when(t == pl.num_programs(0) - 1)
    def _():
        _segment_copies(t, off_s, gs_s, l16_s, hs_ref.at[slot], xs_ref, sem.at[slot], to_global=True, wait=True)


def _sort_call(h2, comb_t, tables, offcol, n_rows_sorted):
    rows, d = h2.shape
    nt = rows // SORT_TILE
    xs0 = jnp.zeros((n_rows_sorted, d), BF16)
    grid_spec = pltpu.PrefetchScalarGridSpec(
        num_scalar_prefetch=3,
        grid=(nt,),
        in_specs=[
            pl.BlockSpec((SORT_TILE, d), lambda i, *_: (i, 0)),
            pl.BlockSpec((N_EXPERTS, SORT_TILE), lambda i, *_: (0, i)),
            pl.BlockSpec((1, N_EXPERTS, 128), lambda i, *_: (i, 0, 0)),
            pl.BlockSpec(memory_space=pl.ANY),
        ],
        out_specs=[
            pl.BlockSpec((SORT_TILE, 128), lambda i, *_: (i, 0)),
            pl.BlockSpec(memory_space=pl.ANY),
        ],
        scratch_shapes=[pltpu.VMEM((2, SORT_CAP, d), BF16), pltpu.SemaphoreType.DMA((2,))],
    )
    slots, xs = pl.pallas_call(
        _sort_kernel,
        grid_spec=grid_spec,
        out_shape=[jax.ShapeDtypeStruct((rows, 128), F32), jax.ShapeDtypeStruct((n_rows_sorted, d), BF16)],
        input_output_aliases={6: 1},
        compiler_params=_cparams(("arbitrary",)),
        name="moe_sort",
    )(*tables, h2, comb_t, offcol, xs0)
    return slots, xs


def _experts_kernel(be_s, bv_s, x_ref, w1_ref, w3_ref, w2_ref, y_ref):
    b = pl.program_id(0)

    @pl.when(bv_s[b] != 0)
    def _():
        x = x_ref[...]
        a = jnp.dot(x, w1_ref[0, 0].astype(BF16), preferred_element_type=F32)
        g = jnp.dot(x, w3_ref[0, 0].astype(BF16), preferred_element_type=F32)
        hid = (_silu(a) * g).astype(BF16)
        y_ref[...] = jnp.dot(hid, w2_ref[0, 0].astype(BF16), preferred_element_type=F32).astype(BF16)

    @pl.when(bv_s[b] == 0)
    def _():
        y_ref[...] = jnp.zeros_like(y_ref)


def _experts_call(xs, blk_expert, blk_valid, w1, w3, w2, layer):
    rows, d = xs.shape
    nb = rows // MOE_BLOCK
    grid_spec = pltpu.PrefetchScalarGridSpec(
        num_scalar_prefetch=2,
        grid=(nb,),
        in_specs=[
            pl.BlockSpec((MOE_BLOCK, d), lambda b, be, bv: (b, 0)),
            pl.BlockSpec((1, 1, d, D_EXPERT), lambda b, be, bv: (layer, be[b], 0, 0)),
            pl.BlockSpec((1, 1, d, D_EXPERT), lambda b, be, bv: (layer, be[b], 0, 0)),
            pl.BlockSpec((1, 1, D_EXPERT, d), lambda b, be, bv: (layer, be[b], 0, 0)),
        ],
        out_specs=pl.BlockSpec((MOE_BLOCK, d), lambda b, be, bv: (b, 0)),
    )
    return pl.pallas_call(
        _experts_kernel,
        grid_spec=grid_spec,
        out_shape=jax.ShapeDtypeStruct((rows, d), BF16),
        compiler_params=_cparams(("arbitrary",)),
        name="moe_experts",
    )(blk_expert, blk_valid, xs, w1, w3, w2)


def _unsort_kernel(final, off_s, gs_s, l16_s, ys_ref, slots_ref, x_ref, fg_ref, g2a_ref, g2b_ref,
                   o_ref, yt_ref, sem):
    t = pl.program_id(0)
    ts = SORT_TILE
    slot = t % 2
    other = 1 - slot

    def fetch(tile, buf):
        yt_ref[buf] = jnp.zeros(yt_ref.shape[1:], yt_ref.dtype)
        _segment_copies(tile, off_s, gs_s, l16_s, yt_ref.at[buf], ys_ref, sem.at[buf], to_global=False, wait=False)

    @pl.when(t == 0)
    def _():
        fetch(t, slot)

    @pl.when(t + 1 < pl.num_programs(0))
    def _():
        fetch(t + 1, other)

    _segment_copies(t, off_s, gs_s, l16_s, yt_ref.at[slot], ys_ref, sem.at[slot], to_global=False, wait=True)
    slots = slots_ref[...]
    src = lax.broadcasted_iota(jnp.int32, (ts, SORT_CAP), 1).astype(F32)
    weights = jnp.where(src == slots[:, 0:1], slots[:, 2:3], 0.0) + jnp.where(src == slots[:, 1:2], slots[:, 3:4], 0.0)
    y = jnp.dot(weights.astype(BF16), yt_ref[slot], preferred_element_type=F32)
    for s, g2_ref in enumerate((g2a_ref, g2b_ref)):
        rws = slice(s * ROW_TILE, (s + 1) * ROW_TILE)
        xn = x_ref[rws, :] + g2_ref[0] * y[rws, :]
        if final:
            ms = jnp.mean(xn * xn, axis=-1, keepdims=True)
            xn = xn * lax.rsqrt(ms + EPS) * fg_ref[...]
        o_ref[rws, :] = xn


def _unsort_call(ys, slots, x, tables, final_g, mods, layer, mod_row, tile_off, final):
    rows, d = x.shape
    nt = rows // SORT_TILE
    sub = SORT_TILE // ROW_TILE
    g2_idx = (layer * 6 + 5) * COND_ROWS
    row_spec = pl.BlockSpec((SORT_TILE, d), lambda i, *_: (i, 0))

    def mod(s):
        return pl.BlockSpec((1, 1, d), lambda i, *_: (g2_idx + mod_row(i * sub + s + tile_off), 0, 0))

    grid_spec = pltpu.PrefetchScalarGridSpec(
        num_scalar_prefetch=3,
        grid=(nt,),
        in_specs=[
            pl.BlockSpec(memory_space=pl.ANY),
            pl.BlockSpec((SORT_TILE, 128), lambda i, *_: (i, 0)),
            row_spec,
            pl.BlockSpec((1, d), lambda i, *_: (0, 0)),
            mod(0), mod(1),
        ],
        out_specs=row_spec,
        scratch_shapes=[pltpu.VMEM((2, SORT_CAP, d), BF16), pltpu.SemaphoreType.DMA((2,))],
    )
    return pl.pallas_call(
        functools.partial(_unsort_kernel, final),
        grid_spec=grid_spec,
        out_shape=jax.ShapeDtypeStruct((rows, d), F32),
        compiler_params=_cparams(("arbitrary",)),
        name="moe_unsort_residual",
    )(*tables, ys, slots, x, final_g, mods, mods)


def _sum_before(a, axis):
    i = jnp.arange(a.shape[axis])
    mask = (i[None, :] < i[:, None]).astype(a.dtype)
    moved = jnp.moveaxis(a, axis, -1)
    return jnp.moveaxis(jnp.sum(moved[..., None, :] * mask, axis=-1), -1, axis)


def _moe_tables(cnt_blocks, n_blocks):
    cnt = cnt_blocks[:, :, 0].astype(jnp.int32)
    seg = (cnt + SEG_ALIGN - 1) // SEG_ALIGN * SEG_ALIGN
    off = _sum_before(seg, 1)
    blocks_e = (seg.sum(axis=0) + MOE_BLOCK - 1) // MOE_BLOCK
    first_block = _sum_before(blocks_e, 0)
    gstart = first_block[None, :] * MOE_BLOCK + _sum_before(seg, 0)
    blk = jnp.arange(n_blocks, dtype=jnp.int32)
    last_block = first_block + blocks_e
    blk_expert = jnp.minimum(jnp.sum((blk[:, None] >= last_block[None, :]).astype(jnp.int32), axis=1),
                             N_EXPERTS - 1)
    blk_valid = (blk < blocks_e.sum()).astype(jnp.int32)
    tables = (off.reshape(-1).astype(jnp.int32), gstart.reshape(-1).astype(jnp.int32),
              (seg // SEG_ALIGN).reshape(-1).astype(jnp.int32))
    offcol = jnp.broadcast_to(off.astype(F32)[:, :, None], off.shape + (128,))
    return tables, offcol, blk_expert.astype(jnp.int32), blk_valid


def _moe_call(h2, comb_t, cnt_blocks, x, w1, w3, w2, final_g, mods, layer, mod_row, tile_off, final):
    rows = x.shape[0]
    n_tiles = rows // SORT_TILE
    max_rows = 2 * rows + n_tiles * N_EXPERTS * (SEG_ALIGN - 1)
    n_blocks = max_rows // MOE_BLOCK + N_EXPERTS
    tables, offcol, blk_expert, blk_valid = _moe_tables(cnt_blocks, n_blocks)
    slots, xs = _sort_call(h2, comb_t, tables, offcol, n_blocks * MOE_BLOCK)
    ys = _experts_call(xs, blk_expert, blk_valid, w1, w3, w2, layer)
    return _unsort_call(ys, slots, x, tables, final_g, mods, layer, mod_row, tile_off, final)


def kernel(x, c, ctx, c_ctx, w_ada, b_ada, norm1_g, norm2_g, w_in, a_ln_g, a_w_s, a_b_s, b_conv_w, b_A_log, b_dt_bias, b_norm_g, w_proj_a, w_proj_b, w_out, w_router, b_router, w_e1, w_e3, w_e2, final_g):
    batch, seq_len, d = x.shape
    ctx_len = ctx.shape[1]
    n_layers = w_ada.shape[0]
    assert d == D_MODEL and batch + 1 <= COND_ROWS
    assert ctx_len % ROW_TILE == 0 and seq_len % ROW_TILE == 0
    assert (batch * ctx_len) % SORT_TILE == 0 and (batch * seq_len) % SORT_TILE == 0
    assert (batch * ctx_len) % IN_TILE == 0 and seq_len % IN_TILE == 0
    assert MIX_TILE == SORT_TILE and (batch * ctx_len) % MIX_TILE == 0 and seq_len % MIX_TILE == 0
    ctx_tiles = ctx_len // ROW_TILE
    lat_tiles = seq_len // ROW_TILE
    n_ctx_tiles = batch * ctx_tiles

    def mod_row(tile):
        return jnp.where(tile < n_ctx_tiles, batch, (tile - n_ctx_tiles) // lat_tiles)

    xs = (ctx.reshape(batch * ctx_len, d), x.reshape(batch * seq_len, d))
    cond = jnp.concatenate([c, c_ctx[None, :], jnp.zeros((COND_ROWS - batch - 1, d), F32)], axis=0)
    mods = _ada_call(cond, w_ada, b_ada).reshape(n_layers * 6 * COND_ROWS, 1, d)

    n_ab = 4 * N_HEADS
    ab_lo = 6 * d
    wr_t = w_router.T
    br = b_router.reshape(N_EXPERTS, 1)
    fg = final_g.reshape(1, d)
    for l in range(n_layers):
        w_bf = w_in[l].astype(BF16)
        w_all = jnp.concatenate([w_bf[:, :ab_lo], w_bf[:, ab_lo + n_ab:], w_bf[:, ab_lo:ab_lo + n_ab],
                                 jnp.zeros((d, 128 - n_ab), BF16)], axis=1)
        u, v, q, k, vv, z, ga, gb, ab = _in_call(xs, mods, l, norm1_g[l].reshape(1, d), w_all, mod_row,
                                                 batch * ctx_len // IN_TILE)

        conv_w = jnp.pad(b_conv_w[l], ((0, 8 - CONV_TAPS), (0, 0)))
        zeros8 = jnp.zeros((N_HEADS,), F32)
        alog_row = jnp.pad(jnp.concatenate([b_A_log[l, 0], zeros8, b_A_log[l, 1], zeros8]), (0, 128 - n_ab)).reshape(1, 128)
        dtb_row = jnp.pad(jnp.concatenate([b_dt_bias[l, 0], zeros8, b_dt_bias[l, 1], zeros8]), (0, 128 - n_ab)).reshape(1, 128)
        qn, kn, vs, gates = _conv_call(q, k, vv, conv_w, ab, alog_row, dtb_row, n_ctx_tiles, ctx_tiles, lat_tiles)
        o2 = _gdn_call(qn, kn, vs, gates, batch, ctx_len, seq_len)

        bs = jnp.repeat(a_b_s[l].T, SGU_CHUNK, axis=1)
        bng = b_norm_g[l].reshape(1, HEAD_DIM)
        last = l == n_layers - 1
        tile_off = n_ctx_tiles if last else 0
        x_new, h2, comb_t, cnt = _mix_call(
            xs, u, v, o2, z, ga, gb, mods, l, norm2_g[l].reshape(1, d), a_ln_g[l].reshape(1, d),
            a_w_s[l].astype(BF16), bs, bng,
            w_proj_a[l].astype(BF16), w_proj_b[l].astype(BF16), w_out[l].astype(BF16), wr_t, br, mod_row,
            batch * ctx_len // MIX_TILE, tile_off * ROW_TILE // MIX_TILE)
        xs = (_moe_call(h2, comb_t, cnt, x_new, w_e1, w_e3, w_e2,
                        fg, mods, l, mod_row, tile_off, final=last),)
    return xs[0].reshape(batch, seq_len, d)
```

```python
import functools

import jax
import jax.numpy as jnp
from jax import lax
from jax.experimental import pallas as pl
from jax.experimental.pallas import tpu as pltpu

F32 = jnp.float32
BF16 = jnp.bfloat16
HIGHEST = lax.Precision.HIGHEST

EPS = 1e-6
D_MODEL = 1024
N_HEADS = 8
HEAD_DIM = 128
SGU_CHUNK = 128
SGU_GROUPS = 8
CONV_TAPS = 5
N_EXPERTS = 16
EXPERTS_PER_GROUP = 4
N_EXPERT_GROUPS = 4
D_EXPERT = 512
ROW_TILE = 256
MIX_TILE = 512
GDN_CHUNK = 128
GDN_BLOCK = 2
SORT_TILE = 512
SEG_ALIGN = 16
SORT_CAP = 1280
MOE_BLOCK = 512
SEG_BITS = (32, 16, 8, 4, 2, 1)
COND_ROWS = 8
HALO = 8
VMEM_LIMIT = 56 * 1024 * 1024


def _cparams(sem):
    return pltpu.CompilerParams(dimension_semantics=sem, vmem_limit_bytes=VMEM_LIMIT)


def _sigmoid(x):
    return 0.5 * jnp.tanh(0.5 * x) + 0.5


def _silu(x):
    return x * _sigmoid(x)


def _gelu_tanh(x):
    return 0.5 * x * (1.0 + jnp.tanh(0.7978845608028654 * (x + 0.044715 * (x * x * x))))


def _softplus(x):
    return jnp.maximum(x, 0.0) + jnp.log1p(jnp.exp(-jnp.abs(x)))


def _ada_kernel(cond_ref, w_ref, b_ref, o_ref):
    s = _silu(cond_ref[...])
    o_ref[0, 0] = jnp.dot(s, w_ref[0], precision=HIGHEST, preferred_element_type=F32) + b_ref[0, 0]


def _ada_call(cond, w_ada, b_ada):
    n_layers = w_ada.shape[0]
    d = D_MODEL
    return pl.pallas_call(
        _ada_kernel,
        grid=(n_layers, 6),
        in_specs=[
            pl.BlockSpec((COND_ROWS, d), lambda l, j: (0, 0)),
            pl.BlockSpec((1, d, d), lambda l, j: (l, 0, j)),
            pl.BlockSpec((1, 1, 1, d), lambda l, j: (l, j, 0, 0)),
        ],
        out_specs=pl.BlockSpec((1, 1, COND_ROWS, d), lambda l, j: (l, j, 0, 0)),
        out_shape=jax.ShapeDtypeStruct((n_layers, 6, COND_ROWS, d), F32),
        compiler_params=_cparams(("arbitrary", "arbitrary")),
        name="ada_params",
    )(cond, w_ada, b_ada.reshape(n_layers, 6, 1, d))


def _load_rows(x_refs, tile, n_ctx_tiles):
    if len(x_refs) == 1:
        return x_refs[0][...]
    return jnp.where(tile < n_ctx_tiles, x_refs[0][...], x_refs[1][...])


def _row_specs(xs, n_ctx_tiles, tile_off, d, tile=ROW_TILE):
    if len(xs) == 1:
        return [pl.BlockSpec((tile, d), lambda i: (i + tile_off, 0))]
    return [pl.BlockSpec((tile, d), lambda i: (jnp.minimum(i + tile_off, n_ctx_tiles - 1), 0)),
            pl.BlockSpec((tile, d), lambda i: (jnp.maximum(i + tile_off - n_ctx_tiles, 0), 0))]


def _halo_specs(xs, n_ctx_tiles, d, nxt):
    per = ROW_TILE // HALO

    def spec(n_blocks, first_tile):
        if nxt:
            return pl.BlockSpec((HALO, d), lambda i: (jnp.clip((i - first_tile + 1) * per, 0, n_blocks - 1), 0))
        return pl.BlockSpec((HALO, d), lambda i: (jnp.clip((i - first_tile) * per - 1, 0, n_blocks - 1), 0))

    if len(xs) == 1:
        return [spec(xs[0].shape[0] // HALO, 0)]
    return [spec(xs[0].shape[0] // HALO, 0), spec(xs[1].shape[0] // HALO, n_ctx_tiles)]


def _in_kernel(n_src, n_ctx_tiles, ctx_tiles, lat_tiles, *refs):
    x_refs, xp_refs, xn_refs = refs[:n_src], refs[n_src:2 * n_src], refs[2 * n_src:3 * n_src]
    (sh_ref, sc_ref, g_ref, w_ref, cw_ref, alog_ref, dtb_ref,
     u_ref, v_ref, z_ref, ga_ref, gb_ref, qo_ref, ko_ref, vo_ref, gate_ref, ext_ref) = refs[3 * n_src:]
    i = pl.program_id(0)
    d = D_MODEL
    j = jnp.where(i < n_ctx_tiles, i % ctx_tiles, (i - n_ctx_tiles) % lat_tiles)
    last = jnp.where(i < n_ctx_tiles, ctx_tiles - 1, lat_tiles - 1)
    has_prev = (j != 0).astype(F32)
    has_next = (j != last).astype(F32)

    x = jnp.concatenate([_load_rows(xp_refs, i, n_ctx_tiles), _load_rows(x_refs, i, n_ctx_tiles),
                         _load_rows(xn_refs, i, n_ctx_tiles)], axis=0)
    ms = jnp.mean(x * x, axis=-1, keepdims=True)
    h = x * lax.rsqrt(ms + EPS) * g_ref[...]
    h = h * (1.0 + sc_ref[0]) + sh_ref[0]
    hb_ext = h.astype(BF16)
    hb = h[HALO:HALO + ROW_TILE].astype(BF16)
    def project(n, o_ref):
        o_ref[...] = jnp.dot(hb, w_ref[:, n * d:(n + 1) * d], preferred_element_type=F32).astype(BF16)

    pad = CONV_TAPS // 2
    lo, hi = HALO, HALO + ROW_TILE
    for t in range(3):
        proj = jnp.dot(hb_ext, w_ref[:, (2 + t) * d:(3 + t) * d], preferred_element_type=F32)
        ext_ref[t, 0:lo, :] = proj[0:lo] * has_prev
        ext_ref[t, lo:hi, :] = proj[lo:hi]
        ext_ref[t, hi:hi + HALO, :] = proj[hi:hi + HALO] * has_next
    others = (((0, u_ref), (1, v_ref)), ((5, z_ref), (6, ga_ref)), ((7, gb_ref),))
    for t, (o_ref, l2, scale) in enumerate(((qo_ref, True, HEAD_DIM ** -0.5), (ko_ref, True, 1.0),
                                            (vo_ref, False, 1.0))):
        for n, p_ref in others[t]:
            project(n, p_ref)
        for hd in range(N_HEADS):
            lanes = slice(hd * HEAD_DIM, (hd + 1) * HEAD_DIM)
            acc = None
            for tap in range(CONV_TAPS):
                w_row = cw_ref[tap:tap + 1, t * d + hd * HEAD_DIM:t * d + (hd + 1) * HEAD_DIM]
                start = lo - pad + tap
                term = ext_ref[t, start:start + ROW_TILE, lanes] * w_row
                acc = term if acc is None else acc + term
            y = _silu(acc)
            if l2:
                ss = jnp.sum(y * y, axis=-1, keepdims=True)
                y = y * (lax.rsqrt(ss + EPS) * scale)
            o_ref[:, lanes] = y.astype(BF16)
    ab = jnp.dot(hb, w_ref[:, 8 * d:8 * d + 128], preferred_element_type=F32)
    lane = lax.broadcasted_iota(jnp.int32, ab.shape, 1)
    is_decay = ((lane // N_HEADS) % 2) == 0
    g = -jnp.exp(alog_ref[...]) * _softplus(ab + dtb_ref[...])
    gate = jnp.where(is_decay, g, _sigmoid(ab))
    gate = jnp.where(lane < 4 * N_HEADS, gate, 0.0)
    gate_ref[0] = gate
    gate_ref[1] = pltpu.roll(gate, 128 - 2 * N_HEADS, axis=1)


def _in_call(xs, mods, layer, norm_g, w_all, conv_w, alog_row, dtb_row, mod_row, n_ctx_tiles, ctx_tiles, lat_tiles):
    d = D_MODEL
    rows = sum(a.shape[0] for a in xs)
    nt = rows // ROW_TILE
    sh_idx = (layer * 6 + 0) * COND_ROWS
    sc_idx = (layer * 6 + 1) * COND_ROWS
    row_spec = pl.BlockSpec((ROW_TILE, d), lambda i: (i, 0))
    small = pl.BlockSpec((1, 128), lambda i: (0, 0))
    return pl.pallas_call(
        functools.partial(_in_kernel, len(xs), n_ctx_tiles, ctx_tiles, lat_tiles),
        grid=(nt,),
        in_specs=(_row_specs(xs, n_ctx_tiles, 0, d) + _halo_specs(xs, n_ctx_tiles, d, False)
                  + _halo_specs(xs, n_ctx_tiles, d, True) + [
            pl.BlockSpec((1, 1, d), lambda i: (sh_idx + mod_row(i), 0, 0)),
            pl.BlockSpec((1, 1, d), lambda i: (sc_idx + mod_row(i), 0, 0)),
            pl.BlockSpec((1, d), lambda i: (0, 0)),
            pl.BlockSpec((d, 8 * d + 128), lambda i: (0, 0), pipeline_mode=pl.Buffered(1)),
            pl.BlockSpec((8, 3 * d), lambda i: (0, 0)),
            small, small,
        ]),
        out_specs=[row_spec] * 8 + [pl.BlockSpec((2, ROW_TILE, 128), lambda i: (0, i, 0))],
        out_shape=[jax.ShapeDtypeStruct((rows, d), BF16)] * 8 + [jax.ShapeDtypeStruct((2, rows, 128), F32)],
        scratch_shapes=[pltpu.VMEM((3, ROW_TILE + 2 * HALO, d), F32)],
        compiler_params=_cparams(("arbitrary",)),
        name="norm_in_proj_conv",
    )(*xs, *xs, *xs, mods, mods, norm_g, w_all, conv_w, alog_row, dtb_row)


def _gdn_kernel(q_ref, k_ref, v_ref, gate_ref, o_ref, s_ref):
    fwd = pl.program_id(1) == 0

    @pl.when(pl.program_id(2) == 0)
    def _():
        s_ref[...] = jnp.zeros_like(s_ref)

    for n in range(GDN_BLOCK):
        start = jnp.where(fwd, n, GDN_BLOCK - 1 - n) * GDN_CHUNK
        _gdn_chunk(fwd, pl.ds(pl.multiple_of(start, GDN_CHUNK), GDN_CHUNK), q_ref, k_ref, v_ref, gate_ref, o_ref, s_ref)


def _gdn_chunk(fwd, rows, q_ref, k_ref, v_ref, gate_ref, o_ref, s_ref):
    c = GDN_CHUNK
    row = lax.broadcasted_iota(jnp.int32, (c, c), 0)
    col = lax.broadcasted_iota(jnp.int32, (c, c), 1)
    later = jnp.where(fwd, row, col)
    earlier = jnp.where(fwd, col, row)
    incl = later >= earlier
    strict = later > earlier
    eye = (row == col).astype(F32)

    gate = gate_ref[0, rows, :]
    gc_all = jnp.dot(incl.astype(F32), gate, precision=HIGHEST, preferred_element_type=F32)
    gtot_all = jnp.dot(jnp.ones((c, c), F32), gate, precision=HIGHEST, preferred_element_type=F32)
    gate_t = gate.T
    gc_t = gc_all.T
    gtot_t = gtot_all.T

    heads = range(N_HEADS)
    lanes = [slice(h * HEAD_DIM, (h + 1) * HEAD_DIM) for h in heads]
    nt_dims = (((1,), (1,)), ((), ()))
    q = [q_ref[rows, lanes[h]] for h in heads]
    k = [k_ref[rows, lanes[h]] for h in heads]
    v = [v_ref[rows, lanes[h]] for h in heads]
    gc_col = [jnp.broadcast_to(gc_all[:, h:h + 1], (c, c)) for h in heads]
    gc_row = [gc_t[h:h + 1, :] for h in heads]
    gtot_row = [gtot_t[h:h + 1, :] for h in heads]
    beta_row = [gate_t[N_HEADS + h:N_HEADS + h + 1, :] for h in heads]

    decay = [jnp.exp(jnp.where(incl, gc_col[h] - gc_row[h], -jnp.inf)) * beta_row[h] for h in heads]
    kq = [lax.dot_general(jnp.concatenate([k[h], q[h]], axis=0), k[h], nt_dims, preferred_element_type=F32)
          for h in heads]
    n_mat = [jnp.where(strict, kq[h][:c] * decay[h], 0.0) for h in heads]
    a_mat = [(kq[h][c:] * decay[h]).astype(BF16) for h in heads]
    a_b = [(eye + n_mat[h]).astype(BF16) for h in heads]
    t_inv = [eye - n_mat[h] for h in heads]
    for _ in range(6):
        tb = [t_inv[h].astype(BF16) for h in heads]
        err = [(eye - jnp.dot(a_b[h], tb[h], preferred_element_type=F32)).astype(BF16) for h in heads]
        t_inv = [t_inv[h] + jnp.dot(tb[h], err[h], preferred_element_type=F32) for h in heads]
    u_val = [jnp.dot(t_inv[h].astype(BF16), v[h], preferred_element_type=F32) for h in heads]
    w_key = [jnp.dot((t_inv[h] * jnp.exp(gc_row[h])).astype(BF16), k[h], preferred_element_type=F32)
             for h in heads]

    s_old = [s_ref[h] for h in heads]
    sb = [s_old[h].astype(BF16) for h in heads]
    wq_s = [jnp.dot(jnp.concatenate([w_key[h].astype(BF16), q[h]], axis=0), sb[h], preferred_element_type=F32)
            for h in heads]
    v_new = [(u_val[h] - wq_s[h][:c]).astype(BF16) for h in heads]
    kd_t = [(k[h].astype(F32).T * (jnp.exp(gtot_row[h] - gc_row[h]) * beta_row[h])).astype(BF16) for h in heads]
    ak_v = [jnp.dot(jnp.concatenate([a_mat[h], kd_t[h]], axis=0), v_new[h], preferred_element_type=F32)
            for h in heads]
    for h in heads:
        s_ref[h] = s_old[h] * jnp.exp(gtot_row[h]) + ak_v[h][c:]
        o_ref[0, rows, lanes[h]] = (jnp.exp(gc_col[h]) * wq_s[h][c:] + ak_v[h][:c]).astype(BF16)


def _gdn_call(q, k, v, gates, batch, ctx_len, seq_len):
    rows, d = q.shape
    c = GDN_BLOCK * GDN_CHUNK
    assert ctx_len % c == 0 and seq_len % c == 0
    n_ctx = ctx_len // c
    n_lat = seq_len // c
    n_steps = n_ctx + n_lat

    def chunk(b, dr, s):
        rev = jnp.where(s < n_ctx, n_ctx - 1 - s, n_ctx + n_steps - 1 - s)
        pos = jnp.where(dr == 0, s, rev)
        return jnp.where(pos < n_ctx, b * n_ctx + pos, batch * n_ctx + b * n_lat + pos - n_ctx)

    blk = pl.BlockSpec((c, d), lambda b, dr, s: (chunk(b, dr, s), 0))
    return pl.pallas_call(
        _gdn_kernel,
        grid=(batch, 2, n_steps),
        in_specs=[blk, blk, blk, pl.BlockSpec((1, c, 128), lambda b, dr, s: (dr, chunk(b, dr, s), 0))],
        out_specs=pl.BlockSpec((1, c, d), lambda b, dr, s: (dr, chunk(b, dr, s), 0)),
        out_shape=jax.ShapeDtypeStruct((2, rows, d), BF16),
        scratch_shapes=[pltpu.VMEM((N_HEADS, HEAD_DIM, HEAD_DIM), F32)],
        compiler_params=_cparams(("arbitrary", "arbitrary", "arbitrary")),
        name="gated_delta",
    )(q, k, v, gates)


def _mix_kernel(n_src, n_ctx_tiles, tile_off, *refs):
    x_refs = refs[:n_src]
    (u_ref, v_ref, of_ref, ob_ref, z_ref, ga_ref, gb_ref,
     g1_ref, sh2_ref, sc2_ref, n2g_ref, lng_ref, ws_ref, bs_ref, bng_ref,
     wpa_ref, wpb_ref, wout_ref, wr_ref, br_ref,
     xo_ref, h2_ref, comb_ref, cnt_ref, sa_ref, sb_ref) = refs[n_src:]
    d = D_MODEL
    ug = _gelu_tanh(u_ref[...].astype(F32))
    vg = _gelu_tanh(v_ref[...].astype(F32))
    mu = jnp.mean(vg, axis=-1, keepdims=True)
    vc = vg - mu
    var = jnp.mean(vc * vc, axis=-1, keepdims=True)
    vn = (vc * lax.rsqrt(var + EPS) * lng_ref[...]).astype(BF16)
    for ch in range(MIX_TILE // SGU_CHUNK):
        rws = slice(ch * SGU_CHUNK, (ch + 1) * SGU_CHUNK)
        for g in range(SGU_GROUPS):
            lanes = slice(g * 128, (g + 1) * 128)
            mixed = jnp.dot(ws_ref[g], vn[rws, lanes], preferred_element_type=F32) + bs_ref[:, lanes]
            sa_ref[rws, lanes] = (ug[rws, lanes] * mixed).astype(BF16)
    y_a = jnp.dot(sa_ref[...], wpa_ref[...], preferred_element_type=F32)
    for h in range(N_HEADS):
        lanes = slice(h * HEAD_DIM, (h + 1) * HEAD_DIM)
        o = of_ref[0, :, lanes].astype(F32) + ob_ref[0, :, lanes].astype(F32)
        ms = jnp.mean(o * o, axis=-1, keepdims=True)
        o = o * lax.rsqrt(ms + EPS) * bng_ref[...]
        sb_ref[:, lanes] = (o * _silu(z_ref[:, lanes].astype(F32))).astype(BF16)
    y_b = jnp.dot(sb_ref[...], wpb_ref[...], preferred_element_type=F32)
    merged = _sigmoid(ga_ref[...].astype(F32)) * y_a + _sigmoid(gb_ref[...].astype(F32)) * y_b
    y = jnp.dot(merged.astype(BF16), wout_ref[...], preferred_element_type=F32)
    xn = _load_rows(x_refs, pl.program_id(0) + tile_off, n_ctx_tiles) + g1_ref[0] * y
    xo_ref[...] = xn
    ms = jnp.mean(xn * xn, axis=-1, keepdims=True)
    h2 = xn * lax.rsqrt(ms + EPS) * n2g_ref[...]
    h2 = h2 * (1.0 + sc2_ref[0]) + sh2_ref[0]
    h2_ref[...] = h2.astype(BF16)
    logits = lax.dot_general(wr_ref[...], h2, (((1,), (1,)), ((), ())),
                             precision=HIGHEST, preferred_element_type=F32)
    scores = _sigmoid(logits)
    sel = scores + br_ref[...]
    srow = [sel[e:e + 1, :] for e in range(N_EXPERTS)]
    grp = []
    for g in range(N_EXPERT_GROUPS):
        m = srow[4 * g:4 * g + 4]
        best2 = None
        for a in range(4):
            for b in range(a + 1, 4):
                pair = m[a] + m[b]
                best2 = pair if best2 is None else jnp.maximum(best2, pair)
        grp.append(best2)
    best_val = grp[0]
    best_idx = jnp.zeros_like(best_val, dtype=jnp.int32)
    for g in range(1, N_EXPERT_GROUPS):
        better = grp[g] > best_val
        best_val = jnp.where(better, grp[g], best_val)
        best_idx = jnp.where(better, g, best_idx)
    picked = []
    for e in range(N_EXPERTS):
        g = e // EXPERTS_PER_GROUP
        rank = jnp.zeros_like(best_idx)
        for o_e in range(4 * g, 4 * g + 4):
            if o_e == e:
                continue
            ahead = (srow[o_e] > srow[e]) if o_e > e else (srow[o_e] >= srow[e])
            rank = rank + ahead.astype(jnp.int32)
        chosen = jnp.logical_and(best_idx == g, rank < 2)
        picked.append(jnp.where(chosen, scores[e:e + 1, :], 0.0))
    total = picked[0]
    for e in range(1, N_EXPERTS):
        total = total + picked[e]
    inv = 1.0 / total
    for e in range(N_EXPERTS):
        w_e = picked[e] * inv
        comb_ref[e:e + 1, :] = w_e
        n_e = jnp.sum(jnp.where(w_e > 0.0, 1.0, 0.0), axis=-1, keepdims=True)
        cnt_ref[0, e:e + 1, :] = jnp.broadcast_to(n_e, (1, 128))


def _mix_call(xs, u, v, o2, z, ga, gb, mods, layer, norm2_g, ln_g, ws, bs, bng,
              wpa, wpb, wout, wr_t, br, mod_row, n_ctx_tiles, tile_off):
    d = D_MODEL
    nt = sum(a.shape[0] for a in xs) // MIX_TILE - tile_off
    rows = nt * MIX_TILE
    per = MIX_TILE // ROW_TILE
    g1_idx = (layer * 6 + 2) * COND_ROWS
    sh2_idx = (layer * 6 + 3) * COND_ROWS
    sc2_idx = (layer * 6 + 4) * COND_ROWS
    row_spec = pl.BlockSpec((MIX_TILE, d), lambda i: (i + tile_off, 0))
    out_spec = pl.BlockSpec((MIX_TILE, d), lambda i: (i, 0))
    vec = pl.BlockSpec((1, d), lambda i: (0, 0))
    wspec = pl.BlockSpec((d, d), lambda i: (0, 0))

    def mod(idx):
        return pl.BlockSpec((1, 1, d), lambda i: (idx + mod_row((i + tile_off) * per), 0, 0))

    return pl.pallas_call(
        functools.partial(_mix_kernel, len(xs), n_ctx_tiles, tile_off),
        grid=(nt,),
        in_specs=_row_specs(xs, n_ctx_tiles, tile_off, d, MIX_TILE) + [
            row_spec, row_spec,
            pl.BlockSpec((1, MIX_TILE, d), lambda i: (0, i + tile_off, 0)),
            pl.BlockSpec((1, MIX_TILE, d), lambda i: (1, i + tile_off, 0)),
            row_spec, row_spec, row_spec,
            mod(g1_idx), mod(sh2_idx), mod(sc2_idx),
            vec, vec,
            pl.BlockSpec((SGU_GROUPS, SGU_CHUNK, SGU_CHUNK), lambda i: (0, 0, 0)),
            pl.BlockSpec((SGU_CHUNK, d), lambda i: (0, 0)),
            pl.BlockSpec((1, HEAD_DIM), lambda i: (0, 0)),
            wspec, wspec, wspec,
            pl.BlockSpec((N_EXPERTS, d), lambda i: (0, 0)),
            pl.BlockSpec((N_EXPERTS, 1), lambda i: (0, 0)),
        ],
        out_specs=[out_spec, out_spec, pl.BlockSpec((N_EXPERTS, MIX_TILE), lambda i: (0, i)),
                   pl.BlockSpec((1, N_EXPERTS, 128), lambda i: (i, 0, 0))],
        out_shape=[jax.ShapeDtypeStruct((rows, d), F32), jax.ShapeDtypeStruct((rows, d), BF16),
                   jax.ShapeDtypeStruct((N_EXPERTS, rows), F32),
                   jax.ShapeDtypeStruct((nt, N_EXPERTS, 128), F32)],
        scratch_shapes=[pltpu.VMEM((MIX_TILE, d), BF16), pltpu.VMEM((MIX_TILE, d), BF16)],
        compiler_params=_cparams(("arbitrary",)),
        name="mix_merge_router",
    )(*xs, u, v, o2, o2, z, ga, gb, mods, mods, mods, norm2_g, ln_g, ws, bs, bng,
      wpa, wpb, wout, wr_t, br)


def _segment_copies(t, off_s, gs_s, l16_s, local_ref, global_ref, sem, to_global, wait):
    for e in range(N_EXPERTS):
        idx = t * N_EXPERTS + e
        lo = off_s[idx]
        go = gs_s[idx]
        n16 = l16_s[idx]
        done = jnp.int32(0)
        for bit in SEG_BITS:
            size = bit * SEG_ALIGN
            present = (n16 & bit) != 0
            l_at = local_ref.at[pl.ds(pl.multiple_of(lo + done, SEG_ALIGN), size)]
            g_at = global_ref.at[pl.ds(pl.multiple_of(go + done, SEG_ALIGN), size)]
            cp = pltpu.make_async_copy(l_at, g_at, sem) if to_global else pltpu.make_async_copy(g_at, l_at, sem)

            @pl.when(present)
            def _():
                if wait:
                    cp.wait()
                else:
                    cp.start()

            done = done + jnp.where(present, size, 0)


def _sort_kernel(off_s, gs_s, l16_s, h_ref, comb_ref, offcol_ref, xs_in_ref, slots_ref, xs_ref, hs_ref, sem):
    del xs_in_ref
    t = pl.program_id(0)
    slot = t % 2
    other = 1 - slot
    ts = SORT_TILE
    comb = comb_ref[...]
    asg = comb > 0.0
    row = lax.broadcasted_iota(jnp.int32, (ts, ts), 0)
    col = lax.broadcasted_iota(jnp.int32, (ts, ts), 1)
    before = jnp.where(row < col, 1.0, 0.0).astype(BF16)
    rank = jnp.dot(jnp.where(asg, 1.0, 0.0).astype(BF16), before, preferred_element_type=F32)
    pos = offcol_ref[0][:, 0:1] + rank
    p_lo = jnp.min(jnp.where(asg, pos, 1e9), axis=0, keepdims=True)
    p_hi = jnp.max(jnp.where(asg, pos, -1.0), axis=0, keepdims=True)
    w_lo = jnp.sum(jnp.where(jnp.logical_and(asg, pos == p_lo), comb, 0.0), axis=0, keepdims=True)
    w_hi = jnp.sum(jnp.where(jnp.logical_and(asg, pos == p_hi), comb, 0.0), axis=0, keepdims=True)
    w_hi = jnp.where(p_hi > p_lo, w_hi, 0.0)
    dest = lax.broadcasted_iota(jnp.int32, (SORT_CAP, ts), 0).astype(F32)
    onehot = jnp.where(jnp.logical_or(dest == p_lo, dest == p_hi), 1.0, 0.0).astype(BF16)
    hs_ref[slot] = jnp.dot(onehot, h_ref[...], preferred_element_type=F32).astype(BF16)
    slot_rows = jnp.concatenate([p_lo, p_hi, w_lo, w_hi, jnp.zeros((124, ts), F32)], axis=0)
    slots_ref[...] = slot_rows.T
    _segment_copies(t, off_s, gs_s, l16_s, hs_ref.at[slot], xs_ref, sem.at[slot], to_global=True, wait=False)

    @pl.when(t > 0)
    def _():
        _segment_copies(t - 1, off_s, gs_s, l16_s, hs_ref.at[other], xs_ref, sem.at[other], to_global=True, wait=True)

    @pl.when(t == pl.num_programs(0) - 1)
    def _():
        _segment_copies(t, off_s, gs_s, l16_s, hs_ref.at[slot], xs_ref, sem.at[slot], to_global=True, wait=True)


def _sort_call(h2, comb_t, tables, offcol, n_rows_sorted):
    rows, d = h2.shape
    nt = rows // SORT_TILE
    xs0 = jnp.zeros((n_rows_sorted, d), BF16)
    grid_spec = pltpu.PrefetchScalarGridSpec(
        num_scalar_prefetch=3,
        grid=(nt,),
        in_specs=[
            pl.BlockSpec((SORT_TILE, d), lambda i, *_: (i, 0)),
            pl.BlockSpec((N_EXPERTS, SORT_TILE), lambda i, *_: (0, i)),
            pl.BlockSpec((1, N_EXPERTS, 128), lambda i, *_: (i, 0, 0)),
            pl.BlockSpec(memory_space=pl.ANY),
        ],
        out_specs=[
            pl.BlockSpec((SORT_TILE, 128), lambda i, *_: (i, 0)),
            pl.BlockSpec(memory_space=pl.ANY),
        ],
        scratch_shapes=[pltpu.VMEM((2, SORT_CAP, d), BF16), pltpu.SemaphoreType.DMA((2,))],
    )
    slots, xs = pl.pallas_call(
        _sort_kernel,
        grid_spec=grid_spec,
        out_shape=[jax.ShapeDtypeStruct((rows, 128), F32), jax.ShapeDtypeStruct((n_rows_sorted, d), BF16)],
        input_output_aliases={6: 1},
        compiler_params=_cparams(("arbitrary",)),
        name="moe_sort",
    )(*tables, h2, comb_t, offcol, xs0)
    return slots, xs


def _experts_kernel(be_s, bv_s, x_ref, w1_ref, w3_ref, w2_ref, y_ref):
    b = pl.program_id(0)

    @pl.when(bv_s[b] != 0)
    def _():
        x = x_ref[...]
        a = jnp.dot(x, w1_ref[0, 0].astype(BF16), preferred_element_type=F32)
        g = jnp.dot(x, w3_ref[0, 0].astype(BF16), preferred_element_type=F32)
        hid = (_silu(a) * g).astype(BF16)
        y_ref[...] = jnp.dot(hid, w2_ref[0, 0].astype(BF16), preferred_element_type=F32).astype(BF16)

    @pl.when(bv_s[b] == 0)
    def _():
        y_ref[...] = jnp.zeros_like(y_ref)


def _experts_call(xs, blk_expert, blk_valid, w1, w3, w2, layer):
    rows, d = xs.shape
    nb = rows // MOE_BLOCK
    grid_spec = pltpu.PrefetchScalarGridSpec(
        num_scalar_prefetch=2,
        grid=(nb,),
        in_specs=[
            pl.BlockSpec((MOE_BLOCK, d), lambda b, be, bv: (b, 0)),
            pl.BlockSpec((1, 1, d, D_EXPERT), lambda b, be, bv: (layer, be[b], 0, 0)),
            pl.BlockSpec((1, 1, d, D_EXPERT), lambda b, be, bv: (layer, be[b], 0, 0)),
            pl.BlockSpec((1, 1, D_EXPERT, d), lambda b, be, bv: (layer, be[b], 0, 0)),
        ],
        out_specs=pl.BlockSpec((MOE_BLOCK, d), lambda b, be, bv: (b, 0)),
    )
    return pl.pallas_call(
        _experts_kernel,
        grid_spec=grid_spec,
        out_shape=jax.ShapeDtypeStruct((rows, d), BF16),
        compiler_params=_cparams(("arbitrary",)),
        name="moe_experts",
    )(blk_expert, blk_valid, xs, w1, w3, w2)


def _unsort_kernel(final, off_s, gs_s, l16_s, ys_ref, slots_ref, x_ref, fg_ref, g2a_ref, g2b_ref,
                   o_ref, yt_ref, sem):
    t = pl.program_id(0)
    ts = SORT_TILE
    slot = t % 2
    other = 1 - slot

    def fetch(tile, buf):
        yt_ref[buf] = jnp.zeros(yt_ref.shape[1:], yt_ref.dtype)
        _segment_copies(tile, off_s, gs_s, l16_s, yt_ref.at[buf], ys_ref, sem.at[buf], to_global=False, wait=False)

    @pl.when(t == 0)
    def _():
        fetch(t, slot)

    @pl.when(t + 1 < pl.num_programs(0))
    def _():
        fetch(t + 1, other)

    _segment_copies(t, off_s, gs_s, l16_s, yt_ref.at[slot], ys_ref, sem.at[slot], to_global=False, wait=True)
    slots = slots_ref[...]
    src = lax.broadcasted_iota(jnp.int32, (ts, SORT_CAP), 1).astype(F32)
    weights = jnp.where(src == slots[:, 0:1], slots[:, 2:3], 0.0) + jnp.where(src == slots[:, 1:2], slots[:, 3:4], 0.0)
    y = jnp.dot(weights.astype(BF16), yt_ref[slot], preferred_element_type=F32)
    for s, g2_ref in enumerate((g2a_ref, g2b_ref)):
        rws = slice(s * ROW_TILE, (s + 1) * ROW_TILE)
        xn = x_ref[rws, :] + g2_ref[0] * y[rws, :]
        if final:
            ms = jnp.mean(xn * xn, axis=-1, keepdims=True)
            xn = xn * lax.rsqrt(ms + EPS) * fg_ref[...]
        o_ref[rws, :] = xn


def _unsort_call(ys, slots, x, tables, final_g, mods, layer, mod_row, tile_off, final):
    rows, d = x.shape
    nt = rows // SORT_TILE
    sub = SORT_TILE // ROW_TILE
    g2_idx = (layer * 6 + 5) * COND_ROWS
    row_spec = pl.BlockSpec((SORT_TILE, d), lambda i, *_: (i, 0))

    def mod(s):
        return pl.BlockSpec((1, 1, d), lambda i, *_: (g2_idx + mod_row(i * sub + s + tile_off), 0, 0))

    grid_spec = pltpu.PrefetchScalarGridSpec(
        num_scalar_prefetch=3,
        grid=(nt,),
        in_specs=[
            pl.BlockSpec(memory_space=pl.ANY),
            pl.BlockSpec((SORT_TILE, 128), lambda i, *_: (i, 0)),
            row_spec,
            pl.BlockSpec((1, d), lambda i, *_: (0, 0)),
            mod(0), mod(1),
        ],
        out_specs=row_spec,
        scratch_shapes=[pltpu.VMEM((2, SORT_CAP, d), BF16), pltpu.SemaphoreType.DMA((2,))],
    )
    return pl.pallas_call(
        functools.partial(_unsort_kernel, final),
        grid_spec=grid_spec,
        out_shape=jax.ShapeDtypeStruct((rows, d), F32),
        compiler_params=_cparams(("arbitrary",)),
        name="moe_unsort_residual",
    )(*tables, ys, slots, x, final_g, mods, mods)


def _sum_before(a, axis):
    i = jnp.arange(a.shape[axis])
    mask = (i[None, :] < i[:, None]).astype(a.dtype)
    moved = jnp.moveaxis(a, axis, -1)
    return jnp.moveaxis(jnp.sum(moved[..., None, :] * mask, axis=-1), -1, axis)


def _moe_tables(cnt_blocks, n_blocks):
    cnt = cnt_blocks[:, :, 0].astype(jnp.int32)
    seg = (cnt + SEG_ALIGN - 1) // SEG_ALIGN * SEG_ALIGN
    off = _sum_before(seg, 1)
    blocks_e = (seg.sum(axis=0) + MOE_BLOCK - 1) // MOE_BLOCK
    first_block = _sum_before(blocks_e, 0)
    gstart = first_block[None, :] * MOE_BLOCK + _sum_before(seg, 0)
    blk = jnp.arange(n_blocks, dtype=jnp.int32)
    last_block = first_block + blocks_e
    blk_expert = jnp.minimum(jnp.sum((blk[:, None] >= last_block[None, :]).astype(jnp.int32), axis=1),
                             N_EXPERTS - 1)
    blk_valid = (blk < blocks_e.sum()).astype(jnp.int32)
    tables = (off.reshape(-1).astype(jnp.int32), gstart.reshape(-1).astype(jnp.int32),
              (seg // SEG_ALIGN).reshape(-1).astype(jnp.int32))
    offcol = jnp.broadcast_to(off.astype(F32)[:, :, None], off.shape + (128,))
    return tables, offcol, blk_expert.astype(jnp.int32), blk_valid


def _moe_call(h2, comb_t, cnt_blocks, x, w1, w3, w2, final_g, mods, layer, mod_row, tile_off, final):
    rows = x.shape[0]
    n_tiles = rows // SORT_TILE
    max_rows = 2 * rows + n_tiles * N_EXPERTS * (SEG_ALIGN - 1)
    n_blocks = max_rows // MOE_BLOCK + N_EXPERTS
    tables, offcol, blk_expert, blk_valid = _moe_tables(cnt_blocks, n_blocks)
    slots, xs = _sort_call(h2, comb_t, tables, offcol, n_blocks * MOE_BLOCK)
    ys = _experts_call(xs, blk_expert, blk_valid, w1, w3, w2, layer)
    return _unsort_call(ys, slots, x, tables, final_g, mods, layer, mod_row, tile_off, final)


def kernel(x, c, ctx, c_ctx, w_ada, b_ada, norm1_g, norm2_g, w_in, a_ln_g, a_w_s, a_b_s, b_conv_w, b_A_log, b_dt_bias, b_norm_g, w_proj_a, w_proj_b, w_out, w_router, b_router, w_e1, w_e3, w_e2, final_g):
    batch, seq_len, d = x.shape
    ctx_len = ctx.shape[1]
    n_layers = w_ada.shape[0]
    assert d == D_MODEL and batch + 1 <= COND_ROWS
    assert ctx_len % ROW_TILE == 0 and seq_len % ROW_TILE == 0
    assert (batch * ctx_len) % SORT_TILE == 0 and (batch * seq_len) % SORT_TILE == 0
    assert MIX_TILE == SORT_TILE and (batch * ctx_len) % MIX_TILE == 0 and seq_len % MIX_TILE == 0
    ctx_tiles = ctx_len // ROW_TILE
    lat_tiles = seq_len // ROW_TILE
    n_ctx_tiles = batch * ctx_tiles

    def mod_row(tile):
        return jnp.where(tile < n_ctx_tiles, batch, (tile - n_ctx_tiles) // lat_tiles)

    xs = (ctx.reshape(batch * ctx_len, d), x.reshape(batch * seq_len, d))
    cond = jnp.concatenate([c, c_ctx[None, :], jnp.zeros((COND_ROWS - batch - 1, d), F32)], axis=0)
    mods = _ada_call(cond, w_ada, b_ada).reshape(n_layers * 6 * COND_ROWS, 1, d)

    n_ab = 4 * N_HEADS
    ab_lo = 6 * d
    wr_t = w_router.T
    br = b_router.reshape(N_EXPERTS, 1)
    fg = final_g.reshape(1, d)
    for l in range(n_layers):
        w_bf = w_in[l].astype(BF16)
        w_all = jnp.concatenate([w_bf[:, :ab_lo], w_bf[:, ab_lo + n_ab:], w_bf[:, ab_lo:ab_lo + n_ab],
                                 jnp.zeros((d, 128 - n_ab), BF16)], axis=1)
        conv_w = jnp.pad(b_conv_w[l], ((0, 8 - CONV_TAPS), (0, 0)))
        zeros8 = jnp.zeros((N_HEADS,), F32)
        alog_row = jnp.pad(jnp.concatenate([b_A_log[l, 0], zeros8, b_A_log[l, 1], zeros8]), (0, 128 - n_ab)).reshape(1, 128)
        dtb_row = jnp.pad(jnp.concatenate([b_dt_bias[l, 0], zeros8, b_dt_bias[l, 1], zeros8]), (0, 128 - n_ab)).reshape(1, 128)
        u, v, z, ga, gb, qn, kn, vs, gates = _in_call(
            xs, mods, l, norm1_g[l].reshape(1, d), w_all, conv_w, alog_row, dtb_row, mod_row,
            n_ctx_tiles, ctx_tiles, lat_tiles)
        o2 = _gdn_call(qn, kn, vs, gates, batch, ctx_len, seq_len)

        bs = jnp.repeat(a_b_s[l].T, SGU_CHUNK, axis=1)
        bng = b_norm_g[l].reshape(1, HEAD_DIM)
        last = l == n_layers - 1
        tile_off = n_ctx_tiles if last else 0
        x_new, h2, comb_t, cnt = _mix_call(
            xs, u, v, o2, z, ga, gb, mods, l, norm2_g[l].reshape(1, d), a_ln_g[l].reshape(1, d),
            a_w_s[l].astype(BF16), bs, bng,
            w_proj_a[l].astype(BF16), w_proj_b[l].astype(BF16), w_out[l].astype(BF16), wr_t, br, mod_row,
            batch * ctx_len // MIX_TILE, tile_off * ROW_TILE // MIX_TILE)
        xs = (_moe_call(h2, comb_t, cnt, x_new, w_e1, w_e3, w_e2,
                        fg, mods, l, mod_row, tile_off, final=last),)
    return xs[0].reshape(batch, seq_len, d)
```

```python
import functools

import jax
import jax.numpy as jnp
from jax import lax
from jax.experimental import pallas as pl
from jax.experimental.pallas import tpu as pltpu

F32 = jnp.float32
BF16 = jnp.bfloat16
HIGHEST = lax.Precision.HIGHEST

EPS = 1e-6
D_MODEL = 1024
N_HEADS = 8
HEAD_DIM = 128
SGU_CHUNK = 128
SGU_GROUPS = 8
CONV_TAPS = 5
N_EXPERTS = 16
EXPERTS_PER_GROUP = 4
N_EXPERT_GROUPS = 4
D_EXPERT = 512
ROW_TILE = 256
MIX_TILE = 512
GDN_CHUNK = 128
GDN_BLOCK = 2
SORT_TILE = 512
SEG_ALIGN = 16
SORT_CAP = 1280
MOE_BLOCK = 512
SEG_BITS = (32, 16, 8, 4, 2, 1)
COND_ROWS = 8
HALO = 8
VMEM_LIMIT = 56 * 1024 * 1024


def _cparams(sem):
    return pltpu.CompilerParams(dimension_semantics=sem, vmem_limit_bytes=VMEM_LIMIT)


def _sigmoid(x):
    return 0.5 * jnp.tanh(0.5 * x) + 0.5


def _silu(x):
    return x * _sigmoid(x)


def _gelu_tanh(x):
    return 0.5 * x * (1.0 + jnp.tanh(0.7978845608028654 * (x + 0.044715 * (x * x * x))))


def _softplus(x):
    return jnp.maximum(x, 0.0) + jnp.log1p(jnp.exp(-jnp.abs(x)))


def _ada_kernel(cond_ref, w_ref, b_ref, o_ref):
    s = _silu(cond_ref[...])
    o_ref[0, 0] = jnp.dot(s, w_ref[0], precision=HIGHEST, preferred_element_type=F32) + b_ref[0, 0]


def _ada_call(cond, w_ada, b_ada):
    n_layers = w_ada.shape[0]
    d = D_MODEL
    return pl.pallas_call(
        _ada_kernel,
        grid=(n_layers, 6),
        in_specs=[
            pl.BlockSpec((COND_ROWS, d), lambda l, j: (0, 0)),
            pl.BlockSpec((1, d, d), lambda l, j: (l, 0, j)),
            pl.BlockSpec((1, 1, 1, d), lambda l, j: (l, j, 0, 0)),
        ],
        out_specs=pl.BlockSpec((1, 1, COND_ROWS, d), lambda l, j: (l, j, 0, 0)),
        out_shape=jax.ShapeDtypeStruct((n_layers, 6, COND_ROWS, d), F32),
        compiler_params=_cparams(("arbitrary", "arbitrary")),
        name="ada_params",
    )(cond, w_ada, b_ada.reshape(n_layers, 6, 1, d))


def _load_rows(x_refs, tile, n_ctx_tiles):
    if len(x_refs) == 1:
        return x_refs[0][...]
    return jnp.where(tile < n_ctx_tiles, x_refs[0][...], x_refs[1][...])


def _row_specs(xs, n_ctx_tiles, tile_off, d, tile=ROW_TILE):
    if len(xs) == 1:
        return [pl.BlockSpec((tile, d), lambda i: (i + tile_off, 0))]
    return [pl.BlockSpec((tile, d), lambda i: (jnp.minimum(i + tile_off, n_ctx_tiles - 1), 0)),
            pl.BlockSpec((tile, d), lambda i: (jnp.maximum(i + tile_off - n_ctx_tiles, 0), 0))]


def _halo_specs(xs, n_ctx_tiles, d, nxt):
    per = ROW_TILE // HALO

    def spec(n_blocks, first_tile):
        if nxt:
            return pl.BlockSpec((HALO, d), lambda i: (jnp.clip((i - first_tile + 1) * per, 0, n_blocks - 1), 0))
        return pl.BlockSpec((HALO, d), lambda i: (jnp.clip((i - first_tile) * per - 1, 0, n_blocks - 1), 0))

    if len(xs) == 1:
        return [spec(xs[0].shape[0] // HALO, 0)]
    return [spec(xs[0].shape[0] // HALO, 0), spec(xs[1].shape[0] // HALO, n_ctx_tiles)]


def _in_kernel(n_src, n_ctx_tiles, ctx_tiles, lat_tiles, *refs):
    x_refs, xp_refs, xn_refs = refs[:n_src], refs[n_src:2 * n_src], refs[2 * n_src:3 * n_src]
    (sh_ref, sc_ref, g_ref, w_ref, cw_ref, alog_ref, dtb_ref,
     u_ref, v_ref, z_ref, ga_ref, gb_ref, qo_ref, ko_ref, vo_ref, gate_ref, ext_ref) = refs[3 * n_src:]
    i = pl.program_id(0)
    d = D_MODEL
    j = jnp.where(i < n_ctx_tiles, i % ctx_tiles, (i - n_ctx_tiles) % lat_tiles)
    last = jnp.where(i < n_ctx_tiles, ctx_tiles - 1, lat_tiles - 1)
    has_prev = (j != 0).astype(F32)
    has_next = (j != last).astype(F32)

    x = jnp.concatenate([_load_rows(xp_refs, i, n_ctx_tiles), _load_rows(x_refs, i, n_ctx_tiles),
                         _load_rows(xn_refs, i, n_ctx_tiles)], axis=0)
    ms = jnp.mean(x * x, axis=-1, keepdims=True)
    h = x * lax.rsqrt(ms + EPS) * g_ref[...]
    h = h * (1.0 + sc_ref[0]) + sh_ref[0]
    hb_ext = h.astype(BF16)
    hb = h[HALO:HALO + ROW_TILE].astype(BF16)
    def project(n, o_ref):
        o_ref[...] = jnp.dot(hb, w_ref[:, n * d:(n + 1) * d], preferred_element_type=F32).astype(BF16)

    pad = CONV_TAPS // 2
    lo, hi = HALO, HALO + ROW_TILE
    for t in range(3):
        proj = jnp.dot(hb_ext, w_ref[:, (2 + t) * d:(3 + t) * d], preferred_element_type=F32)
        ext_ref[t, 0:lo, :] = proj[0:lo] * has_prev
        ext_ref[t, lo:hi, :] = proj[lo:hi]
        ext_ref[t, hi:hi + HALO, :] = proj[hi:hi + HALO] * has_next
    others = (((0, u_ref), (1, v_ref)), ((5, z_ref), (6, ga_ref)), ((7, gb_ref),))
    for t, (o_ref, l2, scale) in enumerate(((qo_ref, True, HEAD_DIM ** -0.5), (ko_ref, True, 1.0),
                                            (vo_ref, False, 1.0))):
        for n, p_ref in others[t]:
            project(n, p_ref)
        for hd in range(N_HEADS):
            lanes = slice(hd * HEAD_DIM, (hd + 1) * HEAD_DIM)
            acc = None
            for tap in range(CONV_TAPS):
                w_row = cw_ref[tap:tap + 1, t * d + hd * HEAD_DIM:t * d + (hd + 1) * HEAD_DIM]
                start = lo - pad + tap
                term = ext_ref[t, start:start + ROW_TILE, lanes] * w_row
                acc = term if acc is None else acc + term
            y = _silu(acc)
            if l2:
                ss = jnp.sum(y * y, axis=-1, keepdims=True)
                y = y * (lax.rsqrt(ss + EPS) * scale)
            o_ref[:, lanes] = y.astype(BF16)
    ab = jnp.dot(hb, w_ref[:, 8 * d:8 * d + 128], preferred_element_type=F32)
    lane = lax.broadcasted_iota(jnp.int32, ab.shape, 1)
    is_decay = ((lane // N_HEADS) % 2) == 0
    g = -jnp.exp(alog_ref[...]) * _softplus(ab + dtb_ref[...])
    gate = jnp.where(is_decay, g, _sigmoid(ab))
    gate = jnp.where(lane < 4 * N_HEADS, gate, 0.0)
    gate_ref[0] = gate
    gate_ref[1] = pltpu.roll(gate, 128 - 2 * N_HEADS, axis=1)


def _in_call(xs, mods, layer, norm_g, w_all, conv_w, alog_row, dtb_row, mod_row, n_ctx_tiles, ctx_tiles, lat_tiles):
    d = D_MODEL
    rows = sum(a.shape[0] for a in xs)
    nt = rows // ROW_TILE
    sh_idx = (layer * 6 + 0) * COND_ROWS
    sc_idx = (layer * 6 + 1) * COND_ROWS
    row_spec = pl.BlockSpec((ROW_TILE, d), lambda i: (i, 0))
    small = pl.BlockSpec((1, 128), lambda i: (0, 0))
    return pl.pallas_call(
        functools.partial(_in_kernel, len(xs), n_ctx_tiles, ctx_tiles, lat_tiles),
        grid=(nt,),
        in_specs=(_row_specs(xs, n_ctx_tiles, 0, d) + _halo_specs(xs, n_ctx_tiles, d, False)
                  + _halo_specs(xs, n_ctx_tiles, d, True) + [
            pl.BlockSpec((1, 1, d), lambda i: (sh_idx + mod_row(i), 0, 0)),
            pl.BlockSpec((1, 1, d), lambda i: (sc_idx + mod_row(i), 0, 0)),
            pl.BlockSpec((1, d), lambda i: (0, 0)),
            pl.BlockSpec((d, 8 * d + 128), lambda i: (0, 0), pipeline_mode=pl.Buffered(1)),
            pl.BlockSpec((8, 3 * d), lambda i: (0, 0)),
            small, small,
        ]),
        out_specs=[row_spec] * 8 + [pl.BlockSpec((2, ROW_TILE, 128), lambda i: (0, i, 0))],
        out_shape=[jax.ShapeDtypeStruct((rows, d), BF16)] * 8 + [jax.ShapeDtypeStruct((2, rows, 128), F32)],
        scratch_shapes=[pltpu.VMEM((3, ROW_TILE + 2 * HALO, d), F32)],
        compiler_params=_cparams(("arbitrary",)),
        name="norm_in_proj_conv",
    )(*xs, *xs, *xs, mods, mods, norm_g, w_all, conv_w, alog_row, dtb_row)


def _gdn_kernel(qf_ref, kf_ref, vf_ref, gf_ref, qb_ref, kb_ref, vb_ref, gb_ref, of_ref, ob_ref, s_ref):
    @pl.when(pl.program_id(1) == 0)
    def _():
        s_ref[...] = jnp.zeros_like(s_ref)

    for n in range(GDN_BLOCK):
        rows_f = slice(n * GDN_CHUNK, (n + 1) * GDN_CHUNK)
        rows_b = slice((GDN_BLOCK - 1 - n) * GDN_CHUNK, (GDN_BLOCK - n) * GDN_CHUNK)
        _gdn_chunk(((True, rows_f, qf_ref, kf_ref, vf_ref, gf_ref, of_ref),
                    (False, rows_b, qb_ref, kb_ref, vb_ref, gb_ref, ob_ref)), s_ref)


def _gdn_chunk(scans, s_ref):
    c = GDN_CHUNK
    row = lax.broadcasted_iota(jnp.int32, (c, c), 0)
    col = lax.broadcasted_iota(jnp.int32, (c, c), 1)
    eye = (row == col).astype(F32)
    nt_dims = (((1,), (1,)), ((), ()))
    lanes = [slice(h * HEAD_DIM, (h + 1) * HEAD_DIM) for h in range(N_HEADS)]

    q, k, v, gc_col, gc_row, gtot_row, beta_row, incl, strict, dst = [], [], [], [], [], [], [], [], [], []
    for d_idx, (fwd, rows, q_ref, k_ref, v_ref, gate_ref, o_ref) in enumerate(scans):
        incl_d = (row >= col) if fwd else (row <= col)
        strict_d = (row > col) if fwd else (row < col)
        gate = gate_ref[0, rows, :]
        gc_all = jnp.dot(incl_d.astype(F32), gate, precision=HIGHEST, preferred_element_type=F32)
        gtot_all = jnp.dot(jnp.ones((c, c), F32), gate, precision=HIGHEST, preferred_element_type=F32)
        gate_t = gate.T
        gc_t = gc_all.T
        gtot_t = gtot_all.T
        for h in range(N_HEADS):
            q.append(q_ref[rows, lanes[h]])
            k.append(k_ref[rows, lanes[h]])
            v.append(v_ref[rows, lanes[h]])
            gc_col.append(jnp.broadcast_to(gc_all[:, h:h + 1], (c, c)))
            gc_row.append(gc_t[h:h + 1, :])
            gtot_row.append(gtot_t[h:h + 1, :])
            beta_row.append(gate_t[N_HEADS + h:N_HEADS + h + 1, :])
            incl.append(incl_d)
            strict.append(strict_d)
            dst.append((o_ref, rows, lanes[h], d_idx, h))
    chains = range(len(q))

    decay = [jnp.exp(jnp.where(incl[i], gc_col[i] - gc_row[i], -jnp.inf)) * beta_row[i] for i in chains]
    kq = [lax.dot_general(jnp.concatenate([k[i], q[i]], axis=0), k[i], nt_dims, preferred_element_type=F32)
          for i in chains]
    n_mat = [jnp.where(strict[i], kq[i][:c] * decay[i], 0.0) for i in chains]
    a_mat = [(kq[i][c:] * decay[i]).astype(BF16) for i in chains]
    a_b = [(eye + n_mat[i]).astype(BF16) for i in chains]
    t_inv = [eye - n_mat[i] for i in chains]
    for _ in range(6):
        tb = [t_inv[i].astype(BF16) for i in chains]
        err = [(eye - jnp.dot(a_b[i], tb[i], preferred_element_type=F32)).astype(BF16) for i in chains]
        t_inv = [t_inv[i] + jnp.dot(tb[i], err[i], preferred_element_type=F32) for i in chains]
    u_val = [jnp.dot(t_inv[i].astype(BF16), v[i], preferred_element_type=F32) for i in chains]
    w_key = [jnp.dot((t_inv[i] * jnp.exp(gc_row[i])).astype(BF16), k[i], preferred_element_type=F32)
             for i in chains]

    s_old = [s_ref[dst[i][3], dst[i][4]] for i in chains]
    sb = [s_old[i].astype(BF16) for i in chains]
    wq_s = [jnp.dot(jnp.concatenate([w_key[i].astype(BF16), q[i]], axis=0), sb[i], preferred_element_type=F32)
            for i in chains]
    v_new = [(u_val[i] - wq_s[i][:c]).astype(BF16) for i in chains]
    kd_t = [(k[i].astype(F32).T * (jnp.exp(gtot_row[i] - gc_row[i]) * beta_row[i])).astype(BF16) for i in chains]
    ak_v = [jnp.dot(jnp.concatenate([a_mat[i], kd_t[i]], axis=0), v_new[i], preferred_element_type=F32)
            for i in chains]
    for i in chains:
        o_ref, rows, ln, d_idx, h = dst[i]
        s_ref[d_idx, h] = s_old[i] * jnp.exp(gtot_row[i]) + ak_v[i][c:]
        o_ref[rows, ln] = (jnp.exp(gc_col[i]) * wq_s[i][c:] + ak_v[i][:c]).astype(BF16)


def _gdn_call(q, k, v, gates, batch, ctx_len, seq_len):
    rows, d = q.shape
    c = GDN_BLOCK * GDN_CHUNK
    assert ctx_len % c == 0 and seq_len % c == 0
    n_ctx = ctx_len // c
    n_lat = seq_len // c
    n_steps = n_ctx + n_lat

    def chunk(b, fwd, s):
        pos = s if fwd else jnp.where(s < n_ctx, n_ctx - 1 - s, n_ctx + n_steps - 1 - s)
        return jnp.where(pos < n_ctx, b * n_ctx + pos, batch * n_ctx + b * n_lat + pos - n_ctx)

    def specs(fwd):
        blk = pl.BlockSpec((c, d), lambda b, s: (chunk(b, fwd, s), 0))
        return [blk, blk, blk, pl.BlockSpec((1, c, 128), lambda b, s: (0 if fwd else 1, chunk(b, fwd, s), 0))]

    return pl.pallas_call(
        _gdn_kernel,
        grid=(batch, n_steps),
        in_specs=specs(True) + specs(False),
        out_specs=[pl.BlockSpec((c, d), lambda b, s: (chunk(b, True, s), 0)),
                   pl.BlockSpec((c, d), lambda b, s: (chunk(b, False, s), 0))],
        out_shape=[jax.ShapeDtypeStruct((rows, d), BF16)] * 2,
        scratch_shapes=[pltpu.VMEM((2, N_HEADS, HEAD_DIM, HEAD_DIM), F32)],
        compiler_params=_cparams(("arbitrary", "arbitrary")),
        name="gated_delta",
    )(q, k, v, gates, q, k, v, gates)


def _mix_kernel(n_src, n_ctx_tiles, tile_off, *refs):
    x_refs = refs[:n_src]
    (u_ref, v_ref, of_ref, ob_ref, z_ref, ga_ref, gb_ref,
     g1_ref, sh2_ref, sc2_ref, n2g_ref, lng_ref, ws_ref, bs_ref, bng_ref,
     wpa_ref, wpb_ref, wout_ref, wr_ref, br_ref,
     xo_ref, h2_ref, comb_ref, cnt_ref, sa_ref, sb_ref) = refs[n_src:]
    d = D_MODEL
    ug = _gelu_tanh(u_ref[...].astype(F32))
    vg = _gelu_tanh(v_ref[...].astype(F32))
    mu = jnp.mean(vg, axis=-1, keepdims=True)
    vc = vg - mu
    var = jnp.mean(vc * vc, axis=-1, keepdims=True)
    vn = (vc * lax.rsqrt(var + EPS) * lng_ref[...]).astype(BF16)
    for ch in range(MIX_TILE // SGU_CHUNK):
        rws = slice(ch * SGU_CHUNK, (ch + 1) * SGU_CHUNK)
        for g in range(SGU_GROUPS):
            lanes = slice(g * 128, (g + 1) * 128)
            mixed = jnp.dot(ws_ref[g], vn[rws, lanes], preferred_element_type=F32) + bs_ref[:, lanes]
            sa_ref[rws, lanes] = (ug[rws, lanes] * mixed).astype(BF16)
    y_a = jnp.dot(sa_ref[...], wpa_ref[...], preferred_element_type=F32)
    for h in range(N_HEADS):
        lanes = slice(h * HEAD_DIM, (h + 1) * HEAD_DIM)
        o = of_ref[:, lanes].astype(F32) + ob_ref[:, lanes].astype(F32)
        ms = jnp.mean(o * o, axis=-1, keepdims=True)
        o = o * lax.rsqrt(ms + EPS) * bng_ref[...]
        sb_ref[:, lanes] = (o * _silu(z_ref[:, lanes].astype(F32))).astype(BF16)
    y_b = jnp.dot(sb_ref[...], wpb_ref[...], preferred_element_type=F32)
    merged = _sigmoid(ga_ref[...].astype(F32)) * y_a + _sigmoid(gb_ref[...].astype(F32)) * y_b
    y = jnp.dot(merged.astype(BF16), wout_ref[...], preferred_element_type=F32)
    xn = _load_rows(x_refs, pl.program_id(0) + tile_off, n_ctx_tiles) + g1_ref[0] * y
    xo_ref[...] = xn
    ms = jnp.mean(xn * xn, axis=-1, keepdims=True)
    h2 = xn * lax.rsqrt(ms + EPS) * n2g_ref[...]
    h2 = h2 * (1.0 + sc2_ref[0]) + sh2_ref[0]
    h2_ref[...] = h2.astype(BF16)
    logits = lax.dot_general(wr_ref[...], h2, (((1,), (1,)), ((), ())),
                             precision=HIGHEST, preferred_element_type=F32)
    scores = _sigmoid(logits)
    sel = scores + br_ref[...]
    srow = [sel[e:e + 1, :] for e in range(N_EXPERTS)]
    grp = []
    for g in range(N_EXPERT_GROUPS):
        m = srow[4 * g:4 * g + 4]
        best2 = None
        for a in range(4):
            for b in range(a + 1, 4):
                pair = m[a] + m[b]
                best2 = pair if best2 is None else jnp.maximum(best2, pair)
        grp.append(best2)
    best_val = grp[0]
    best_idx = jnp.zeros_like(best_val, dtype=jnp.int32)
    for g in range(1, N_EXPERT_GROUPS):
        better = grp[g] > best_val
        best_val = jnp.where(better, grp[g], best_val)
        best_idx = jnp.where(better, g, best_idx)
    picked = []
    for e in range(N_EXPERTS):
        g = e // EXPERTS_PER_GROUP
        rank = jnp.zeros_like(best_idx)
        for o_e in range(4 * g, 4 * g + 4):
            if o_e == e:
                continue
            ahead = (srow[o_e] > srow[e]) if o_e > e else (srow[o_e] >= srow[e])
            rank = rank + ahead.astype(jnp.int32)
        chosen = jnp.logical_and(best_idx == g, rank < 2)
        picked.append(jnp.where(chosen, scores[e:e + 1, :], 0.0))
    total = picked[0]
    for e in range(1, N_EXPERTS):
        total = total + picked[e]
    inv = 1.0 / total
    for e in range(N_EXPERTS):
        w_e = picked[e] * inv
        comb_ref[e:e + 1, :] = w_e
        n_e = jnp.sum(jnp.where(w_e > 0.0, 1.0, 0.0), axis=-1, keepdims=True)
        cnt_ref[0, e:e + 1, :] = jnp.broadcast_to(n_e, (1, 128))


def _mix_call(xs, u, v, o_f, o_b, z, ga, gb, mods, layer, norm2_g, ln_g, ws, bs, bng,
              wpa, wpb, wout, wr_t, br, mod_row, n_ctx_tiles, tile_off):
    d = D_MODEL
    nt = sum(a.shape[0] for a in xs) // MIX_TILE - tile_off
    rows = nt * MIX_TILE
    per = MIX_TILE // ROW_TILE
    g1_idx = (layer * 6 + 2) * COND_ROWS
    sh2_idx = (layer * 6 + 3) * COND_ROWS
    sc2_idx = (layer * 6 + 4) * COND_ROWS
    row_spec = pl.BlockSpec((MIX_TILE, d), lambda i: (i + tile_off, 0))
    out_spec = pl.BlockSpec((MIX_TILE, d), lambda i: (i, 0))
    vec = pl.BlockSpec((1, d), lambda i: (0, 0))
    wspec = pl.BlockSpec((d, d), lambda i: (0, 0))

    def mod(idx):
        return pl.BlockSpec((1, 1, d), lambda i: (idx + mod_row((i + tile_off) * per), 0, 0))

    return pl.pallas_call(
        functools.partial(_mix_kernel, len(xs), n_ctx_tiles, tile_off),
        grid=(nt,),
        in_specs=_row_specs(xs, n_ctx_tiles, tile_off, d, MIX_TILE) + [
            row_spec, row_spec,
            row_spec, row_spec,
            row_spec, row_spec, row_spec,
            mod(g1_idx), mod(sh2_idx), mod(sc2_idx),
            vec, vec,
            pl.BlockSpec((SGU_GROUPS, SGU_CHUNK, SGU_CHUNK), lambda i: (0, 0, 0)),
            pl.BlockSpec((SGU_CHUNK, d), lambda i: (0, 0)),
            pl.BlockSpec((1, HEAD_DIM), lambda i: (0, 0)),
            wspec, wspec, wspec,
            pl.BlockSpec((N_EXPERTS, d), lambda i: (0, 0)),
            pl.BlockSpec((N_EXPERTS, 1), lambda i: (0, 0)),
        ],
        out_specs=[out_spec, out_spec, pl.BlockSpec((N_EXPERTS, MIX_TILE), lambda i: (0, i)),
                   pl.BlockSpec((1, N_EXPERTS, 128), lambda i: (i, 0, 0))],
        out_shape=[jax.ShapeDtypeStruct((rows, d), F32), jax.ShapeDtypeStruct((rows, d), BF16),
                   jax.ShapeDtypeStruct((N_EXPERTS, rows), F32),
                   jax.ShapeDtypeStruct((nt, N_EXPERTS, 128), F32)],
        scratch_shapes=[pltpu.VMEM((MIX_TILE, d), BF16), pltpu.VMEM((MIX_TILE, d), BF16)],
        compiler_params=_cparams(("arbitrary",)),
        name="mix_merge_router",
    )(*xs, u, v, o_f, o_b, z, ga, gb, mods, mods, mods, norm2_g, ln_g, ws, bs, bng,
      wpa, wpb, wout, wr_t, br)


def _segment_copies(t, off_s, gs_s, l16_s, local_ref, global_ref, sem, to_global, wait):
    for e in range(N_EXPERTS):
        idx = t * N_EXPERTS + e
        lo = off_s[idx]
        go = gs_s[idx]
        n16 = l16_s[idx]
        done = jnp.int32(0)
        for bit in SEG_BITS:
            size = bit * SEG_ALIGN
            present = (n16 & bit) != 0
            l_at = local_ref.at[pl.ds(pl.multiple_of(lo + done, SEG_ALIGN), size)]
            g_at = global_ref.at[pl.ds(pl.multiple_of(go + done, SEG_ALIGN), size)]
            cp = pltpu.make_async_copy(l_at, g_at, sem) if to_global else pltpu.make_async_copy(g_at, l_at, sem)

            @pl.when(present)
            def _():
                if wait:
                    cp.wait()
                else:
                    cp.start()

            done = done + jnp.where(present, size, 0)


def _sort_kernel(off_s, gs_s, l16_s, h_ref, comb_ref, offcol_ref, xs_in_ref, slots_ref, xs_ref, hs_ref, sem):
    del xs_in_ref
    t = pl.program_id(0)
    slot = t % 2
    other = 1 - slot
    ts = SORT_TILE
    comb = comb_ref[...]
    asg = comb > 0.0
    row = lax.broadcasted_iota(jnp.int32, (ts, ts), 0)
    col = lax.broadcasted_iota(jnp.int32, (ts, ts), 1)
    before = jnp.where(row < col, 1.0, 0.0).astype(BF16)
    rank = jnp.dot(jnp.where(asg, 1.0, 0.0).astype(BF16), before, preferred_element_type=F32)
    pos = offcol_ref[0][:, 0:1] + rank
    p_lo = jnp.min(jnp.where(asg, pos, 1e9), axis=0, keepdims=True)
    p_hi = jnp.max(jnp.where(asg, pos, -1.0), axis=0, keepdims=True)
    w_lo = jnp.sum(jnp.where(jnp.logical_and(asg, pos == p_lo), comb, 0.0), axis=0, keepdims=True)
    w_hi = jnp.sum(jnp.where(jnp.logical_and(asg, pos == p_hi), comb, 0.0), axis=0, keepdims=True)
    w_hi = jnp.where(p_hi > p_lo, w_hi, 0.0)
    dest = lax.broadcasted_iota(jnp.int32, (SORT_CAP, ts), 0).astype(F32)
    onehot = jnp.where(jnp.logical_or(dest == p_lo, dest == p_hi), 1.0, 0.0).astype(BF16)
    hs_ref[slot] = jnp.dot(onehot, h_ref[...], preferred_element_type=F32).astype(BF16)
    slot_rows = jnp.concatenate([p_lo, p_hi, w_lo, w_hi, jnp.zeros((124, ts), F32)], axis=0)
    slots_ref[...] = slot_rows.T
    _segment_copies(t, off_s, gs_s, l16_s, hs_ref.at[slot], xs_ref, sem.at[slot], to_global=True, wait=False)

    @pl.when(t > 0)
    def _():
        _segment_copies(t - 1, off_s, gs_s, l16_s, hs_ref.at[other], xs_ref, sem.at[other], to_global=True, wait=True)

    @pl.when(t == pl.num_programs(0) - 1)
    def _():
        _segment_copies(t, off_s, gs_s, l16_s, hs_ref.at[slot], xs_ref, sem.at[slot], to_global=True, wait=True)


def _sort_call(h2, comb_t, tables, offcol, n_rows_sorted):
    rows, d = h2.shape
    nt = rows // SORT_TILE
    xs0 = jnp.zeros((n_rows_sorted, d), BF16)
    grid_spec = pltpu.PrefetchScalarGridSpec(
        num_scalar_prefetch=3,
        grid=(nt,),
        in_specs=[
            pl.BlockSpec((SORT_TILE, d), lambda i, *_: (i, 0)),
            pl.BlockSpec((N_EXPERTS, SORT_TILE), lambda i, *_: (0, i)),
            pl.BlockSpec((1, N_EXPERTS, 128), lambda i, *_: (i, 0, 0)),
            pl.BlockSpec(memory_space=pl.ANY),
        ],
        out_specs=[
            pl.BlockSpec((SORT_TILE, 128), lambda i, *_: (i, 0)),
            pl.BlockSpec(memory_space=pl.ANY),
        ],
        scratch_shapes=[pltpu.VMEM((2, SORT_CAP, d), BF16), pltpu.SemaphoreType.DMA((2,))],
    )
    slots, xs = pl.pallas_call(
        _sort_kernel,
        grid_spec=grid_spec,
        out_shape=[jax.ShapeDtypeStruct((rows, 128), F32), jax.ShapeDtypeStruct((n_rows_sorted, d), BF16)],
        input_output_aliases={6: 1},
        compiler_params=_cparams(("arbitrary",)),
        name="moe_sort",
    )(*tables, h2, comb_t, offcol, xs0)
    return slots, xs


def _experts_kernel(be_s, bv_s, x_ref, w1_ref, w3_ref, w2_ref, y_ref):
    b = pl.program_id(0)

    @pl.when(bv_s[b] != 0)
    def _():
        x = x_ref[...]
        a = jnp.dot(x, w1_ref[0, 0].astype(BF16), preferred_element_type=F32)
        g = jnp.dot(x, w3_ref[0, 0].astype(BF16), preferred_element_type=F32)
        hid = (_silu(a) * g).astype(BF16)
        y_ref[...] = jnp.dot(hid, w2_ref[0, 0].astype(BF16), preferred_element_type=F32).astype(BF16)

    @pl.when(bv_s[b] == 0)
    def _():
        y_ref[...] = jnp.zeros_like(y_ref)


def _experts_call(xs, blk_expert, blk_valid, w1, w3, w2, layer):
    rows, d = xs.shape
    nb = rows // MOE_BLOCK
    grid_spec = pltpu.PrefetchScalarGridSpec(
        num_scalar_prefetch=2,
        grid=(nb,),
        in_specs=[
            pl.BlockSpec((MOE_BLOCK, d), lambda b, be, bv: (b, 0)),
            pl.BlockSpec((1, 1, d, D_EXPERT), lambda b, be, bv: (layer, be[b], 0, 0)),
            pl.BlockSpec((1, 1, d, D_EXPERT), lambda b, be, bv: (layer, be[b], 0, 0)),
            pl.BlockSpec((1, 1, D_EXPERT, d), lambda b, be, bv: (layer, be[b], 0, 0)),
        ],
        out_specs=pl.BlockSpec((MOE_BLOCK, d), lambda b, be, bv: (b, 0)),
    )
    return pl.pallas_call(
        _experts_kernel,
        grid_spec=grid_spec,
        out_shape=jax.ShapeDtypeStruct((rows, d), BF16),
        compiler_params=_cparams(("arbitrary",)),
        name="moe_experts",
    )(blk_expert, blk_valid, xs, w1, w3, w2)


def _unsort_kernel(final, off_s, gs_s, l16_s, ys_ref, slots_ref, x_ref, fg_ref, g2a_ref, g2b_ref,
                   o_ref, yt_ref, sem):
    t = pl.program_id(0)
    ts = SORT_TILE
    slot = t % 2
    other = 1 - slot

    def fetch(tile, buf):
        yt_ref[buf] = jnp.zeros(yt_ref.shape[1:], yt_ref.dtype)
        _segment_copies(tile, off_s, gs_s, l16_s, yt_ref.at[buf], ys_ref, sem.at[buf], to_global=False, wait=False)

    @pl.when(t == 0)
    def _():
        fetch(t, slot)

    @pl.when(t + 1 < pl.num_programs(0))
    def _():
        fetch(t + 1, other)

    _segment_copies(t, off_s, gs_s, l16_s, yt_ref.at[slot], ys_ref, sem.at[slot], to_global=False, wait=True)
    slots = slots_ref[...]
    src = lax.broadcasted_iota(jnp.int32, (ts, SORT_CAP), 1).astype(F32)
    weights = jnp.where(src == slots[:, 0:1], slots[:, 2:3], 0.0) + jnp.where(src == slots[:, 1:2], slots[:, 3:4], 0.0)
    y = jnp.dot(weights.astype(BF16), yt_ref[slot], preferred_element_type=F32)
    for s, g2_ref in enumerate((g2a_ref, g2b_ref)):
        rws = slice(s * ROW_TILE, (s + 1) * ROW_TILE)
        xn = x_ref[rws, :] + g2_ref[0] * y[rws, :]
        if final:
            ms = jnp.mean(xn * xn, axis=-1, keepdims=True)
            xn = xn * lax.rsqrt(ms + EPS) * fg_ref[...]
        o_ref[rws, :] = xn


def _unsort_call(ys, slots, x, tables, final_g, mods, layer, mod_row, tile_off, final):
    rows, d = x.shape
    nt = rows // SORT_TILE
    sub = SORT_TILE // ROW_TILE
    g2_idx = (layer * 6 + 5) * COND_ROWS
    row_spec = pl.BlockSpec((SORT_TILE, d), lambda i, *_: (i, 0))

    def mod(s):
        return pl.BlockSpec((1, 1, d), lambda i, *_: (g2_idx + mod_row(i * sub + s + tile_off), 0, 0))

    grid_spec = pltpu.PrefetchScalarGridSpec(
        num_scalar_prefetch=3,
        grid=(nt,),
        in_specs=[
            pl.BlockSpec(memory_space=pl.ANY),
            pl.BlockSpec((SORT_TILE, 128), lambda i, *_: (i, 0)),
            row_spec,
            pl.BlockSpec((1, d), lambda i, *_: (0, 0)),
            mod(0), mod(1),
        ],
        out_specs=row_spec,
        scratch_shapes=[pltpu.VMEM((2, SORT_CAP, d), BF16), pltpu.SemaphoreType.DMA((2,))],
    )
    return pl.pallas_call(
        functools.partial(_unsort_kernel, final),
        grid_spec=grid_spec,
        out_shape=jax.ShapeDtypeStruct((rows, d), F32),
        compiler_params=_cparams(("arbitrary",)),
        name="moe_unsort_residual",
    )(*tables, ys, slots, x, final_g, mods, mods)


def _sum_before(a, axis):
    i = jnp.arange(a.shape[axis])
    mask = (i[None, :] < i[:, None]).astype(a.dtype)
    moved = jnp.moveaxis(a, axis, -1)
    return jnp.moveaxis(jnp.sum(moved[..., None, :] * mask, axis=-1), -1, axis)


def _moe_tables(cnt_blocks, n_blocks):
    cnt = cnt_blocks[:, :, 0].astype(jnp.int32)
    seg = (cnt + SEG_ALIGN - 1) // SEG_ALIGN * SEG_ALIGN
    off = _sum_before(seg, 1)
    blocks_e = (seg.sum(axis=0) + MOE_BLOCK - 1) // MOE_BLOCK
    first_block = _sum_before(blocks_e, 0)
    gstart = first_block[None, :] * MOE_BLOCK + _sum_before(seg, 0)
    blk = jnp.arange(n_blocks, dtype=jnp.int32)
    last_block = first_block + blocks_e
    blk_expert = jnp.minimum(jnp.sum((blk[:, None] >= last_block[None, :]).astype(jnp.int32), axis=1),
                             N_EXPERTS - 1)
    blk_valid = (blk < blocks_e.sum()).astype(jnp.int32)
    tables = (off.reshape(-1).astype(jnp.int32), gstart.reshape(-1).astype(jnp.int32),
              (seg // SEG_ALIGN).reshape(-1).astype(jnp.int32))
    offcol = jnp.broadcast_to(off.astype(F32)[:, :, None], off.shape + (128,))
    return tables, offcol, blk_expert.astype(jnp.int32), blk_valid


def _moe_call(h2, comb_t, cnt_blocks, x, w1, w3, w2, final_g, mods, layer, mod_row, tile_off, final):
    rows = x.shape[0]
    n_tiles = rows // SORT_TILE
    max_rows = 2 * rows + n_tiles * N_EXPERTS * (SEG_ALIGN - 1)
    n_blocks = max_rows // MOE_BLOCK + N_EXPERTS
    tables, offcol, blk_expert, blk_valid = _moe_tables(cnt_blocks, n_blocks)
    slots, xs = _sort_call(h2, comb_t, tables, offcol, n_blocks * MOE_BLOCK)
    ys = _experts_call(xs, blk_expert, blk_valid, w1, w3, w2, layer)
    return _unsort_call(ys, slots, x, tables, final_g, mods, layer, mod_row, tile_off, final)


def kernel(x, c, ctx, c_ctx, w_ada, b_ada, norm1_g, norm2_g, w_in, a_ln_g, a_w_s, a_b_s, b_conv_w, b_A_log, b_dt_bias, b_norm_g, w_proj_a, w_proj_b, w_out, w_router, b_router, w_e1, w_e3, w_e2, final_g):
    batch, seq_len, d = x.shape
    ctx_len = ctx.shape[1]
    n_layers = w_ada.shape[0]
    assert d == D_MODEL and batch + 1 <= COND_ROWS
    assert ctx_len % ROW_TILE == 0 and seq_len % ROW_TILE == 0
    assert (batch * ctx_len) % SORT_TILE == 0 and (batch * seq_len) % SORT_TILE == 0
    assert MIX_TILE == SORT_TILE and (batch * ctx_len) % MIX_TILE == 0 and seq_len % MIX_TILE == 0
    ctx_tiles = ctx_len // ROW_TILE
    lat_tiles = seq_len // ROW_TILE
    n_ctx_tiles = batch * ctx_tiles

    def mod_row(tile):
        return jnp.where(tile < n_ctx_tiles, batch, (tile - n_ctx_tiles) // lat_tiles)

    xs = (ctx.reshape(batch * ctx_len, d), x.reshape(batch * seq_len, d))
    cond = jnp.concatenate([c, c_ctx[None, :], jnp.zeros((COND_ROWS - batch - 1, d), F32)], axis=0)
    mods = _ada_call(cond, w_ada, b_ada).reshape(n_layers * 6 * COND_ROWS, 1, d)

    n_ab = 4 * N_HEADS
    ab_lo = 6 * d
    wr_t = w_router.T
    br = b_router.reshape(N_EXPERTS, 1)
    fg = final_g.reshape(1, d)
    for l in range(n_layers):
        w_bf = w_in[l].astype(BF16)
        w_all = jnp.concatenate([w_bf[:, :ab_lo], w_bf[:, ab_lo + n_ab:], w_bf[:, ab_lo:ab_lo + n_ab],
                                 jnp.zeros((d, 128 - n_ab), BF16)], axis=1)
        conv_w = jnp.pad(b_conv_w[l], ((0, 8 - CONV_TAPS), (0, 0)))
        zeros8 = jnp.zeros((N_HEADS,), F32)
        alog_row = jnp.pad(jnp.concatenate([b_A_log[l, 0], zeros8, b_A_log[l, 1], zeros8]), (0, 128 - n_ab)).reshape(1, 128)
        dtb_row = jnp.pad(jnp.concatenate([b_dt_bias[l, 0], zeros8, b_dt_bias[l, 1], zeros8]), (0, 128 - n_ab)).reshape(1, 128)
        u, v, z, ga, gb, qn, kn, vs, gates = _in_call(
            xs, mods, l, norm1_g[l].reshape(1, d), w_all, conv_w, alog_row, dtb_row, mod_row,
            n_ctx_tiles, ctx_tiles, lat_tiles)
        o_f, o_b = _gdn_call(qn, kn, vs, gates, batch, ctx_len, seq_len)

        bs = jnp.repeat(a_b_s[l].T, SGU_CHUNK, axis=1)
        bng = b_norm_g[l].reshape(1, HEAD_DIM)
        last = l == n_layers - 1
        tile_off = n_ctx_tiles if last else 0
        x_new, h2, comb_t, cnt = _mix_call(
            xs, u, v, o_f, o_b, z, ga, gb, mods, l, norm2_g[l].reshape(1, d), a_ln_g[l].reshape(1, d),
            a_w_s[l].astype(BF16), bs, bng,
            w_proj_a[l].astype(BF16), w_proj_b[l].astype(BF16), w_out[l].astype(BF16), wr_t, br, mod_row,
            batch * ctx_len // MIX_TILE, tile_off * ROW_TILE // MIX_TILE)
        xs = (_moe_call(h2, comb_t, cnt, x_new, w_e1, w_e3, w_e2,
                        fg, mods, l, mod_row, tile_off, final=last),)
    return xs[0].reshape(batch, seq_len, d)
```

```python
import functools

import jax
import jax.numpy as jnp
from jax import lax
from jax.experimental import pallas as pl
from jax.experimental.pallas import tpu as pltpu

F32 = jnp.float32
BF16 = jnp.bfloat16
HIGHEST = lax.Precision.HIGHEST

EPS = 1e-6
D_MODEL = 1024
N_HEADS = 8
HEAD_DIM = 128
SGU_CHUNK = 128
SGU_GROUPS = 8
CONV_TAPS = 5
N_EXPERTS = 16
EXPERTS_PER_GROUP = 4
N_EXPERT_GROUPS = 4
D_EXPERT = 512
ROW_TILE = 256
MIX_TILE = 512
GDN_CHUNK = 128
GDN_BLOCK = 2
SORT_TILE = 512
SEG_ALIGN = 16
SORT_CAP = 1280
MOE_BLOCK = 512
SEG_BITS = (32, 16, 8, 4, 2, 1)
COND_ROWS = 8
HALO = 8
VMEM_LIMIT = 56 * 1024 * 1024


def _cparams(sem):
    return pltpu.CompilerParams(dimension_semantics=sem, vmem_limit_bytes=VMEM_LIMIT)


def _sigmoid(x):
    return 0.5 * jnp.tanh(0.5 * x) + 0.5


def _silu(x):
    return x * _sigmoid(x)


def _gelu_tanh(x):
    return 0.5 * x * (1.0 + jnp.tanh(0.7978845608028654 * (x + 0.044715 * (x * x * x))))


def _softplus(x):
    return jnp.maximum(x, 0.0) + jnp.log1p(jnp.exp(-jnp.abs(x)))


def _ada_kernel(cond_ref, w_ref, b_ref, o_ref):
    s = _silu(cond_ref[...])
    o_ref[0, 0] = jnp.dot(s, w_ref[0], precision=HIGHEST, preferred_element_type=F32) + b_ref[0, 0]


def _ada_call(cond, w_ada, b_ada):
    n_layers = w_ada.shape[0]
    d = D_MODEL
    return pl.pallas_call(
        _ada_kernel,
        grid=(n_layers, 6),
        in_specs=[
            pl.BlockSpec((COND_ROWS, d), lambda l, j: (0, 0)),
            pl.BlockSpec((1, d, d), lambda l, j: (l, 0, j)),
            pl.BlockSpec((1, 1, 1, d), lambda l, j: (l, j, 0, 0)),
        ],
        out_specs=pl.BlockSpec((1, 1, COND_ROWS, d), lambda l, j: (l, j, 0, 0)),
        out_shape=jax.ShapeDtypeStruct((n_layers, 6, COND_ROWS, d), F32),
        compiler_params=_cparams(("arbitrary", "arbitrary")),
        name="ada_params",
    )(cond, w_ada, b_ada.reshape(n_layers, 6, 1, d))


def _load_rows(x_refs, tile, n_ctx_tiles):
    if len(x_refs) == 1:
        return x_refs[0][...]
    return jnp.where(tile < n_ctx_tiles, x_refs[0][...], x_refs[1][...])


def _row_specs(xs, n_ctx_tiles, tile_off, d, tile=ROW_TILE):
    if len(xs) == 1:
        return [pl.BlockSpec((tile, d), lambda i: (i + tile_off, 0))]
    return [pl.BlockSpec((tile, d), lambda i: (jnp.minimum(i + tile_off, n_ctx_tiles - 1), 0)),
            pl.BlockSpec((tile, d), lambda i: (jnp.maximum(i + tile_off - n_ctx_tiles, 0), 0))]


def _halo_specs(xs, n_ctx_tiles, d, nxt):
    per = ROW_TILE // HALO

    def spec(n_blocks, first_tile):
        if nxt:
            return pl.BlockSpec((HALO, d), lambda i: (jnp.clip((i - first_tile + 1) * per, 0, n_blocks - 1), 0))
        return pl.BlockSpec((HALO, d), lambda i: (jnp.clip((i - first_tile) * per - 1, 0, n_blocks - 1), 0))

    if len(xs) == 1:
        return [spec(xs[0].shape[0] // HALO, 0)]
    return [spec(xs[0].shape[0] // HALO, 0), spec(xs[1].shape[0] // HALO, n_ctx_tiles)]


def _in_kernel(n_src, n_ctx_tiles, ctx_tiles, lat_tiles, *refs):
    x_refs, xp_refs, xn_refs = refs[:n_src], refs[n_src:2 * n_src], refs[2 * n_src:3 * n_src]
    (sh_ref, sc_ref, g_ref, w_ref, cw_ref, alog_ref, dtb_ref,
     u_ref, v_ref, z_ref, ga_ref, gb_ref, qo_ref, ko_ref, vo_ref, gate_ref, ext_ref) = refs[3 * n_src:]
    i = pl.program_id(0)
    d = D_MODEL
    j = jnp.where(i < n_ctx_tiles, i % ctx_tiles, (i - n_ctx_tiles) % lat_tiles)
    last = jnp.where(i < n_ctx_tiles, ctx_tiles - 1, lat_tiles - 1)
    has_prev = (j != 0).astype(F32)
    has_next = (j != last).astype(F32)

    x = jnp.concatenate([_load_rows(xp_refs, i, n_ctx_tiles), _load_rows(x_refs, i, n_ctx_tiles),
                         _load_rows(xn_refs, i, n_ctx_tiles)], axis=0)
    ms = jnp.mean(x * x, axis=-1, keepdims=True)
    h = x * lax.rsqrt(ms + EPS) * g_ref[...]
    h = h * (1.0 + sc_ref[0]) + sh_ref[0]
    hb_ext = h.astype(BF16)
    hb = h[HALO:HALO + ROW_TILE].astype(BF16)
    pad = CONV_TAPS // 2
    lo, hi = HALO, HALO + ROW_TILE
    step = 2 * HEAD_DIM

    def project_qkv(t, c0):
        cols = slice((2 + t) * d + c0, (2 + t) * d + c0 + step)
        proj = jnp.dot(hb_ext, w_ref[:, cols], preferred_element_type=F32)
        ext_ref[t, 0:lo, c0:c0 + step] = proj[0:lo] * has_prev
        ext_ref[t, lo:hi, c0:c0 + step] = proj[lo:hi]
        ext_ref[t, hi:hi + HALO, c0:c0 + step] = proj[hi:hi + HALO] * has_next
        return proj[hi + HALO - 1:hi + HALO, 0:HEAD_DIM]

    def conv_head(t, hd, o_ref, l2, scale, anchor):
        lanes = slice(hd * HEAD_DIM, (hd + 1) * HEAD_DIM)
        acc = None
        for tap in range(CONV_TAPS):
            w_row = cw_ref[tap:tap + 1, t * d + hd * HEAD_DIM:t * d + (hd + 1) * HEAD_DIM]
            if tap == 0 and anchor is not None:
                w_row = w_row + jnp.where(i < 0, anchor, 0.0)
            start = lo - pad + tap
            term = ext_ref[t, start:start + ROW_TILE, lanes] * w_row
            acc = term if acc is None else acc + term
        y = _silu(acc)
        if l2:
            ss = jnp.sum(y * y, axis=-1, keepdims=True)
            y = y * (lax.rsqrt(ss + EPS) * scale)
        o_ref[:, lanes] = y.astype(BF16)

    passes = [functools.partial(project_qkv, t, c0) for t in range(3) for c0 in range(0, d, step)]

    def project_other(n, o_ref, c0):
        res = jnp.dot(hb, w_ref[:, n * d + c0:n * d + c0 + step], preferred_element_type=F32)
        o_ref[:, c0:c0 + step] = res.astype(BF16)
        return res[ROW_TILE - 1:ROW_TILE, 0:HEAD_DIM]

    passes += [functools.partial(project_other, n, o_ref, c0)
               for n, o_ref in ((0, u_ref), (1, v_ref), (5, z_ref), (6, ga_ref), (7, gb_ref))
               for c0 in range(0, d, step)]
    n_front = d // step
    for p in passes[:n_front]:
        p()
    rest = passes[n_front:]
    convs = [(t, hd) for t in range(3) for hd in range(N_HEADS)]
    n_rest = len(rest)
    conv_args = ((qo_ref, True, HEAD_DIM ** -0.5), (ko_ref, True, 1.0), (vo_ref, False, 1.0))
    anchor = None
    for j, (t, hd) in enumerate(convs):
        for _ in range((j + 1) * n_rest // len(convs) - j * n_rest // len(convs)):
            anchor = rest.pop(0)()
        conv_head(t, hd, *conv_args[t], anchor)
    for p in rest:
        p()
    ab = jnp.dot(hb, w_ref[:, 8 * d:8 * d + 128], preferred_element_type=F32)
    lane = lax.broadcasted_iota(jnp.int32, ab.shape, 1)
    is_decay = ((lane // N_HEADS) % 2) == 0
    g = -jnp.exp(alog_ref[...]) * _softplus(ab + dtb_ref[...])
    gate = jnp.where(is_decay, g, _sigmoid(ab))
    gate = jnp.where(lane < 4 * N_HEADS, gate, 0.0)
    gate_ref[0] = gate
    gate_ref[1] = pltpu.roll(gate, 128 - 2 * N_HEADS, axis=1)


def _in_call(xs, mods, layer, norm_g, w_all, conv_w, alog_row, dtb_row, mod_row, n_ctx_tiles, ctx_tiles, lat_tiles):
    d = D_MODEL
    rows = sum(a.shape[0] for a in xs)
    nt = rows // ROW_TILE
    sh_idx = (layer * 6 + 0) * COND_ROWS
    sc_idx = (layer * 6 + 1) * COND_ROWS
    row_spec = pl.BlockSpec((ROW_TILE, d), lambda i: (i, 0))
    small = pl.BlockSpec((1, 128), lambda i: (0, 0))
    return pl.pallas_call(
        functools.partial(_in_kernel, len(xs), n_ctx_tiles, ctx_tiles, lat_tiles),
        grid=(nt,),
        in_specs=(_row_specs(xs, n_ctx_tiles, 0, d) + _halo_specs(xs, n_ctx_tiles, d, False)
                  + _halo_specs(xs, n_ctx_tiles, d, True) + [
            pl.BlockSpec((1, 1, d), lambda i: (sh_idx + mod_row(i), 0, 0)),
            pl.BlockSpec((1, 1, d), lambda i: (sc_idx + mod_row(i), 0, 0)),
            pl.BlockSpec((1, d), lambda i: (0, 0)),
            pl.BlockSpec((d, 8 * d + 128), lambda i: (0, 0), pipeline_mode=pl.Buffered(1)),
            pl.BlockSpec((8, 3 * d), lambda i: (0, 0)),
            small, small,
        ]),
        out_specs=[row_spec] * 8 + [pl.BlockSpec((2, ROW_TILE, 128), lambda i: (0, i, 0))],
        out_shape=[jax.ShapeDtypeStruct((rows, d), BF16)] * 8 + [jax.ShapeDtypeStruct((2, rows, 128), F32)],
        scratch_shapes=[pltpu.VMEM((3, ROW_TILE + 2 * HALO, d), F32)],
        compiler_params=_cparams(("arbitrary",)),
        name="norm_in_proj_conv",
    )(*xs, *xs, *xs, mods, mods, norm_g, w_all, conv_w, alog_row, dtb_row)


def _gdn_kernel(qf_ref, kf_ref, vf_ref, gf_ref, qb_ref, kb_ref, vb_ref, gb_ref, of_ref, ob_ref, s_ref):
    @pl.when(pl.program_id(1) == 0)
    def _():
        s_ref[...] = jnp.zeros_like(s_ref)

    for n in range(GDN_BLOCK):
        rows_f = slice(n * GDN_CHUNK, (n + 1) * GDN_CHUNK)
        rows_b = slice((GDN_BLOCK - 1 - n) * GDN_CHUNK, (GDN_BLOCK - n) * GDN_CHUNK)
        _gdn_chunk(((True, rows_f, qf_ref, kf_ref, vf_ref, gf_ref, of_ref),
                    (False, rows_b, qb_ref, kb_ref, vb_ref, gb_ref, ob_ref)), s_ref)


def _gdn_chunk(scans, s_ref):
    c = GDN_CHUNK
    row = lax.broadcasted_iota(jnp.int32, (c, c), 0)
    col = lax.broadcasted_iota(jnp.int32, (c, c), 1)
    eye = (row == col).astype(F32)
    nt_dims = (((1,), (1,)), ((), ()))
    lanes = [slice(h * HEAD_DIM, (h + 1) * HEAD_DIM) for h in range(N_HEADS)]

    q, k, v, gc_col, gc_row, gtot_row, beta_row, incl, strict, dst = [], [], [], [], [], [], [], [], [], []
    for d_idx, (fwd, rows, q_ref, k_ref, v_ref, gate_ref, o_ref) in enumerate(scans):
        incl_d = (row >= col) if fwd else (row <= col)
        strict_d = (row > col) if fwd else (row < col)
        gate = gate_ref[0, rows, :]
        gc_all = jnp.dot(incl_d.astype(F32), gate, precision=HIGHEST, preferred_element_type=F32)
        gtot_all = jnp.dot(jnp.ones((c, c), F32), gate, precision=HIGHEST, preferred_element_type=F32)
        gate_t = gate.T
        gc_t = gc_all.T
        gtot_t = gtot_all.T
        for h in range(N_HEADS):
            q.append(q_ref[rows, lanes[h]])
            k.append(k_ref[rows, lanes[h]])
            v.append(v_ref[rows, lanes[h]])
            gc_col.append(jnp.broadcast_to(gc_all[:, h:h + 1], (c, c)))
            gc_row.append(gc_t[h:h + 1, :])
            gtot_row.append(gtot_t[h:h + 1, :])
            beta_row.append(gate_t[N_HEADS + h:N_HEADS + h + 1, :])
            incl.append(incl_d)
            strict.append(strict_d)
            dst.append((o_ref, rows, lanes[h], d_idx, h))
    chains = range(len(q))

    decay = [jnp.exp(jnp.where(incl[i], gc_col[i] - gc_row[i], -jnp.inf)) * beta_row[i] for i in chains]
    kq = [lax.dot_general(jnp.concatenate([k[i], q[i]], axis=0), k[i], nt_dims, preferred_element_type=F32)
          for i in chains]
    n_mat = [jnp.where(strict[i], kq[i][:c] * decay[i], 0.0) for i in chains]
    a_mat = [(kq[i][c:] * decay[i]).astype(BF16) for i in chains]
    a_b = [(eye + n_mat[i]).astype(BF16) for i in chains]
    t_inv = [eye - n_mat[i] for i in chains]
    for _ in range(6):
        tb = [t_inv[i].astype(BF16) for i in chains]
        err = [(eye - jnp.dot(a_b[i], tb[i], preferred_element_type=F32)).astype(BF16) for i in chains]
        t_inv = [t_inv[i] + jnp.dot(tb[i], err[i], preferred_element_type=F32) for i in chains]
    u_val = [jnp.dot(t_inv[i].astype(BF16), v[i], preferred_element_type=F32) for i in chains]
    w_key = [jnp.dot((t_inv[i] * jnp.exp(gc_row[i])).astype(BF16), k[i], preferred_element_type=F32)
             for i in chains]

    s_old = [s_ref[dst[i][3], dst[i][4]] for i in chains]
    sb = [s_old[i].astype(BF16) for i in chains]
    wq_s = [jnp.dot(jnp.concatenate([w_key[i].astype(BF16), q[i]], axis=0), sb[i], preferred_element_type=F32)
            for i in chains]
    v_new = [(u_val[i] - wq_s[i][:c]).astype(BF16) for i in chains]
    kd_t = [(k[i].astype(F32).T * (jnp.exp(gtot_row[i] - gc_row[i]) * beta_row[i])).astype(BF16) for i in chains]
    ak_v = [jnp.dot(jnp.concatenate([a_mat[i], kd_t[i]], axis=0), v_new[i], preferred_element_type=F32)
            for i in chains]
    for i in chains:
        o_ref, rows, ln, d_idx, h = dst[i]
        s_ref[d_idx, h] = s_old[i] * jnp.exp(gtot_row[i]) + ak_v[i][c:]
        o_ref[rows, ln] = (jnp.exp(gc_col[i]) * wq_s[i][c:] + ak_v[i][:c]).astype(BF16)


def _gdn_call(q, k, v, gates, batch, ctx_len, seq_len):
    rows, d = q.shape
    c = GDN_BLOCK * GDN_CHUNK
    assert ctx_len % c == 0 and seq_len % c == 0
    n_ctx = ctx_len // c
    n_lat = seq_len // c
    n_steps = n_ctx + n_lat

    def chunk(b, fwd, s):
        pos = s if fwd else jnp.where(s < n_ctx, n_ctx - 1 - s, n_ctx + n_steps - 1 - s)
        return jnp.where(pos < n_ctx, b * n_ctx + pos, batch * n_ctx + b * n_lat + pos - n_ctx)

    def specs(fwd):
        blk = pl.BlockSpec((c, d), lambda b, s: (chunk(b, fwd, s), 0))
        return [blk, blk, blk, pl.BlockSpec((1, c, 128), lambda b, s: (0 if fwd else 1, chunk(b, fwd, s), 0))]

    return pl.pallas_call(
        _gdn_kernel,
        grid=(batch, n_steps),
        in_specs=specs(True) + specs(False),
        out_specs=[pl.BlockSpec((c, d), lambda b, s: (chunk(b, True, s), 0)),
                   pl.BlockSpec((c, d), lambda b, s: (chunk(b, False, s), 0))],
        out_shape=[jax.ShapeDtypeStruct((rows, d), BF16)] * 2,
        scratch_shapes=[pltpu.VMEM((2, N_HEADS, HEAD_DIM, HEAD_DIM), F32)],
        compiler_params=_cparams(("arbitrary", "arbitrary")),
        name="gated_delta",
    )(q, k, v, gates, q, k, v, gates)


def _mix_kernel(n_src, n_ctx_tiles, tile_off, *refs):
    x_refs = refs[:n_src]
    (u_ref, v_ref, of_ref, ob_ref, z_ref, ga_ref, gb_ref,
     g1_ref, sh2_ref, sc2_ref, n2g_ref, lng_ref, ws_ref, bs_ref, bng_ref,
     wpa_ref, wpb_ref, wout_ref, wr_ref, br_ref,
     xo_ref, h2_ref, comb_ref, cnt_ref, sa_ref, sb_ref) = refs[n_src:]
    d = D_MODEL
    ug = _gelu_tanh(u_ref[...].astype(F32))
    vg = _gelu_tanh(v_ref[...].astype(F32))
    mu = jnp.mean(vg, axis=-1, keepdims=True)
    vc = vg - mu
    var = jnp.mean(vc * vc, axis=-1, keepdims=True)
    vn = (vc * lax.rsqrt(var + EPS) * lng_ref[...]).astype(BF16)
    for ch in range(MIX_TILE // SGU_CHUNK):
        rws = slice(ch * SGU_CHUNK, (ch + 1) * SGU_CHUNK)
        for g in range(SGU_GROUPS):
            lanes = slice(g * 128, (g + 1) * 128)
            mixed = jnp.dot(ws_ref[g], vn[rws, lanes], preferred_element_type=F32) + bs_ref[:, lanes]
            sa_ref[rws, lanes] = (ug[rws, lanes] * mixed).astype(BF16)
    y_a = jnp.dot(sa_ref[...], wpa_ref[...], preferred_element_type=F32)
    for h in range(N_HEADS):
        lanes = slice(h * HEAD_DIM, (h + 1) * HEAD_DIM)
        o = of_ref[:, lanes].astype(F32) + ob_ref[:, lanes].astype(F32)
        ms = jnp.mean(o * o, axis=-1, keepdims=True)
        o = o * lax.rsqrt(ms + EPS) * bng_ref[...]
        sb_ref[:, lanes] = (o * _silu(z_ref[:, lanes].astype(F32))).astype(BF16)
    y_b = jnp.dot(sb_ref[...], wpb_ref[...], preferred_element_type=F32)
    merged = _sigmoid(ga_ref[...].astype(F32)) * y_a + _sigmoid(gb_ref[...].astype(F32)) * y_b
    y = jnp.dot(merged.astype(BF16), wout_ref[...], preferred_element_type=F32)
    xn = _load_rows(x_refs, pl.program_id(0) + tile_off, n_ctx_tiles) + g1_ref[0] * y
    xo_ref[...] = xn
    ms = jnp.mean(xn * xn, axis=-1, keepdims=True)
    h2 = xn * lax.rsqrt(ms + EPS) * n2g_ref[...]
    h2 = h2 * (1.0 + sc2_ref[0]) + sh2_ref[0]
    h2_ref[...] = h2.astype(BF16)
    logits = lax.dot_general(wr_ref[...], h2, (((1,), (1,)), ((), ())),
                             precision=HIGHEST, preferred_element_type=F32)
    scores = _sigmoid(logits)
    sel = scores + br_ref[...]
    srow = [sel[e:e + 1, :] for e in range(N_EXPERTS)]
    grp = []
    for g in range(N_EXPERT_GROUPS):
        m = srow[4 * g:4 * g + 4]
        best2 = None
        for a in range(4):
            for b in range(a + 1, 4):
                pair = m[a] + m[b]
                best2 = pair if best2 is None else jnp.maximum(best2, pair)
        grp.append(best2)
    best_val = grp[0]
    best_idx = jnp.zeros_like(best_val, dtype=jnp.int32)
    for g in range(1, N_EXPERT_GROUPS):
        better = grp[g] > best_val
        best_val = jnp.where(better, grp[g], best_val)
        best_idx = jnp.where(better, g, best_idx)
    picked = []
    for e in range(N_EXPERTS):
        g = e // EXPERTS_PER_GROUP
        rank = jnp.zeros_like(best_idx)
        for o_e in range(4 * g, 4 * g + 4):
            if o_e == e:
                continue
            ahead = (srow[o_e] > srow[e]) if o_e > e else (srow[o_e] >= srow[e])
            rank = rank + ahead.astype(jnp.int32)
        chosen = jnp.logical_and(best_idx == g, rank < 2)
        picked.append(jnp.where(chosen, scores[e:e + 1, :], 0.0))
    total = picked[0]
    for e in range(1, N_EXPERTS):
        total = total + picked[e]
    inv = 1.0 / total
    for e in range(N_EXPERTS):
        w_e = picked[e] * inv
        comb_ref[e:e + 1, :] = w_e
        n_e = jnp.sum(jnp.where(w_e > 0.0, 1.0, 0.0), axis=-1, keepdims=True)
        cnt_ref[0, e:e + 1, :] = jnp.broadcast_to(n_e, (1, 128))


def _mix_call(xs, u, v, o_f, o_b, z, ga, gb, mods, layer, norm2_g, ln_g, ws, bs, bng,
              wpa, wpb, wout, wr_t, br, mod_row, n_ctx_tiles, tile_off):
    d = D_MODEL
    nt = sum(a.shape[0] for a in xs) // MIX_TILE - tile_off
    rows = nt * MIX_TILE
    per = MIX_TILE // ROW_TILE
    g1_idx = (layer * 6 + 2) * COND_ROWS
    sh2_idx = (layer * 6 + 3) * COND_ROWS
    sc2_idx = (layer * 6 + 4) * COND_ROWS
    row_spec = pl.BlockSpec((MIX_TILE, d), lambda i: (i + tile_off, 0))
    out_spec = pl.BlockSpec((MIX_TILE, d), lambda i: (i, 0))
    vec = pl.BlockSpec((1, d), lambda i: (0, 0))
    wspec = pl.BlockSpec((d, d), lambda i: (0, 0))

    def mod(idx):
        return pl.BlockSpec((1, 1, d), lambda i: (idx + mod_row((i + tile_off) * per), 0, 0))

    return pl.pallas_call(
        functools.partial(_mix_kernel, len(xs), n_ctx_tiles, tile_off),
        grid=(nt,),
        in_specs=_row_specs(xs, n_ctx_tiles, tile_off, d, MIX_TILE) + [
            row_spec, row_spec,
            row_spec, row_spec,
            row_spec, row_spec, row_spec,
            mod(g1_idx), mod(sh2_idx), mod(sc2_idx),
            vec, vec,
            pl.BlockSpec((SGU_GROUPS, SGU_CHUNK, SGU_CHUNK), lambda i: (0, 0, 0)),
            pl.BlockSpec((SGU_CHUNK, d), lambda i: (0, 0)),
            pl.BlockSpec((1, HEAD_DIM), lambda i: (0, 0)),
            wspec, wspec, wspec,
            pl.BlockSpec((N_EXPERTS, d), lambda i: (0, 0)),
            pl.BlockSpec((N_EXPERTS, 1), lambda i: (0, 0)),
        ],
        out_specs=[out_spec, out_spec, pl.BlockSpec((N_EXPERTS, MIX_TILE), lambda i: (0, i)),
                   pl.BlockSpec((1, N_EXPERTS, 128), lambda i: (i, 0, 0))],
        out_shape=[jax.ShapeDtypeStruct((rows, d), F32), jax.ShapeDtypeStruct((rows, d), BF16),
                   jax.ShapeDtypeStruct((N_EXPERTS, rows), F32),
                   jax.ShapeDtypeStruct((nt, N_EXPERTS, 128), F32)],
        scratch_shapes=[pltpu.VMEM((MIX_TILE, d), BF16), pltpu.VMEM((MIX_TILE, d), BF16)],
        compiler_params=_cparams(("arbitrary",)),
        name="mix_merge_router",
    )(*xs, u, v, o_f, o_b, z, ga, gb, mods, mods, mods, norm2_g, ln_g, ws, bs, bng,
      wpa, wpb, wout, wr_t, br)


def _segment_copies(t, off_s, gs_s, l16_s, local_ref, global_ref, sem, to_global, wait):
    for e in range(N_EXPERTS):
        idx = t * N_EXPERTS + e
        lo = off_s[idx]
        go = gs_s[idx]
        n16 = l16_s[idx]
        done = jnp.int32(0)
        for bit in SEG_BITS:
            size = bit * SEG_ALIGN
            present = (n16 & bit) != 0
            l_at = local_ref.at[pl.ds(pl.multiple_of(lo + done, SEG_ALIGN), size)]
            g_at = global_ref.at[pl.ds(pl.multiple_of(go + done, SEG_ALIGN), size)]
            cp = pltpu.make_async_copy(l_at, g_at, sem) if to_global else pltpu.make_async_copy(g_at, l_at, sem)

            @pl.when(present)
            def _():
                if wait:
                    cp.wait()
                else:
                    cp.start()

            done = done + jnp.where(present, size, 0)


def _sort_kernel(off_s, gs_s, l16_s, h_ref, comb_ref, offcol_ref, xs_in_ref, slots_ref, xs_ref, hs_ref, sem):
    del xs_in_ref
    t = pl.program_id(0)
    slot = t % 2
    other = 1 - slot
    ts = SORT_TILE
    comb = comb_ref[...]
    asg = comb > 0.0
    row = lax.broadcasted_iota(jnp.int32, (ts, ts), 0)
    col = lax.broadcasted_iota(jnp.int32, (ts, ts), 1)
    before = jnp.where(row < col, 1.0, 0.0).astype(BF16)
    rank = jnp.dot(jnp.where(asg, 1.0, 0.0).astype(BF16), before, preferred_element_type=F32)
    pos = offcol_ref[0][:, 0:1] + rank
    p_lo = jnp.min(jnp.where(asg, pos, 1e9), axis=0, keepdims=True)
    p_hi = jnp.max(jnp.where(asg, pos, -1.0), axis=0, keepdims=True)
    w_lo = jnp.sum(jnp.where(jnp.logical_and(asg, pos == p_lo), comb, 0.0), axis=0, keepdims=True)
    w_hi = jnp.sum(jnp.where(jnp.logical_and(asg, pos == p_hi), comb, 0.0), axis=0, keepdims=True)
    w_hi = jnp.where(p_hi > p_lo, w_hi, 0.0)
    dest = lax.broadcasted_iota(jnp.int32, (SORT_CAP, ts), 0).astype(F32)
    onehot = jnp.where(jnp.logical_or(dest == p_lo, dest == p_hi), 1.0, 0.0).astype(BF16)
    hs_ref[slot] = jnp.dot(onehot, h_ref[...], preferred_element_type=F32).astype(BF16)
    slot_rows = jnp.concatenate([p_lo, p_hi, w_lo, w_hi, jnp.zeros((124, ts), F32)], axis=0)
    slots_ref[...] = slot_rows.T
    _segment_copies(t, off_s, gs_s, l16_s, hs_ref.at[slot], xs_ref, sem.at[slot], to_global=True, wait=False)

    @pl.when(t > 0)
    def _():
        _segment_copies(t - 1, off_s, gs_s, l16_s, hs_ref.at[other], xs_ref, sem.at[other], to_global=True, wait=True)

    @pl.when(t == pl.num_programs(0) - 1)
    def _():
        _segment_copies(t, off_s, gs_s, l16_s, hs_ref.at[slot], xs_ref, sem.at[slot], to_global=True, wait=True)


def _sort_call(h2, comb_t, tables, offcol, n_rows_sorted):
    rows, d = h2.shape
    nt = rows // SORT_TILE
    xs0 = jnp.zeros((n_rows_sorted, d), BF16)
    grid_spec = pltpu.PrefetchScalarGridSpec(
        num_scalar_prefetch=3,
        grid=(nt,),
        in_specs=[
            pl.BlockSpec((SORT_TILE, d), lambda i, *_: (i, 0)),
            pl.BlockSpec((N_EXPERTS, SORT_TILE), lambda i, *_: (0, i)),
            pl.BlockSpec((1, N_EXPERTS, 128), lambda i, *_: (i, 0, 0)),
            pl.BlockSpec(memory_space=pl.ANY),
        ],
        out_specs=[
            pl.BlockSpec((SORT_TILE, 128), lambda i, *_: (i, 0)),
            pl.BlockSpec(memory_space=pl.ANY),
        ],
        scratch_shapes=[pltpu.VMEM((2, SORT_CAP, d), BF16), pltpu.SemaphoreType.DMA((2,))],
    )
    slots, xs = pl.pallas_call(
        _sort_kernel,
        grid_spec=grid_spec,
        out_shape=[jax.ShapeDtypeStruct((rows, 128), F32), jax.ShapeDtypeStruct((n_rows_sorted, d), BF16)],
        input_output_aliases={6: 1},
        compiler_params=_cparams(("arbitrary",)),
        name="moe_sort",
    )(*tables, h2, comb_t, offcol, xs0)
    return slots, xs


def _experts_kernel(be_s, bv_s, x_ref, w1_ref, w3_ref, w2_ref, y_ref):
    b = pl.program_id(0)

    @pl.when(bv_s[b] != 0)
    def _():
        x = x_ref[...]
        a = jnp.dot(x, w1_ref[0, 0].astype(BF16), preferred_element_type=F32)
        g = jnp.dot(x, w3_ref[0, 0].astype(BF16), preferred_element_type=F32)
        hid = (_silu(a) * g).astype(BF16)
        y_ref[...] = jnp.dot(hid, w2_ref[0, 0].astype(BF16), preferred_element_type=F32).astype(BF16)

    @pl.when(bv_s[b] == 0)
    def _():
        y_ref[...] = jnp.zeros_like(y_ref)


def _experts_call(xs, blk_expert, blk_valid, w1, w3, w2, layer):
    rows, d = xs.shape
    nb = rows // MOE_BLOCK
    grid_spec = pltpu.PrefetchScalarGridSpec(
        num_scalar_prefetch=2,
        grid=(nb,),
        in_specs=[
            pl.BlockSpec((MOE_BLOCK, d), lambda b, be, bv: (b, 0)),
            pl.BlockSpec((1, 1, d, D_EXPERT), lambda b, be, bv: (layer, be[b], 0, 0)),
            pl.BlockSpec((1, 1, d, D_EXPERT), lambda b, be, bv: (layer, be[b], 0, 0)),
            pl.BlockSpec((1, 1, D_EXPERT, d), lambda b, be, bv: (layer, be[b], 0, 0)),
        ],
        out_specs=pl.BlockSpec((MOE_BLOCK, d), lambda b, be, bv: (b, 0)),
    )
    return pl.pallas_call(
        _experts_kernel,
        grid_spec=grid_spec,
        out_shape=jax.ShapeDtypeStruct((rows, d), BF16),
        compiler_params=_cparams(("arbitrary",)),
        name="moe_experts",
    )(blk_expert, blk_valid, xs, w1, w3, w2)


def _unsort_kernel(final, off_s, gs_s, l16_s, ys_ref, slots_ref, x_ref, fg_ref, g2a_ref, g2b_ref,
                   o_ref, yt_ref, sem):
    t = pl.program_id(0)
    ts = SORT_TILE
    slot = t % 2
    other = 1 - slot

    def fetch(tile, buf):
        yt_ref[buf] = jnp.zeros(yt_ref.shape[1:], yt_ref.dtype)
        _segment_copies(tile, off_s, gs_s, l16_s, yt_ref.at[buf], ys_ref, sem.at[buf], to_global=False, wait=False)

    @pl.when(t == 0)
    def _():
        fetch(t, slot)

    @pl.when(t + 1 < pl.num_programs(0))
    def _():
        fetch(t + 1, other)

    _segment_copies(t, off_s, gs_s, l16_s, yt_ref.at[slot], ys_ref, sem.at[slot], to_global=False, wait=True)
    slots = slots_ref[...]
    src = lax.broadcasted_iota(jnp.int32, (ts, SORT_CAP), 1).astype(F32)
    weights = jnp.where(src == slots[:, 0:1], slots[:, 2:3], 0.0) + jnp.where(src == slots[:, 1:2], slots[:, 3:4], 0.0)
    y = jnp.dot(weights.astype(BF16), yt_ref[slot], preferred_element_type=F32)
    for s, g2_ref in enumerate((g2a_ref, g2b_ref)):
        rws = slice(s * ROW_TILE, (s + 1) * ROW_TILE)
        xn = x_ref[rws, :] + g2_ref[0] * y[rws, :]
        if final:
            ms = jnp.mean(xn * xn, axis=-1, keepdims=True)
            xn = xn * lax.rsqrt(ms + EPS) * fg_ref[...]
        o_ref[rws, :] = xn


def _unsort_call(ys, slots, x, tables, final_g, mods, layer, mod_row, tile_off, final):
    rows, d = x.shape
    nt = rows // SORT_TILE
    sub = SORT_TILE // ROW_TILE
    g2_idx = (layer * 6 + 5) * COND_ROWS
    row_spec = pl.BlockSpec((SORT_TILE, d), lambda i, *_: (i, 0))

    def mod(s):
        return pl.BlockSpec((1, 1, d), lambda i, *_: (g2_idx + mod_row(i * sub + s + tile_off), 0, 0))

    grid_spec = pltpu.PrefetchScalarGridSpec(
        num_scalar_prefetch=3,
        grid=(nt,),
        in_specs=[
            pl.BlockSpec(memory_space=pl.ANY),
            pl.BlockSpec((SORT_TILE, 128), lambda i, *_: (i, 0)),
            row_spec,
            pl.BlockSpec((1, d), lambda i, *_: (0, 0)),
            mod(0), mod(1),
        ],
        out_specs=row_spec,
        scratch_shapes=[pltpu.VMEM((2, SORT_CAP, d), BF16), pltpu.SemaphoreType.DMA((2,))],
    )
    return pl.pallas_call(
        functools.partial(_unsort_kernel, final),
        grid_spec=grid_spec,
        out_shape=jax.ShapeDtypeStruct((rows, d), F32),
        compiler_params=_cparams(("arbitrary",)),
        name="moe_unsort_residual",
    )(*tables, ys, slots, x, final_g, mods, mods)


def _sum_before(a, axis):
    i = jnp.arange(a.shape[axis])
    mask = (i[None, :] < i[:, None]).astype(a.dtype)
    moved = jnp.moveaxis(a, axis, -1)
    return jnp.moveaxis(jnp.sum(moved[..., None, :] * mask, axis=-1), -1, axis)


def _moe_tables(cnt_blocks, n_blocks):
    cnt = cnt_blocks[:, :, 0].astype(jnp.int32)
    seg = (cnt + SEG_ALIGN - 1) // SEG_ALIGN * SEG_ALIGN
    off = _sum_before(seg, 1)
    blocks_e = (seg.sum(axis=0) + MOE_BLOCK - 1) // MOE_BLOCK
    first_block = _sum_before(blocks_e, 0)
    gstart = first_block[None, :] * MOE_BLOCK + _sum_before(seg, 0)
    blk = jnp.arange(n_blocks, dtype=jnp.int32)
    last_block = first_block + blocks_e
    blk_expert = jnp.minimum(jnp.sum((blk[:, None] >= last_block[None, :]).astype(jnp.int32), axis=1),
                             N_EXPERTS - 1)
    blk_valid = (blk < blocks_e.sum()).astype(jnp.int32)
    tables = (off.reshape(-1).astype(jnp.int32), gstart.reshape(-1).astype(jnp.int32),
              (seg // SEG_ALIGN).reshape(-1).astype(jnp.int32))
    offcol = jnp.broadcast_to(off.astype(F32)[:, :, None], off.shape + (128,))
    return tables, offcol, blk_expert.astype(jnp.int32), blk_valid


def _moe_call(h2, comb_t, cnt_blocks, x, w1, w3, w2, final_g, mods, layer, mod_row, tile_off, final):
    rows = x.shape[0]
    n_tiles = rows // SORT_TILE
    max_rows = 2 * rows + n_tiles * N_EXPERTS * (SEG_ALIGN - 1)
    n_blocks = max_rows // MOE_BLOCK + N_EXPERTS
    tables, offcol, blk_expert, blk_valid = _moe_tables(cnt_blocks, n_blocks)
    slots, xs = _sort_call(h2, comb_t, tables, offcol, n_blocks * MOE_BLOCK)
    ys = _experts_call(xs, blk_expert, blk_valid, w1, w3, w2, layer)
    return _unsort_call(ys, slots, x, tables, final_g, mods, layer, mod_row, tile_off, final)


def kernel(x, c, ctx, c_ctx, w_ada, b_ada, norm1_g, norm2_g, w_in, a_ln_g, a_w_s, a_b_s, b_conv_w, b_A_log, b_dt_bias, b_norm_g, w_proj_a, w_proj_b, w_out, w_router, b_router, w_e1, w_e3, w_e2, final_g):
    batch, seq_len, d = x.shape
    ctx_len = ctx.shape[1]
    n_layers = w_ada.shape[0]
    assert d == D_MODEL and batch + 1 <= COND_ROWS
    assert ctx_len % ROW_TILE == 0 and seq_len % ROW_TILE == 0
    assert (batch * ctx_len) % SORT_TILE == 0 and (batch * seq_len) % SORT_TILE == 0
    assert MIX_TILE == SORT_TILE and (batch * ctx_len) % MIX_TILE == 0 and seq_len % MIX_TILE == 0
    ctx_tiles = ctx_len // ROW_TILE
    lat_tiles = seq_len // ROW_TILE
    n_ctx_tiles = batch * ctx_tiles

    def mod_row(tile):
        return jnp.where(tile < n_ctx_tiles, batch, (tile - n_ctx_tiles) // lat_tiles)

    xs = (ctx.reshape(batch * ctx_len, d), x.reshape(batch * seq_len, d))
    cond = jnp.concatenate([c, c_ctx[None, :], jnp.zeros((COND_ROWS - batch - 1, d), F32)], axis=0)
    mods = _ada_call(cond, w_ada, b_ada).reshape(n_layers * 6 * COND_ROWS, 1, d)

    n_ab = 4 * N_HEADS
    ab_lo = 6 * d
    wr_t = w_router.T
    br = b_router.reshape(N_EXPERTS, 1)
    fg = final_g.reshape(1, d)
    for l in range(n_layers):
        w_bf = w_in[l].astype(BF16)
        w_all = jnp.concatenate([w_bf[:, :ab_lo], w_bf[:, ab_lo + n_ab:], w_bf[:, ab_lo:ab_lo + n_ab],
                                 jnp.zeros((d, 128 - n_ab), BF16)], axis=1)
        conv_w = jnp.pad(b_conv_w[l], ((0, 8 - CONV_TAPS), (0, 0)))
        zeros8 = jnp.zeros((N_HEADS,), F32)
        alog_row = jnp.pad(jnp.concatenate([b_A_log[l, 0], zeros8, b_A_log[l, 1], zeros8]), (0, 128 - n_ab)).reshape(1, 128)
        dtb_row = jnp.pad(jnp.concatenate([b_dt_bias[l, 0], zeros8, b_dt_bias[l, 1], zeros8]), (0, 128 - n_ab)).reshape(1, 128)
        u, v, z, ga, gb, qn, kn, vs, gates = _in_call(
            xs, mods, l, norm1_g[l].reshape(1, d), w_all, conv_w, alog_row, dtb_row, mod_row,
            n_ctx_tiles, ctx_tiles, lat_tiles)
        o_f, o_b = _gdn_call(qn, kn, vs, gates, batch, ctx_len, seq_len)

        bs = jnp.repeat(a_b_s[l].T, SGU_CHUNK, axis=1)
        bng = b_norm_g[l].reshape(1, HEAD_DIM)
        last = l == n_layers - 1
        tile_off = n_ctx_tiles if last else 0
        x_new, h2, comb_t, cnt = _mix_call(
            xs, u, v, o_f, o_b, z, ga, gb, mods, l, norm2_g[l].reshape(1, d), a_ln_g[l].reshape(1, d),
            a_w_s[l].astype(BF16), bs, bng,
            w_proj_a[l].astype(BF16), w_proj_b[l].astype(BF16), w_out[l].astype(BF16), wr_t, br, mod_row,
            batch * ctx_len // MIX_TILE, tile_off * ROW_TILE // MIX_TILE)
        xs = (_moe_call(h2, comb_t, cnt, x_new, w_e1, w_e3, w_e2,
                        fg, mods, l, mod_row, tile_off, final=last),)
    return xs[0].reshape(batch, seq_len, d)
```

```python
import functools

import jax
import jax.numpy as jnp
from jax import lax
from jax.experimental import pallas as pl
from jax.experimental.pallas import tpu as pltpu

F32 = jnp.float32
BF16 = jnp.bfloat16
HIGHEST = lax.Precision.HIGHEST

EPS = 1e-6
D_MODEL = 1024
N_HEADS = 8
HEAD_DIM = 128
SGU_CHUNK = 128
SGU_GROUPS = 8
CONV_TAPS = 5
N_EXPERTS = 16
EXPERTS_PER_GROUP = 4
N_EXPERT_GROUPS = 4
D_EXPERT = 512
ROW_TILE = 256
MIX_TILE = 512
GDN_CHUNK = 128
GDN_BLOCK = 2
SORT_TILE = 512
SEG_ALIGN = 16
SORT_CAP = 1280
MOE_BLOCK = 512
SEG_BITS = (32, 16, 8, 4, 2, 1)
COND_ROWS = 8
HALO = 8
VMEM_LIMIT = 56 * 1024 * 1024


def _cparams(sem):
    return pltpu.CompilerParams(dimension_semantics=sem, vmem_limit_bytes=VMEM_LIMIT)


def _sigmoid(x):
    return 0.5 * jnp.tanh(0.5 * x) + 0.5


def _silu(x):
    return x * _sigmoid(x)


def _gelu_tanh(x):
    return 0.5 * x * (1.0 + jnp.tanh(0.7978845608028654 * (x + 0.044715 * (x * x * x))))


def _softplus(x):
    return jnp.maximum(x, 0.0) + jnp.log1p(jnp.exp(-jnp.abs(x)))


def _ada_kernel(cond_ref, w_ref, b_ref, o_ref):
    s = _silu(cond_ref[...])
    o_ref[0, 0] = jnp.dot(s, w_ref[0], precision=HIGHEST, preferred_element_type=F32) + b_ref[0, 0]


def _ada_call(cond, w_ada, b_ada):
    n_layers = w_ada.shape[0]
    d = D_MODEL
    return pl.pallas_call(
        _ada_kernel,
        grid=(n_layers, 6),
        in_specs=[
            pl.BlockSpec((COND_ROWS, d), lambda l, j: (0, 0)),
            pl.BlockSpec((1, d, d), lambda l, j: (l, 0, j)),
            pl.BlockSpec((1, 1, 1, d), lambda l, j: (l, j, 0, 0)),
        ],
        out_specs=pl.BlockSpec((1, 1, COND_ROWS, d), lambda l, j: (l, j, 0, 0)),
        out_shape=jax.ShapeDtypeStruct((n_layers, 6, COND_ROWS, d), F32),
        compiler_params=_cparams(("arbitrary", "arbitrary")),
        name="ada_params",
    )(cond, w_ada, b_ada.reshape(n_layers, 6, 1, d))


def _load_rows(x_refs, tile, n_ctx_tiles):
    if len(x_refs) == 1:
        return x_refs[0][...]
    return jnp.where(tile < n_ctx_tiles, x_refs[0][...], x_refs[1][...])


def _row_specs(xs, n_ctx_tiles, tile_off, d, tile=ROW_TILE):
    if len(xs) == 1:
        return [pl.BlockSpec((tile, d), lambda i: (i + tile_off, 0))]
    return [pl.BlockSpec((tile, d), lambda i: (jnp.minimum(i + tile_off, n_ctx_tiles - 1), 0)),
            pl.BlockSpec((tile, d), lambda i: (jnp.maximum(i + tile_off - n_ctx_tiles, 0), 0))]


def _halo_specs(xs, n_ctx_tiles, d, nxt):
    per = ROW_TILE // HALO

    def spec(n_blocks, first_tile):
        if nxt:
            return pl.BlockSpec((HALO, d), lambda i: (jnp.clip((i - first_tile + 1) * per, 0, n_blocks - 1), 0))
        return pl.BlockSpec((HALO, d), lambda i: (jnp.clip((i - first_tile) * per - 1, 0, n_blocks - 1), 0))

    if len(xs) == 1:
        return [spec(xs[0].shape[0] // HALO, 0)]
    return [spec(xs[0].shape[0] // HALO, 0), spec(xs[1].shape[0] // HALO, n_ctx_tiles)]


def _in_kernel(n_src, n_ctx_tiles, ctx_tiles, lat_tiles, *refs):
    x_refs, xp_refs, xn_refs = refs[:n_src], refs[n_src:2 * n_src], refs[2 * n_src:3 * n_src]
    (sh_ref, sc_ref, g_ref, w_ref, cw_ref, alog_ref, dtb_ref,
     u_ref, v_ref, z_ref, ga_ref, gb_ref, qo_ref, ko_ref, vo_ref, gate_ref, ext_ref) = refs[3 * n_src:]
    i = pl.program_id(0)
    d = D_MODEL
    j = jnp.where(i < n_ctx_tiles, i % ctx_tiles, (i - n_ctx_tiles) % lat_tiles)
    last = jnp.where(i < n_ctx_tiles, ctx_tiles - 1, lat_tiles - 1)
    has_prev = (j != 0).astype(F32)
    has_next = (j != last).astype(F32)

    x = jnp.concatenate([_load_rows(xp_refs, i, n_ctx_tiles), _load_rows(x_refs, i, n_ctx_tiles),
                         _load_rows(xn_refs, i, n_ctx_tiles)], axis=0)
    ms = jnp.mean(x * x, axis=-1, keepdims=True)
    h = x * lax.rsqrt(ms + EPS) * g_ref[...]
    h = h * (1.0 + sc_ref[0]) + sh_ref[0]
    hb_ext = h.astype(BF16)
    hb = h[HALO:HALO + ROW_TILE].astype(BF16)
    pad = CONV_TAPS // 2
    lo, hi = HALO, HALO + ROW_TILE
    step = 2 * HEAD_DIM

    def project_qkv(t, c0):
        cols = slice((2 + t) * d + c0, (2 + t) * d + c0 + step)
        proj = jnp.dot(hb_ext, w_ref[:, cols], preferred_element_type=F32)
        ext_ref[t, 0:lo, c0:c0 + step] = proj[0:lo] * has_prev
        ext_ref[t, lo:hi, c0:c0 + step] = proj[lo:hi]
        ext_ref[t, hi:hi + HALO, c0:c0 + step] = proj[hi:hi + HALO] * has_next
        return proj[hi + HALO - 1:hi + HALO, 0:HEAD_DIM]

    def conv_head(t, hd, o_ref, l2, scale, anchor):
        lanes = slice(hd * HEAD_DIM, (hd + 1) * HEAD_DIM)
        acc = None
        for tap in range(CONV_TAPS):
            w_row = cw_ref[tap:tap + 1, t * d + hd * HEAD_DIM:t * d + (hd + 1) * HEAD_DIM]
            if tap == 0 and anchor is not None:
                w_row = w_row + jnp.where(i < 0, anchor, 0.0)
            start = lo - pad + tap
            term = ext_ref[t, start:start + ROW_TILE, lanes] * w_row
            acc = term if acc is None else acc + term
        y = _silu(acc)
        if l2:
            ss = jnp.sum(y * y, axis=-1, keepdims=True)
            y = y * (lax.rsqrt(ss + EPS) * scale)
        o_ref[:, lanes] = y.astype(BF16)

    passes = [functools.partial(project_qkv, t, c0) for t in range(3) for c0 in range(0, d, step)]

    def project_other(n, o_ref, c0):
        res = jnp.dot(hb, w_ref[:, n * d + c0:n * d + c0 + step], preferred_element_type=F32)
        o_ref[:, c0:c0 + step] = res.astype(BF16)
        return res[ROW_TILE - 1:ROW_TILE, 0:HEAD_DIM]

    passes += [functools.partial(project_other, n, o_ref, c0)
               for n, o_ref in ((0, u_ref), (1, v_ref), (5, z_ref), (6, ga_ref), (7, gb_ref))
               for c0 in range(0, d, step)]
    n_front = d // step
    for p in passes[:n_front]:
        p()
    rest = passes[n_front:]
    convs = [(t, hd) for t in range(3) for hd in range(N_HEADS)]
    n_rest = len(rest)
    conv_args = ((qo_ref, True, HEAD_DIM ** -0.5), (ko_ref, True, 1.0), (vo_ref, False, 1.0))
    anchor = None
    for j, (t, hd) in enumerate(convs):
        for _ in range((j + 1) * n_rest // len(convs) - j * n_rest // len(convs)):
            anchor = rest.pop(0)()
        conv_head(t, hd, *conv_args[t], anchor)
    for p in rest:
        p()
    ab = jnp.dot(hb, w_ref[:, 8 * d:8 * d + 128], preferred_element_type=F32)
    lane = lax.broadcasted_iota(jnp.int32, ab.shape, 1)
    is_decay = ((lane // N_HEADS) % 2) == 0
    g = -jnp.exp(alog_ref[...]) * _softplus(ab + dtb_ref[...])
    gate = jnp.where(is_decay, g, _sigmoid(ab))
    gate = jnp.where(lane < 4 * N_HEADS, gate, 0.0)
    gate_ref[0] = gate
    gate_ref[1] = pltpu.roll(gate, 128 - 2 * N_HEADS, axis=1)


def _in_call(xs, mods, layer, norm_g, w_all, conv_w, alog_row, dtb_row, mod_row, n_ctx_tiles, ctx_tiles, lat_tiles):
    d = D_MODEL
    rows = sum(a.shape[0] for a in xs)
    nt = rows // ROW_TILE
    sh_idx = (layer * 6 + 0) * COND_ROWS
    sc_idx = (layer * 6 + 1) * COND_ROWS
    row_spec = pl.BlockSpec((ROW_TILE, d), lambda i: (i, 0))
    small = pl.BlockSpec((1, 128), lambda i: (0, 0))
    return pl.pallas_call(
        functools.partial(_in_kernel, len(xs), n_ctx_tiles, ctx_tiles, lat_tiles),
        grid=(nt,),
        in_specs=(_row_specs(xs, n_ctx_tiles, 0, d) + _halo_specs(xs, n_ctx_tiles, d, False)
                  + _halo_specs(xs, n_ctx_tiles, d, True) + [
            pl.BlockSpec((1, 1, d), lambda i: (sh_idx + mod_row(i), 0, 0)),
            pl.BlockSpec((1, 1, d), lambda i: (sc_idx + mod_row(i), 0, 0)),
            pl.BlockSpec((1, d), lambda i: (0, 0)),
            pl.BlockSpec((d, 8 * d + 128), lambda i: (0, 0), pipeline_mode=pl.Buffered(1)),
            pl.BlockSpec((8, 3 * d), lambda i: (0, 0)),
            small, small,
        ]),
        out_specs=[row_spec] * 8 + [pl.BlockSpec((2, ROW_TILE, 128), lambda i: (0, i, 0))],
        out_shape=[jax.ShapeDtypeStruct((rows, d), BF16)] * 8 + [jax.ShapeDtypeStruct((2, rows, 128), F32)],
        scratch_shapes=[pltpu.VMEM((3, ROW_TILE + 2 * HALO, d), F32)],
        compiler_params=_cparams(("arbitrary",)),
        name="norm_in_proj_conv",
    )(*xs, *xs, *xs, mods, mods, norm_g, w_all, conv_w, alog_row, dtb_row)


def _gdn_kernel(qf_ref, kf_ref, vf_ref, gf_ref, qb_ref, kb_ref, vb_ref, gb_ref, of_ref, ob_ref, s_ref):
    @pl.when(pl.program_id(1) == 0)
    def _():
        s_ref[...] = jnp.zeros_like(s_ref)

    for n in range(GDN_BLOCK):
        rows_f = slice(n * GDN_CHUNK, (n + 1) * GDN_CHUNK)
        rows_b = slice((GDN_BLOCK - 1 - n) * GDN_CHUNK, (GDN_BLOCK - n) * GDN_CHUNK)
        _gdn_chunk(((True, rows_f, qf_ref, kf_ref, vf_ref, gf_ref, of_ref),
                    (False, rows_b, qb_ref, kb_ref, vb_ref, gb_ref, ob_ref)), s_ref)


def _gdn_chunk(scans, s_ref):
    c = GDN_CHUNK
    row = lax.broadcasted_iota(jnp.int32, (c, c), 0)
    col = lax.broadcasted_iota(jnp.int32, (c, c), 1)
    eye = (row == col).astype(F32)
    nt_dims = (((1,), (1,)), ((), ()))
    lanes = [slice(h * HEAD_DIM, (h + 1) * HEAD_DIM) for h in range(N_HEADS)]

    q, k, v, gc_col, gc_row, gtot_row, beta_row, incl, strict, dst = [], [], [], [], [], [], [], [], [], []
    for d_idx, (fwd, rows, q_ref, k_ref, v_ref, gate_ref, o_ref) in enumerate(scans):
        incl_d = (row >= col) if fwd else (row <= col)
        strict_d = (row > col) if fwd else (row < col)
        gate = gate_ref[0, rows, :]
        gc_all = jnp.dot(incl_d.astype(F32), gate, precision=HIGHEST, preferred_element_type=F32)
        gtot_all = jnp.dot(jnp.ones((c, c), F32), gate, precision=HIGHEST, preferred_element_type=F32)
        gate_t = gate.T
        gc_t = gc_all.T
        gtot_t = gtot_all.T
        for h in range(N_HEADS):
            q.append(q_ref[rows, lanes[h]])
            k.append(k_ref[rows, lanes[h]])
            v.append(v_ref[rows, lanes[h]])
            gc_col.append(jnp.broadcast_to(gc_all[:, h:h + 1], (c, c)))
            gc_row.append(gc_t[h:h + 1, :])
            gtot_row.append(gtot_t[h:h + 1, :])
            beta_row.append(gate_t[N_HEADS + h:N_HEADS + h + 1, :])
            incl.append(incl_d)
            strict.append(strict_d)
            dst.append((o_ref, rows, lanes[h], d_idx, h))
    chains = range(len(q))

    decay = [jnp.exp(jnp.where(incl[i], gc_col[i] - gc_row[i], -jnp.inf)) * beta_row[i] for i in chains]
    kq = [lax.dot_general(jnp.concatenate([k[i], q[i]], axis=0), k[i], nt_dims, preferred_element_type=F32)
          for i in chains]
    n_mat = [jnp.where(strict[i], kq[i][:c] * decay[i], 0.0) for i in chains]
    a_mat = [(kq[i][c:] * decay[i]).astype(BF16) for i in chains]
    a_b = [(eye + n_mat[i]).astype(BF16) for i in chains]
    t_inv = [eye - n_mat[i] for i in chains]
    for _ in range(6):
        tb = [t_inv[i].astype(BF16) for i in chains]
        err = [(eye - jnp.dot(a_b[i], tb[i], preferred_element_type=F32)).astype(BF16) for i in chains]
        t_inv = [t_inv[i] + jnp.dot(tb[i], err[i], preferred_element_type=F32) for i in chains]
    u_val = [jnp.dot(t_inv[i].astype(BF16), v[i], preferred_element_type=F32) for i in chains]
    w_key = [jnp.dot((t_inv[i] * jnp.exp(gc_row[i])).astype(BF16), k[i], preferred_element_type=F32)
             for i in chains]

    s_old = [s_ref[dst[i][3], dst[i][4]] for i in chains]
    sb = [s_old[i].astype(BF16) for i in chains]
    wq_s = [jnp.dot(jnp.concatenate([w_key[i].astype(BF16), q[i]], axis=0), sb[i], preferred_element_type=F32)
            for i in chains]
    v_new = [(u_val[i] - wq_s[i][:c]).astype(BF16) for i in chains]
    kd_t = [(k[i].astype(F32).T * (jnp.exp(gtot_row[i] - gc_row[i]) * beta_row[i])).astype(BF16) for i in chains]
    ak_v = [jnp.dot(jnp.concatenate([a_mat[i], kd_t[i]], axis=0), v_new[i], preferred_element_type=F32)
            for i in chains]
    for i in chains:
        o_ref, rows, ln, d_idx, h = dst[i]
        s_ref[d_idx, h] = s_old[i] * jnp.exp(gtot_row[i]) + ak_v[i][c:]
        o_ref[rows, ln] = (jnp.exp(gc_col[i]) * wq_s[i][c:] + ak_v[i][:c]).astype(BF16)


def _gdn_call(q, k, v, gates, batch, ctx_len, seq_len):
    rows, d = q.shape
    c = GDN_BLOCK * GDN_CHUNK
    assert ctx_len % c == 0 and seq_len % c == 0
    n_ctx = ctx_len // c
    n_lat = seq_len // c
    n_steps = n_ctx + n_lat

    def chunk(b, fwd, s):
        pos = s if fwd else jnp.where(s < n_ctx, n_ctx - 1 - s, n_ctx + n_steps - 1 - s)
        return jnp.where(pos < n_ctx, b * n_ctx + pos, batch * n_ctx + b * n_lat + pos - n_ctx)

    def specs(fwd):
        blk = pl.BlockSpec((c, d), lambda b, s: (chunk(b, fwd, s), 0))
        return [blk, blk, blk, pl.BlockSpec((1, c, 128), lambda b, s: (0 if fwd else 1, chunk(b, fwd, s), 0))]

    return pl.pallas_call(
        _gdn_kernel,
        grid=(batch, n_steps),
        in_specs=specs(True) + specs(False),
        out_specs=[pl.BlockSpec((c, d), lambda b, s: (chunk(b, True, s), 0)),
                   pl.BlockSpec((c, d), lambda b, s: (chunk(b, False, s), 0))],
        out_shape=[jax.ShapeDtypeStruct((rows, d), BF16)] * 2,
        scratch_shapes=[pltpu.VMEM((2, N_HEADS, HEAD_DIM, HEAD_DIM), F32)],
        compiler_params=_cparams(("arbitrary", "arbitrary")),
        name="gated_delta",
    )(q, k, v, gates, q, k, v, gates)


def _mix_kernel(n_src, n_ctx_tiles, tile_off, *refs):
    x_refs = refs[:n_src]
    (u_ref, v_ref, of_ref, ob_ref, z_ref, ga_ref, gb_ref,
     g1_ref, sh2_ref, sc2_ref, n2g_ref, lng_ref, ws_ref, bs_ref, bng_ref,
     wpa_ref, wpb_ref, wout_ref, wr_ref, br_ref,
     xo_ref, h2_ref, comb_ref, cnt_ref, sa_ref, sb_ref) = refs[n_src:]
    d = D_MODEL
    ug = _gelu_tanh(u_ref[...].astype(F32))
    vg = _gelu_tanh(v_ref[...].astype(F32))
    mu = jnp.mean(vg, axis=-1, keepdims=True)
    vc = vg - mu
    var = jnp.mean(vc * vc, axis=-1, keepdims=True)
    vn = (vc * lax.rsqrt(var + EPS) * lng_ref[...]).astype(BF16)
    for ch in range(MIX_TILE // SGU_CHUNK):
        rws = slice(ch * SGU_CHUNK, (ch + 1) * SGU_CHUNK)
        for g in range(SGU_GROUPS):
            lanes = slice(g * 128, (g + 1) * 128)
            mixed = jnp.dot(ws_ref[g], vn[rws, lanes], preferred_element_type=F32) + bs_ref[:, lanes]
            sa_ref[rws, lanes] = (ug[rws, lanes] * mixed).astype(BF16)
    y_a = jnp.dot(sa_ref[...], wpa_ref[...], preferred_element_type=F32)
    for h in range(N_HEADS):
        lanes = slice(h * HEAD_DIM, (h + 1) * HEAD_DIM)
        o = of_ref[:, lanes].astype(F32) + ob_ref[:, lanes].astype(F32)
        ms = jnp.mean(o * o, axis=-1, keepdims=True)
        o = o * lax.rsqrt(ms + EPS) * bng_ref[...]
        sb_ref[:, lanes] = (o * _silu(z_ref[:, lanes].astype(F32))).astype(BF16)
    y_b = jnp.dot(sb_ref[...], wpb_ref[...], preferred_element_type=F32)
    merged = _sigmoid(ga_ref[...].astype(F32)) * y_a + _sigmoid(gb_ref[...].astype(F32)) * y_b
    y = jnp.dot(merged.astype(BF16), wout_ref[...], preferred_element_type=F32)
    xn = _load_rows(x_refs, pl.program_id(0) + tile_off, n_ctx_tiles) + g1_ref[0] * y
    xo_ref[...] = xn
    ms = jnp.mean(xn * xn, axis=-1, keepdims=True)
    h2 = xn * lax.rsqrt(ms + EPS) * n2g_ref[...]
    h2 = h2 * (1.0 + sc2_ref[0]) + sh2_ref[0]
    h2_ref[...] = h2.astype(BF16)
    logits = lax.dot_general(wr_ref[...], h2, (((1,), (1,)), ((), ())),
                             precision=HIGHEST, preferred_element_type=F32)
    scores = _sigmoid(logits)
    sel = scores + br_ref[...]
    srow = [sel[e:e + 1, :] for e in range(N_EXPERTS)]
    grp = []
    for g in range(N_EXPERT_GROUPS):
        m = srow[4 * g:4 * g + 4]
        best2 = None
        for a in range(4):
            for b in range(a + 1, 4):
                pair = m[a] + m[b]
                best2 = pair if best2 is None else jnp.maximum(best2, pair)
        grp.append(best2)
    best_val = grp[0]
    best_idx = jnp.zeros_like(best_val, dtype=jnp.int32)
    for g in range(1, N_EXPERT_GROUPS):
        better = grp[g] > best_val
        best_val = jnp.where(better, grp[g], best_val)
        best_idx = jnp.where(better, g, best_idx)
    picked = []
    for e in range(N_EXPERTS):
        g = e // EXPERTS_PER_GROUP
        rank = jnp.zeros_like(best_idx)
        for o_e in range(4 * g, 4 * g + 4):
            if o_e == e:
                continue
            ahead = (srow[o_e] > srow[e]) if o_e > e else (srow[o_e] >= srow[e])
            rank = rank + ahead.astype(jnp.int32)
        chosen = jnp.logical_and(best_idx == g, rank < 2)
        picked.append(jnp.where(chosen, scores[e:e + 1, :], 0.0))
    total = picked[0]
    for e in range(1, N_EXPERTS):
        total = total + picked[e]
    inv = 1.0 / total
    for e in range(N_EXPERTS):
        w_e = picked[e] * inv
        comb_ref[e:e + 1, :] = w_e
        n_e = jnp.sum(jnp.where(w_e > 0.0, 1.0, 0.0), axis=-1, keepdims=True)
        cnt_ref[0, e:e + 1, :] = jnp.broadcast_to(n_e, (1, 128))


def _mix_call(xs, u, v, o_f, o_b, z, ga, gb, mods, layer, norm2_g, ln_g, ws, bs, bng,
              wpa, wpb, wout, wr_t, br, mod_row, n_ctx_tiles, tile_off):
    d = D_MODEL
    nt = sum(a.shape[0] for a in xs) // MIX_TILE - tile_off
    rows = nt * MIX_TILE
    per = MIX_TILE // ROW_TILE
    g1_idx = (layer * 6 + 2) * COND_ROWS
    sh2_idx = (layer * 6 + 3) * COND_ROWS
    sc2_idx = (layer * 6 + 4) * COND_ROWS
    row_spec = pl.BlockSpec((MIX_TILE, d), lambda i: (i + tile_off, 0))
    out_spec = pl.BlockSpec((MIX_TILE, d), lambda i: (i, 0))
    vec = pl.BlockSpec((1, d), lambda i: (0, 0))
    wspec = pl.BlockSpec((d, d), lambda i: (0, 0))

    def mod(idx):
        return pl.BlockSpec((1, 1, d), lambda i: (idx + mod_row((i + tile_off) * per), 0, 0))

    return pl.pallas_call(
        functools.partial(_mix_kernel, len(xs), n_ctx_tiles, tile_off),
        grid=(nt,),
        in_specs=_row_specs(xs, n_ctx_tiles, tile_off, d, MIX_TILE) + [
            row_spec, row_spec,
            row_spec, row_spec,
            row_spec, row_spec, row_spec,
            mod(g1_idx), mod(sh2_idx), mod(sc2_idx),
            vec, vec,
            pl.BlockSpec((SGU_GROUPS, SGU_CHUNK, SGU_CHUNK), lambda i: (0, 0, 0)),
            pl.BlockSpec((SGU_CHUNK, d), lambda i: (0, 0)),
            pl.BlockSpec((1, HEAD_DIM), lambda i: (0, 0)),
            wspec, wspec, wspec,
            pl.BlockSpec((N_EXPERTS, d), lambda i: (0, 0)),
            pl.BlockSpec((N_EXPERTS, 1), lambda i: (0, 0)),
        ],
        out_specs=[out_spec, out_spec, pl.BlockSpec((N_EXPERTS, MIX_TILE), lambda i: (0, i)),
                   pl.BlockSpec((1, N_EXPERTS, 128), lambda i: (i, 0, 0))],
        out_shape=[jax.ShapeDtypeStruct((rows, d), F32), jax.ShapeDtypeStruct((rows, d), BF16),
                   jax.ShapeDtypeStruct((N_EXPERTS, rows), F32),
                   jax.ShapeDtypeStruct((nt, N_EXPERTS, 128), F32)],
        scratch_shapes=[pltpu.VMEM((MIX_TILE, d), BF16), pltpu.VMEM((MIX_TILE, d), BF16)],
        compiler_params=_cparams(("arbitrary",)),
        name="mix_merge_router",
    )(*xs, u, v, o_f, o_b, z, ga, gb, mods, mods, mods, norm2_g, ln_g, ws, bs, bng,
      wpa, wpb, wout, wr_t, br)


def _piece_copies(lo, go, n16, local_ref, global_ref, sem, to_global, wait, repeat_local=False):
    done = jnp.int32(0)
    for bit in SEG_BITS:
        size = bit * SEG_ALIGN
        present = (n16 & bit) != 0
        l_start = lo if repeat_local else lo + done
        l_at = local_ref.at[pl.ds(pl.multiple_of(l_start, SEG_ALIGN), size)]
        g_at = global_ref.at[pl.ds(pl.multiple_of(go + done, SEG_ALIGN), size)]
        cp = pltpu.make_async_copy(l_at, g_at, sem) if to_global else pltpu.make_async_copy(g_at, l_at, sem)

        @pl.when(present)
        def _():
            if wait:
                cp.wait()
            else:
                cp.start()

        done = done + jnp.where(present, size, 0)


def _segment_copies(t, off_s, gs_s, l16_s, local_ref, global_ref, sem, to_global, wait):
    for e in range(N_EXPERTS):
        idx = t * N_EXPERTS + e
        _piece_copies(off_s[idx], gs_s[idx], l16_s[idx], local_ref, global_ref, sem, to_global, wait)


def _sort_kernel(off_s, gs_s, l16_s, tail_s, tail16_s, h_ref, comb_ref, offcol_ref, slots_ref, xs_ref,
                 hs_ref, zero_ref, sem):
    t = pl.program_id(0)
    slot = t % 2
    other = 1 - slot
    ts = SORT_TILE
    comb = comb_ref[...]
    asg = comb > 0.0
    row = lax.broadcasted_iota(jnp.int32, (ts, ts), 0)
    col = lax.broadcasted_iota(jnp.int32, (ts, ts), 1)
    before = jnp.where(row < col, 1.0, 0.0).astype(BF16)
    rank = jnp.dot(jnp.where(asg, 1.0, 0.0).astype(BF16), before, preferred_element_type=F32)
    pos = offcol_ref[0][:, 0:1] + rank
    p_lo = jnp.min(jnp.where(asg, pos, 1e9), axis=0, keepdims=True)
    p_hi = jnp.max(jnp.where(asg, pos, -1.0), axis=0, keepdims=True)
    w_lo = jnp.sum(jnp.where(jnp.logical_and(asg, pos == p_lo), comb, 0.0), axis=0, keepdims=True)
    w_hi = jnp.sum(jnp.where(jnp.logical_and(asg, pos == p_hi), comb, 0.0), axis=0, keepdims=True)
    w_hi = jnp.where(p_hi > p_lo, w_hi, 0.0)
    dest = lax.broadcasted_iota(jnp.int32, (SORT_CAP, ts), 0).astype(F32)
    onehot = jnp.where(jnp.logical_or(dest == p_lo, dest == p_hi), 1.0, 0.0).astype(BF16)
    hs_ref[slot] = jnp.dot(onehot, h_ref[...], preferred_element_type=F32).astype(BF16)
    slot_rows = jnp.concatenate([p_lo, p_hi, w_lo, w_hi, jnp.zeros((124, ts), F32)], axis=0)
    slots_ref[...] = slot_rows.T
    _segment_copies(t, off_s, gs_s, l16_s, hs_ref.at[slot], xs_ref, sem.at[slot], to_global=True, wait=False)

    @pl.when(t > 0)
    def _():
        _segment_copies(t - 1, off_s, gs_s, l16_s, hs_ref.at[other], xs_ref, sem.at[other], to_global=True, wait=True)

    @pl.when(t == pl.num_programs(0) - 1)
    def _():
        _segment_copies(t, off_s, gs_s, l16_s, hs_ref.at[slot], xs_ref, sem.at[slot], to_global=True, wait=True)
        zero_ref[...] = jnp.zeros_like(zero_ref)
        n_blocks = xs_ref.shape[0] // MOE_BLOCK

        def spare_block(b):
            rows = pl.ds(pl.multiple_of(b * MOE_BLOCK, MOE_BLOCK), MOE_BLOCK)
            return pltpu.make_async_copy(zero_ref, xs_ref.at[rows], sem.at[2])

        for wait in (False, True):
            for e in range(N_EXPERTS):
                _piece_copies(0, tail_s[e], tail16_s[e], zero_ref, xs_ref, sem.at[2], to_global=True, wait=wait,
                              repeat_local=True)

            @pl.loop(tail_s[N_EXPERTS], n_blocks)
            def _(b):
                if wait:
                    spare_block(b).wait()
                else:
                    spare_block(b).start()


def _sort_call(h2, comb_t, tables, tails, offcol, n_rows_sorted):
    rows, d = h2.shape
    nt = rows // SORT_TILE
    grid_spec = pltpu.PrefetchScalarGridSpec(
        num_scalar_prefetch=5,
        grid=(nt,),
        in_specs=[
            pl.BlockSpec((SORT_TILE, d), lambda i, *_: (i, 0)),
            pl.BlockSpec((N_EXPERTS, SORT_TILE), lambda i, *_: (0, i)),
            pl.BlockSpec((1, N_EXPERTS, 128), lambda i, *_: (i, 0, 0)),
        ],
        out_specs=[
            pl.BlockSpec((SORT_TILE, 128), lambda i, *_: (i, 0)),
            pl.BlockSpec(memory_space=pl.ANY),
        ],
        scratch_shapes=[pltpu.VMEM((2, SORT_CAP, d), BF16), pltpu.VMEM((MOE_BLOCK, d), BF16),
                        pltpu.SemaphoreType.DMA((3,))],
    )
    slots, xs = pl.pallas_call(
        _sort_kernel,
        grid_spec=grid_spec,
        out_shape=[jax.ShapeDtypeStruct((rows, 128), F32), jax.ShapeDtypeStruct((n_rows_sorted, d), BF16)],
        compiler_params=_cparams(("arbitrary",)),
        name="moe_sort",
    )(*tables, *tails, h2, comb_t, offcol)
    return slots, xs


def _experts_kernel(be_s, bv_s, x_ref, w1_ref, w3_ref, w2_ref, y_ref):
    b = pl.program_id(0)

    @pl.when(bv_s[b] != 0)
    def _():
        x = x_ref[...]
        a = jnp.dot(x, w1_ref[0, 0].astype(BF16), preferred_element_type=F32)
        g = jnp.dot(x, w3_ref[0, 0].astype(BF16), preferred_element_type=F32)
        hid = (_silu(a) * g).astype(BF16)
        y_ref[...] = jnp.dot(hid, w2_ref[0, 0].astype(BF16), preferred_element_type=F32).astype(BF16)

    @pl.when(bv_s[b] == 0)
    def _():
        y_ref[...] = jnp.zeros_like(y_ref)


def _experts_call(xs, blk_expert, blk_valid, w1, w3, w2, layer):
    rows, d = xs.shape
    nb = rows // MOE_BLOCK
    grid_spec = pltpu.PrefetchScalarGridSpec(
        num_scalar_prefetch=2,
        grid=(nb,),
        in_specs=[
            pl.BlockSpec((MOE_BLOCK, d), lambda b, be, bv: (b, 0)),
            pl.BlockSpec((1, 1, d, D_EXPERT), lambda b, be, bv: (layer, be[b], 0, 0)),
            pl.BlockSpec((1, 1, d, D_EXPERT), lambda b, be, bv: (layer, be[b], 0, 0)),
            pl.BlockSpec((1, 1, D_EXPERT, d), lambda b, be, bv: (layer, be[b], 0, 0)),
        ],
        out_specs=pl.BlockSpec((MOE_BLOCK, d), lambda b, be, bv: (b, 0)),
    )
    return pl.pallas_call(
        _experts_kernel,
        grid_spec=grid_spec,
        out_shape=jax.ShapeDtypeStruct((rows, d), BF16),
        compiler_params=_cparams(("arbitrary",)),
        name="moe_experts",
    )(blk_expert, blk_valid, xs, w1, w3, w2)


def _unsort_kernel(final, off_s, gs_s, l16_s, ys_ref, slots_ref, x_ref, fg_ref, g2a_ref, g2b_ref,
                   o_ref, yt_ref, sem):
    t = pl.program_id(0)
    ts = SORT_TILE
    slot = t % 2
    other = 1 - slot

    def fetch(tile, buf):
        yt_ref[buf] = jnp.zeros(yt_ref.shape[1:], yt_ref.dtype)
        _segment_copies(tile, off_s, gs_s, l16_s, yt_ref.at[buf], ys_ref, sem.at[buf], to_global=False, wait=False)

    @pl.when(t == 0)
    def _():
        fetch(t, slot)

    @pl.when(t + 1 < pl.num_programs(0))
    def _():
        fetch(t + 1, other)

    _segment_copies(t, off_s, gs_s, l16_s, yt_ref.at[slot], ys_ref, sem.at[slot], to_global=False, wait=True)
    slots = slots_ref[...]
    src = lax.broadcasted_iota(jnp.int32, (ts, SORT_CAP), 1).astype(F32)
    weights = jnp.where(src == slots[:, 0:1], slots[:, 2:3], 0.0) + jnp.where(src == slots[:, 1:2], slots[:, 3:4], 0.0)
    y = jnp.dot(weights.astype(BF16), yt_ref[slot], preferred_element_type=F32)
    for s, g2_ref in enumerate((g2a_ref, g2b_ref)):
        rws = slice(s * ROW_TILE, (s + 1) * ROW_TILE)
        xn = x_ref[rws, :] + g2_ref[0] * y[rws, :]
        if final:
            ms = jnp.mean(xn * xn, axis=-1, keepdims=True)
            xn = xn * lax.rsqrt(ms + EPS) * fg_ref[...]
        o_ref[rws, :] = xn


def _unsort_call(ys, slots, x, tables, final_g, mods, layer, mod_row, tile_off, final):
    rows, d = x.shape
    nt = rows // SORT_TILE
    sub = SORT_TILE // ROW_TILE
    g2_idx = (layer * 6 + 5) * COND_ROWS
    row_spec = pl.BlockSpec((SORT_TILE, d), lambda i, *_: (i, 0))

    def mod(s):
        return pl.BlockSpec((1, 1, d), lambda i, *_: (g2_idx + mod_row(i * sub + s + tile_off), 0, 0))

    grid_spec = pltpu.PrefetchScalarGridSpec(
        num_scalar_prefetch=3,
        grid=(nt,),
        in_specs=[
            pl.BlockSpec(memory_space=pl.ANY),
            pl.BlockSpec((SORT_TILE, 128), lambda i, *_: (i, 0)),
            row_spec,
            pl.BlockSpec((1, d), lambda i, *_: (0, 0)),
            mod(0), mod(1),
        ],
        out_specs=row_spec,
        scratch_shapes=[pltpu.VMEM((2, SORT_CAP, d), BF16), pltpu.SemaphoreType.DMA((2,))],
    )
    return pl.pallas_call(
        functools.partial(_unsort_kernel, final),
        grid_spec=grid_spec,
        out_shape=jax.ShapeDtypeStruct((rows, d), F32),
        compiler_params=_cparams(("arbitrary",)),
        name="moe_unsort_residual",
    )(*tables, ys, slots, x, final_g, mods, mods)


def _sum_before(a, axis):
    i = jnp.arange(a.shape[axis])
    mask = (i[None, :] < i[:, None]).astype(a.dtype)
    moved = jnp.moveaxis(a, axis, -1)
    return jnp.moveaxis(jnp.sum(moved[..., None, :] * mask, axis=-1), -1, axis)


def _moe_tables(cnt_blocks, n_blocks):
    cnt = cnt_blocks[:, :, 0].astype(jnp.int32)
    seg = (cnt + SEG_ALIGN - 1) // SEG_ALIGN * SEG_ALIGN
    off = _sum_before(seg, 1)
    blocks_e = (seg.sum(axis=0) + MOE_BLOCK - 1) // MOE_BLOCK
    first_block = _sum_before(blocks_e, 0)
    gstart = first_block[None, :] * MOE_BLOCK + _sum_before(seg, 0)
    blk = jnp.arange(n_blocks, dtype=jnp.int32)
    last_block = first_block + blocks_e
    blk_expert = jnp.minimum(jnp.sum((blk[:, None] >= last_block[None, :]).astype(jnp.int32), axis=1),
                             N_EXPERTS - 1)
    blk_valid = (blk < blocks_e.sum()).astype(jnp.int32)
    tables = (off.reshape(-1).astype(jnp.int32), gstart.reshape(-1).astype(jnp.int32),
              (seg // SEG_ALIGN).reshape(-1).astype(jnp.int32))
    offcol = jnp.broadcast_to(off.astype(F32)[:, :, None], off.shape + (128,))
    used = seg.sum(axis=0)
    tails = (jnp.concatenate([first_block * MOE_BLOCK + used, blocks_e.sum()[None]]).astype(jnp.int32),
             ((blocks_e * MOE_BLOCK - used) // SEG_ALIGN).astype(jnp.int32))
    return tables, tails, offcol, blk_expert.astype(jnp.int32), blk_valid


def _moe_call(h2, comb_t, cnt_blocks, x, w1, w3, w2, final_g, mods, layer, mod_row, tile_off, final):
    rows = x.shape[0]
    n_tiles = rows // SORT_TILE
    max_rows = 2 * rows + n_tiles * N_EXPERTS * (SEG_ALIGN - 1)
    n_blocks = max_rows // MOE_BLOCK + N_EXPERTS
    tables, tails, offcol, blk_expert, blk_valid = _moe_tables(cnt_blocks, n_blocks)
    slots, xs = _sort_call(h2, comb_t, tables, tails, offcol, n_blocks * MOE_BLOCK)
    ys = _experts_call(xs, blk_expert, blk_valid, w1, w3, w2, layer)
    return _unsort_call(ys, slots, x, tables, final_g, mods, layer, mod_row, tile_off, final)


def kernel(x, c, ctx, c_ctx, w_ada, b_ada, norm1_g, norm2_g, w_in, a_ln_g, a_w_s, a_b_s, b_conv_w, b_A_log, b_dt_bias, b_norm_g, w_proj_a, w_proj_b, w_out, w_router, b_router, w_e1, w_e3, w_e2, final_g):
    batch, seq_len, d = x.shape
    ctx_len = ctx.shape[1]
    n_layers = w_ada.shape[0]
    assert d == D_MODEL and batch + 1 <= COND_ROWS
    assert ctx_len % ROW_TILE == 0 and seq_len % ROW_TILE == 0
    assert (batch * ctx_len) % SORT_TILE == 0 and (batch * seq_len) % SORT_TILE == 0
    assert MIX_TILE == SORT_TILE and (batch * ctx_len) % MIX_TILE == 0 and seq_len % MIX_TILE == 0
    ctx_tiles = ctx_len // ROW_TILE
    lat_tiles = seq_len // ROW_TILE
    n_ctx_tiles = batch * ctx_tiles

    def mod_row(tile):
        return jnp.where(tile < n_ctx_tiles, batch, (tile - n_ctx_tiles) // lat_tiles)

    xs = (ctx.reshape(batch * ctx_len, d), x.reshape(batch * seq_len, d))
    cond = jnp.concatenate([c, c_ctx[None, :], jnp.zeros((COND_ROWS - batch - 1, d), F32)], axis=0)
    mods = _ada_call(cond, w_ada, b_ada).reshape(n_layers * 6 * COND_ROWS, 1, d)

    n_ab = 4 * N_HEADS
    ab_lo = 6 * d
    wr_t = w_router.T
    br = b_router.reshape(N_EXPERTS, 1)
    fg = final_g.reshape(1, d)
    for l in range(n_layers):
        w_bf = w_in[l].astype(BF16)
        w_all = jnp.concatenate([w_bf[:, :ab_lo], w_bf[:, ab_lo + n_ab:], w_bf[:, ab_lo:ab_lo + n_ab],
                                 jnp.zeros((d, 128 - n_ab), BF16)], axis=1)
        conv_w = jnp.pad(b_conv_w[l], ((0, 8 - CONV_TAPS), (0, 0)))
        zeros8 = jnp.zeros((N_HEADS,), F32)
        alog_row = jnp.pad(jnp.concatenate([b_A_log[l, 0], zeros8, b_A_log[l, 1], zeros8]), (0, 128 - n_ab)).reshape(1, 128)
        dtb_row = jnp.pad(jnp.concatenate([b_dt_bias[l, 0], zeros8, b_dt_bias[l, 1], zeros8]), (0, 128 - n_ab)).reshape(1, 128)
        u, v, z, ga, gb, qn, kn, vs, gates = _in_call(
            xs, mods, l, norm1_g[l].reshape(1, d), w_all, conv_w, alog_row, dtb_row, mod_row,
            n_ctx_tiles, ctx_tiles, lat_tiles)
        o_f, o_b = _gdn_call(qn, kn, vs, gates, batch, ctx_len, seq_len)

        bs = jnp.repeat(a_b_s[l].T, SGU_CHUNK, axis=1)
        bng = b_norm_g[l].reshape(1, HEAD_DIM)
        last = l == n_layers - 1
        tile_off = n_ctx_tiles if last else 0
        x_new, h2, comb_t, cnt = _mix_call(
            xs, u, v, o_f, o_b, z, ga, gb, mods, l, norm2_g[l].reshape(1, d), a_ln_g[l].reshape(1, d),
            a_w_s[l].astype(BF16), bs, bng,
            w_proj_a[l].astype(BF16), w_proj_b[l].astype(BF16), w_out[l].astype(BF16), wr_t, br, mod_row,
            batch * ctx_len // MIX_TILE, tile_off * ROW_TILE // MIX_TILE)
        xs = (_moe_call(h2, comb_t, cnt, x_new, w_e1, w_e3, w_e2,
                        fg, mods, l, mod_row, tile_off, final=last),)
    return xs[0].reshape(batch, seq_len, d)
```

```python
import functools

import jax
import jax.numpy as jnp
from jax import lax
from jax.experimental import pallas as pl
from jax.experimental.pallas import tpu as pltpu

F32 = jnp.float32
BF16 = jnp.bfloat16
HIGHEST = lax.Precision.HIGHEST

EPS = 1e-6
D_MODEL = 1024
N_HEADS = 8
HEAD_DIM = 128
SGU_CHUNK = 128
SGU_GROUPS = 8
CONV_TAPS = 5
N_EXPERTS = 16
EXPERTS_PER_GROUP = 4
N_EXPERT_GROUPS = 4
D_EXPERT = 512
ROW_TILE = 256
MIX_TILE = 512
GDN_CHUNK = 128
GDN_BLOCK = 2
SORT_TILE = 512
SEG_ALIGN = 16
SORT_CAP = 1280
MOE_BLOCK = 512
SEG_BITS = (32, 16, 8, 4, 2, 1)
COND_ROWS = 8
HALO = 8
VMEM_LIMIT = 56 * 1024 * 1024


def _cparams(sem):
    return pltpu.CompilerParams(dimension_semantics=sem, vmem_limit_bytes=VMEM_LIMIT)


def _sigmoid(x):
    return 0.5 * jnp.tanh(0.5 * x) + 0.5


def _silu(x):
    return x * _sigmoid(x)


def _gelu_tanh(x):
    return 0.5 * x * (1.0 + jnp.tanh(0.7978845608028654 * (x + 0.044715 * (x * x * x))))


def _softplus(x):
    return jnp.maximum(x, 0.0) + jnp.log1p(jnp.exp(-jnp.abs(x)))


def _ada_kernel(cond_ref, w_ref, b_ref, o_ref):
    s = _silu(cond_ref[...])
    o_ref[0, 0] = jnp.dot(s, w_ref[0], precision=HIGHEST, preferred_element_type=F32) + b_ref[0, 0]


def _ada_call(cond, w_ada, b_ada):
    n_layers = w_ada.shape[0]
    d = D_MODEL
    return pl.pallas_call(
        _ada_kernel,
        grid=(n_layers, 6),
        in_specs=[
            pl.BlockSpec((COND_ROWS, d), lambda l, j: (0, 0)),
            pl.BlockSpec((1, d, d), lambda l, j: (l, 0, j)),
            pl.BlockSpec((1, 1, 1, d), lambda l, j: (l, j, 0, 0)),
        ],
        out_specs=pl.BlockSpec((1, 1, COND_ROWS, d), lambda l, j: (l, j, 0, 0)),
        out_shape=jax.ShapeDtypeStruct((n_layers, 6, COND_ROWS, d), F32),
        compiler_params=_cparams(("arbitrary", "arbitrary")),
        name="ada_params",
    )(cond, w_ada, b_ada.reshape(n_layers, 6, 1, d))


def _load_rows(x_refs, tile, n_ctx_tiles):
    if len(x_refs) == 1:
        return x_refs[0][...]
    return jnp.where(tile < n_ctx_tiles, x_refs[0][...], x_refs[1][...])


def _row_specs(xs, n_ctx_tiles, tile_off, d, tile=ROW_TILE):
    if len(xs) == 1:
        return [pl.BlockSpec((tile, d), lambda i: (i + tile_off, 0))]
    return [pl.BlockSpec((tile, d), lambda i: (jnp.minimum(i + tile_off, n_ctx_tiles - 1), 0)),
            pl.BlockSpec((tile, d), lambda i: (jnp.maximum(i + tile_off - n_ctx_tiles, 0), 0))]


def _halo_specs(xs, n_ctx_tiles, d, nxt):
    per = ROW_TILE // HALO

    def spec(n_blocks, first_tile):
        if nxt:
            return pl.BlockSpec((HALO, d), lambda i: (jnp.clip((i - first_tile + 1) * per, 0, n_blocks - 1), 0))
        return pl.BlockSpec((HALO, d), lambda i: (jnp.clip((i - first_tile) * per - 1, 0, n_blocks - 1), 0))

    if len(xs) == 1:
        return [spec(xs[0].shape[0] // HALO, 0)]
    return [spec(xs[0].shape[0] // HALO, 0), spec(xs[1].shape[0] // HALO, n_ctx_tiles)]


def _in_kernel(n_src, n_ctx_tiles, ctx_tiles, lat_tiles, *refs):
    x_refs, xp_refs, xn_refs = refs[:n_src], refs[n_src:2 * n_src], refs[2 * n_src:3 * n_src]
    (sh_ref, sc_ref, g_ref, w_ref, cw_ref, alog_ref, dtb_ref,
     u_ref, v_ref, z_ref, ga_ref, gb_ref, qo_ref, ko_ref, vo_ref, gate_ref, ext_ref) = refs[3 * n_src:]
    i = pl.program_id(0)
    d = D_MODEL
    j = jnp.where(i < n_ctx_tiles, i % ctx_tiles, (i - n_ctx_tiles) % lat_tiles)
    last = jnp.where(i < n_ctx_tiles, ctx_tiles - 1, lat_tiles - 1)
    has_prev = (j != 0).astype(F32)
    has_next = (j != last).astype(F32)

    x = jnp.concatenate([_load_rows(xp_refs, i, n_ctx_tiles), _load_rows(x_refs, i, n_ctx_tiles),
                         _load_rows(xn_refs, i, n_ctx_tiles)], axis=0)
    ms = jnp.mean(x * x, axis=-1, keepdims=True)
    h = x * lax.rsqrt(ms + EPS) * g_ref[...]
    h = h * (1.0 + sc_ref[0]) + sh_ref[0]
    hb_ext = h.astype(BF16)
    hb = h[HALO:HALO + ROW_TILE].astype(BF16)
    pad = CONV_TAPS // 2
    lo, hi = HALO, HALO + ROW_TILE
    step = 2 * HEAD_DIM

    def project_qkv(t, c0):
        cols = slice((2 + t) * d + c0, (2 + t) * d + c0 + step)
        proj = jnp.dot(hb_ext, w_ref[0, :, cols], preferred_element_type=F32)
        ext_ref[t, 0:lo, c0:c0 + step] = proj[0:lo] * has_prev
        ext_ref[t, lo:hi, c0:c0 + step] = proj[lo:hi]
        ext_ref[t, hi:hi + HALO, c0:c0 + step] = proj[hi:hi + HALO] * has_next
        return proj[hi + HALO - 1:hi + HALO, 0:HEAD_DIM]

    def conv_head(t, hd, o_ref, l2, scale, anchor):
        lanes = slice(hd * HEAD_DIM, (hd + 1) * HEAD_DIM)
        acc = None
        for tap in range(CONV_TAPS):
            w_row = cw_ref[tap:tap + 1, t * d + hd * HEAD_DIM:t * d + (hd + 1) * HEAD_DIM]
            if tap == 0 and anchor is not None:
                w_row = w_row + jnp.where(i < 0, anchor, 0.0)
            start = lo - pad + tap
            term = ext_ref[t, start:start + ROW_TILE, lanes] * w_row
            acc = term if acc is None else acc + term
        y = _silu(acc)
        if l2:
            ss = jnp.sum(y * y, axis=-1, keepdims=True)
            y = y * (lax.rsqrt(ss + EPS) * scale)
        o_ref[:, lanes] = y.astype(BF16)

    passes = [functools.partial(project_qkv, t, c0) for t in range(3) for c0 in range(0, d, step)]

    def project_other(n, o_ref, c0):
        res = jnp.dot(hb, w_ref[0, :, n * d + c0:n * d + c0 + step], preferred_element_type=F32)
        o_ref[:, c0:c0 + step] = res.astype(BF16)
        return res[ROW_TILE - 1:ROW_TILE, 0:HEAD_DIM]

    passes += [functools.partial(project_other, n, o_ref, c0)
               for n, o_ref in ((0, u_ref), (1, v_ref), (5, z_ref), (6, ga_ref), (7, gb_ref))
               for c0 in range(0, d, step)]
    n_front = d // step
    for p in passes[:n_front]:
        p()
    rest = passes[n_front:]
    convs = [(t, hd) for t in range(3) for hd in range(N_HEADS)]
    n_rest = len(rest)
    conv_args = ((qo_ref, True, HEAD_DIM ** -0.5), (ko_ref, True, 1.0), (vo_ref, False, 1.0))
    anchor = None
    for j, (t, hd) in enumerate(convs):
        for _ in range((j + 1) * n_rest // len(convs) - j * n_rest // len(convs)):
            anchor = rest.pop(0)()
        conv_head(t, hd, *conv_args[t], anchor)
    for p in rest:
        p()
    ab = jnp.dot(hb, w_ref[0, :, 8 * d:8 * d + 128], preferred_element_type=F32)
    lane = lax.broadcasted_iota(jnp.int32, ab.shape, 1)
    is_decay = ((lane // N_HEADS) % 2) == 0
    g = -jnp.exp(alog_ref[...]) * _softplus(ab + dtb_ref[...])
    gate = jnp.where(is_decay, g, _sigmoid(ab))
    gate = jnp.where(lane < 4 * N_HEADS, gate, 0.0)
    gate_ref[0] = gate
    gate_ref[1] = pltpu.roll(gate, 128 - 2 * N_HEADS, axis=1)


def _in_call(xs, mods, layer, norm_g, w_all, conv_w, alog_row, dtb_row, mod_row, n_ctx_tiles, ctx_tiles, lat_tiles):
    d = D_MODEL
    rows = sum(a.shape[0] for a in xs)
    nt = rows // ROW_TILE
    sh_idx = (layer * 6 + 0) * COND_ROWS
    sc_idx = (layer * 6 + 1) * COND_ROWS
    row_spec = pl.BlockSpec((ROW_TILE, d), lambda i: (i, 0))
    small = pl.BlockSpec((1, 128), lambda i: (0, 0))
    return pl.pallas_call(
        functools.partial(_in_kernel, len(xs), n_ctx_tiles, ctx_tiles, lat_tiles),
        grid=(nt,),
        in_specs=(_row_specs(xs, n_ctx_tiles, 0, d) + _halo_specs(xs, n_ctx_tiles, d, False)
                  + _halo_specs(xs, n_ctx_tiles, d, True) + [
            pl.BlockSpec((1, 1, d), lambda i: (sh_idx + mod_row(i), 0, 0)),
            pl.BlockSpec((1, 1, d), lambda i: (sc_idx + mod_row(i), 0, 0)),
            pl.BlockSpec((1, d), lambda i: (0, 0)),
            pl.BlockSpec((1, d, 8 * d + 128), lambda i: (layer, 0, 0), pipeline_mode=pl.Buffered(1)),
            pl.BlockSpec((8, 3 * d), lambda i: (0, 0)),
            small, small,
        ]),
        out_specs=[row_spec] * 8 + [pl.BlockSpec((2, ROW_TILE, 128), lambda i: (0, i, 0))],
        out_shape=[jax.ShapeDtypeStruct((rows, d), BF16)] * 8 + [jax.ShapeDtypeStruct((2, rows, 128), F32)],
        scratch_shapes=[pltpu.VMEM((3, ROW_TILE + 2 * HALO, d), F32)],
        compiler_params=_cparams(("arbitrary",)),
        name="norm_in_proj_conv",
    )(*xs, *xs, *xs, mods, mods, norm_g, w_all, conv_w, alog_row, dtb_row)


def _gdn_kernel(qf_ref, kf_ref, vf_ref, gf_ref, qb_ref, kb_ref, vb_ref, gb_ref, of_ref, ob_ref, s_ref):
    @pl.when(pl.program_id(1) == 0)
    def _():
        s_ref[...] = jnp.zeros_like(s_ref)

    for n in range(GDN_BLOCK):
        rows_f = slice(n * GDN_CHUNK, (n + 1) * GDN_CHUNK)
        rows_b = slice((GDN_BLOCK - 1 - n) * GDN_CHUNK, (GDN_BLOCK - n) * GDN_CHUNK)
        _gdn_chunk(((True, rows_f, qf_ref, kf_ref, vf_ref, gf_ref, of_ref),
                    (False, rows_b, qb_ref, kb_ref, vb_ref, gb_ref, ob_ref)), s_ref)


def _gdn_chunk(scans, s_ref):
    c = GDN_CHUNK
    row = lax.broadcasted_iota(jnp.int32, (c, c), 0)
    col = lax.broadcasted_iota(jnp.int32, (c, c), 1)
    eye = (row == col).astype(F32)
    nt_dims = (((1,), (1,)), ((), ()))
    lanes = [slice(h * HEAD_DIM, (h + 1) * HEAD_DIM) for h in range(N_HEADS)]

    q, k, v, gc_col, gc_row, gtot_row, beta_row, incl, strict, dst = [], [], [], [], [], [], [], [], [], []
    for d_idx, (fwd, rows, q_ref, k_ref, v_ref, gate_ref, o_ref) in enumerate(scans):
        incl_d = (row >= col) if fwd else (row <= col)
        strict_d = (row > col) if fwd else (row < col)
        gate = gate_ref[0, rows, :]
        gc_all = jnp.dot(incl_d.astype(F32), gate, precision=HIGHEST, preferred_element_type=F32)
        gtot_all = jnp.dot(jnp.ones((c, c), F32), gate, precision=HIGHEST, preferred_element_type=F32)
        gate_t = gate.T
        gc_t = gc_all.T
        gtot_t = gtot_all.T
        for h in range(N_HEADS):
            q.append(q_ref[rows, lanes[h]])
            k.append(k_ref[rows, lanes[h]])
            v.append(v_ref[rows, lanes[h]])
            gc_col.append(jnp.broadcast_to(gc_all[:, h:h + 1], (c, c)))
            gc_row.append(gc_t[h:h + 1, :])
            gtot_row.append(gtot_t[h:h + 1, :])
            beta_row.append(gate_t[N_HEADS + h:N_HEADS + h + 1, :])
            incl.append(incl_d)
            strict.append(strict_d)
            dst.append((o_ref, rows, lanes[h], d_idx, h))
    chains = range(len(q))

    decay = [jnp.exp(jnp.where(incl[i], gc_col[i] - gc_row[i], -jnp.inf)) * beta_row[i] for i in chains]
    kq = [lax.dot_general(jnp.concatenate([k[i], q[i]], axis=0), k[i], nt_dims, preferred_element_type=F32)
          for i in chains]
    n_mat = [jnp.where(strict[i], kq[i][:c] * decay[i], 0.0) for i in chains]
    a_mat = [(kq[i][c:] * decay[i]).astype(BF16) for i in chains]
    a_b = [(eye + n_mat[i]).astype(BF16) for i in chains]
    t_inv = [eye - n_mat[i] for i in chains]
    for _ in range(6):
        tb = [t_inv[i].astype(BF16) for i in chains]
        err = [(eye - jnp.dot(a_b[i], tb[i], preferred_element_type=F32)).astype(BF16) for i in chains]
        t_inv = [t_inv[i] + jnp.dot(tb[i], err[i], preferred_element_type=F32) for i in chains]
    u_val = [jnp.dot(t_inv[i].astype(BF16), v[i], preferred_element_type=F32) for i in chains]
    w_key = [jnp.dot((t_inv[i] * jnp.exp(gc_row[i])).astype(BF16), k[i], preferred_element_type=F32)
             for i in chains]

    s_old = [s_ref[dst[i][3], dst[i][4]] for i in chains]
    sb = [s_old[i].astype(BF16) for i in chains]
    wq_s = [jnp.dot(jnp.concatenate([w_key[i].astype(BF16), q[i]], axis=0), sb[i], preferred_element_type=F32)
            for i in chains]
    v_new = [(u_val[i] - wq_s[i][:c]).astype(BF16) for i in chains]
    kd_t = [(k[i].astype(F32).T * (jnp.exp(gtot_row[i] - gc_row[i]) * beta_row[i])).astype(BF16) for i in chains]
    ak_v = [jnp.dot(jnp.concatenate([a_mat[i], kd_t[i]], axis=0), v_new[i], preferred_element_type=F32)
            for i in chains]
    for i in chains:
        o_ref, rows, ln, d_idx, h = dst[i]
        s_ref[d_idx, h] = s_old[i] * jnp.exp(gtot_row[i]) + ak_v[i][c:]
        o_ref[rows, ln] = (jnp.exp(gc_col[i]) * wq_s[i][c:] + ak_v[i][:c]).astype(BF16)


def _gdn_call(q, k, v, gates, batch, ctx_len, seq_len):
    rows, d = q.shape
    c = GDN_BLOCK * GDN_CHUNK
    assert ctx_len % c == 0 and seq_len % c == 0
    n_ctx = ctx_len // c
    n_lat = seq_len // c
    n_steps = n_ctx + n_lat

    def chunk(b, fwd, s):
        pos = s if fwd else jnp.where(s < n_ctx, n_ctx - 1 - s, n_ctx + n_steps - 1 - s)
        return jnp.where(pos < n_ctx, b * n_ctx + pos, batch * n_ctx + b * n_lat + pos - n_ctx)

    def specs(fwd):
        blk = pl.BlockSpec((c, d), lambda b, s: (chunk(b, fwd, s), 0))
        return [blk, blk, blk, pl.BlockSpec((1, c, 128), lambda b, s: (0 if fwd else 1, chunk(b, fwd, s), 0))]

    return pl.pallas_call(
        _gdn_kernel,
        grid=(batch, n_steps),
        in_specs=specs(True) + specs(False),
        out_specs=[pl.BlockSpec((c, d), lambda b, s: (chunk(b, True, s), 0)),
                   pl.BlockSpec((c, d), lambda b, s: (chunk(b, False, s), 0))],
        out_shape=[jax.ShapeDtypeStruct((rows, d), BF16)] * 2,
        scratch_shapes=[pltpu.VMEM((2, N_HEADS, HEAD_DIM, HEAD_DIM), F32)],
        compiler_params=_cparams(("arbitrary", "arbitrary")),
        name="gated_delta",
    )(q, k, v, gates, q, k, v, gates)


def _mix_kernel(n_src, n_ctx_tiles, tile_off, *refs):
    x_refs = refs[:n_src]
    (u_ref, v_ref, of_ref, ob_ref, z_ref, ga_ref, gb_ref,
     g1_ref, sh2_ref, sc2_ref, n2g_ref, lng_ref, ws_ref, bs_ref, bng_ref,
     wpa_ref, wpb_ref, wout_ref, wr_ref, br_ref,
     xo_ref, h2_ref, comb_ref, cnt_ref, sa_ref, sb_ref) = refs[n_src:]
    d = D_MODEL
    ug = _gelu_tanh(u_ref[...].astype(F32))
    vg = _gelu_tanh(v_ref[...].astype(F32))
    mu = jnp.mean(vg, axis=-1, keepdims=True)
    vc = vg - mu
    var = jnp.mean(vc * vc, axis=-1, keepdims=True)
    vn = (vc * lax.rsqrt(var + EPS) * lng_ref[...]).astype(BF16)
    for ch in range(MIX_TILE // SGU_CHUNK):
        rws = slice(ch * SGU_CHUNK, (ch + 1) * SGU_CHUNK)
        for g in range(SGU_GROUPS):
            lanes = slice(g * 128, (g + 1) * 128)
            mixed = jnp.dot(ws_ref[g], vn[rws, lanes], preferred_element_type=F32) + bs_ref[:, lanes]
            sa_ref[rws, lanes] = (ug[rws, lanes] * mixed).astype(BF16)
    y_a = jnp.dot(sa_ref[...], wpa_ref[...], preferred_element_type=F32)
    for h in range(N_HEADS):
        lanes = slice(h * HEAD_DIM, (h + 1) * HEAD_DIM)
        o = of_ref[:, lanes].astype(F32) + ob_ref[:, lanes].astype(F32)
        ms = jnp.mean(o * o, axis=-1, keepdims=True)
        o = o * lax.rsqrt(ms + EPS) * bng_ref[...]
        sb_ref[:, lanes] = (o * _silu(z_ref[:, lanes].astype(F32))).astype(BF16)
    y_b = jnp.dot(sb_ref[...], wpb_ref[...], preferred_element_type=F32)
    merged = _sigmoid(ga_ref[...].astype(F32)) * y_a + _sigmoid(gb_ref[...].astype(F32)) * y_b
    y = jnp.dot(merged.astype(BF16), wout_ref[...], preferred_element_type=F32)
    xn = _load_rows(x_refs, pl.program_id(0) + tile_off, n_ctx_tiles) + g1_ref[0] * y
    xo_ref[...] = xn
    ms = jnp.mean(xn * xn, axis=-1, keepdims=True)
    h2 = xn * lax.rsqrt(ms + EPS) * n2g_ref[...]
    h2 = h2 * (1.0 + sc2_ref[0]) + sh2_ref[0]
    h2_ref[...] = h2.astype(BF16)
    logits = lax.dot_general(wr_ref[...], h2, (((1,), (1,)), ((), ())),
                             precision=HIGHEST, preferred_element_type=F32)
    scores = _sigmoid(logits)
    sel = scores + br_ref[...]
    srow = [sel[e:e + 1, :] for e in range(N_EXPERTS)]
    grp = []
    for g in range(N_EXPERT_GROUPS):
        m = srow[4 * g:4 * g + 4]
        best2 = None
        for a in range(4):
            for b in range(a + 1, 4):
                pair = m[a] + m[b]
                best2 = pair if best2 is None else jnp.maximum(best2, pair)
        grp.append(best2)
    best_val = grp[0]
    best_idx = jnp.zeros_like(best_val, dtype=jnp.int32)
    for g in range(1, N_EXPERT_GROUPS):
        better = grp[g] > best_val
        best_val = jnp.where(better, grp[g], best_val)
        best_idx = jnp.where(better, g, best_idx)
    picked = []
    for e in range(N_EXPERTS):
        g = e // EXPERTS_PER_GROUP
        rank = jnp.zeros_like(best_idx)
        for o_e in range(4 * g, 4 * g + 4):
            if o_e == e:
                continue
            ahead = (srow[o_e] > srow[e]) if o_e > e else (srow[o_e] >= srow[e])
            rank = rank + ahead.astype(jnp.int32)
        chosen = jnp.logical_and(best_idx == g, rank < 2)
        picked.append(jnp.where(chosen, scores[e:e + 1, :], 0.0))
    total = picked[0]
    for e in range(1, N_EXPERTS):
        total = total + picked[e]
    inv = 1.0 / total
    for e in range(N_EXPERTS):
        w_e = picked[e] * inv
        comb_ref[e:e + 1, :] = w_e
        n_e = jnp.sum(jnp.where(w_e > 0.0, 1.0, 0.0), axis=-1, keepdims=True)
        cnt_ref[0, e:e + 1, :] = jnp.broadcast_to(n_e, (1, 128))


def _mix_call(xs, u, v, o_f, o_b, z, ga, gb, mods, layer, norm2_g, ln_g, ws, bs, bng,
              wpa, wpb, wout, wr_t, br, mod_row, n_ctx_tiles, tile_off):
    d = D_MODEL
    nt = sum(a.shape[0] for a in xs) // MIX_TILE - tile_off
    rows = nt * MIX_TILE
    per = MIX_TILE // ROW_TILE
    g1_idx = (layer * 6 + 2) * COND_ROWS
    sh2_idx = (layer * 6 + 3) * COND_ROWS
    sc2_idx = (layer * 6 + 4) * COND_ROWS
    row_spec = pl.BlockSpec((MIX_TILE, d), lambda i: (i + tile_off, 0))
    out_spec = pl.BlockSpec((MIX_TILE, d), lambda i: (i, 0))
    vec = pl.BlockSpec((1, d), lambda i: (0, 0))
    wspec = pl.BlockSpec((d, d), lambda i: (0, 0))

    def mod(idx):
        return pl.BlockSpec((1, 1, d), lambda i: (idx + mod_row((i + tile_off) * per), 0, 0))

    return pl.pallas_call(
        functools.partial(_mix_kernel, len(xs), n_ctx_tiles, tile_off),
        grid=(nt,),
        in_specs=_row_specs(xs, n_ctx_tiles, tile_off, d, MIX_TILE) + [
            row_spec, row_spec,
            row_spec, row_spec,
            row_spec, row_spec, row_spec,
            mod(g1_idx), mod(sh2_idx), mod(sc2_idx),
            vec, vec,
            pl.BlockSpec((SGU_GROUPS, SGU_CHUNK, SGU_CHUNK), lambda i: (0, 0, 0)),
            pl.BlockSpec((SGU_CHUNK, d), lambda i: (0, 0)),
            pl.BlockSpec((1, HEAD_DIM), lambda i: (0, 0)),
            wspec, wspec, wspec,
            pl.BlockSpec((N_EXPERTS, d), lambda i: (0, 0)),
            pl.BlockSpec((N_EXPERTS, 1), lambda i: (0, 0)),
        ],
        out_specs=[out_spec, out_spec, pl.BlockSpec((N_EXPERTS, MIX_TILE), lambda i: (0, i)),
                   pl.BlockSpec((1, N_EXPERTS, 128), lambda i: (i, 0, 0))],
        out_shape=[jax.ShapeDtypeStruct((rows, d), F32), jax.ShapeDtypeStruct((rows, d), BF16),
                   jax.ShapeDtypeStruct((N_EXPERTS, rows), F32),
                   jax.ShapeDtypeStruct((nt, N_EXPERTS, 128), F32)],
        scratch_shapes=[pltpu.VMEM((MIX_TILE, d), BF16), pltpu.VMEM((MIX_TILE, d), BF16)],
        compiler_params=_cparams(("arbitrary",)),
        name="mix_merge_router",
    )(*xs, u, v, o_f, o_b, z, ga, gb, mods, mods, mods, norm2_g, ln_g, ws, bs, bng,
      wpa, wpb, wout, wr_t, br)


def _piece_copies(lo, go, n16, local_ref, global_ref, sem, to_global, wait, repeat_local=False):
    done = jnp.int32(0)
    for bit in SEG_BITS:
        size = bit * SEG_ALIGN
        present = (n16 & bit) != 0
        l_start = lo if repeat_local else lo + done
        l_at = local_ref.at[pl.ds(pl.multiple_of(l_start, SEG_ALIGN), size)]
        g_at = global_ref.at[pl.ds(pl.multiple_of(go + done, SEG_ALIGN), size)]
        cp = pltpu.make_async_copy(l_at, g_at, sem) if to_global else pltpu.make_async_copy(g_at, l_at, sem)

        @pl.when(present)
        def _():
            if wait:
                cp.wait()
            else:
                cp.start()

        done = done + jnp.where(present, size, 0)


def _segment_copies(t, off_s, gs_s, l16_s, local_ref, global_ref, sem, to_global, wait):
    for e in range(N_EXPERTS):
        idx = t * N_EXPERTS + e
        _piece_copies(off_s[idx], gs_s[idx], l16_s[idx], local_ref, global_ref, sem, to_global, wait)


def _sort_kernel(off_s, gs_s, l16_s, tail_s, tail16_s, h_ref, comb_ref, offcol_ref, slots_ref, xs_ref,
                 hs_ref, zero_ref, sem):
    t = pl.program_id(0)
    slot = t % 2
    other = 1 - slot
    ts = SORT_TILE
    comb = comb_ref[...]
    asg = comb > 0.0
    row = lax.broadcasted_iota(jnp.int32, (ts, ts), 0)
    col = lax.broadcasted_iota(jnp.int32, (ts, ts), 1)
    before = jnp.where(row < col, 1.0, 0.0).astype(BF16)
    rank = jnp.dot(jnp.where(asg, 1.0, 0.0).astype(BF16), before, preferred_element_type=F32)
    pos = offcol_ref[0][:, 0:1] + rank
    p_lo = jnp.min(jnp.where(asg, pos, 1e9), axis=0, keepdims=True)
    p_hi = jnp.max(jnp.where(asg, pos, -1.0), axis=0, keepdims=True)
    w_lo = jnp.sum(jnp.where(jnp.logical_and(asg, pos == p_lo), comb, 0.0), axis=0, keepdims=True)
    w_hi = jnp.sum(jnp.where(jnp.logical_and(asg, pos == p_hi), comb, 0.0), axis=0, keepdims=True)
    w_hi = jnp.where(p_hi > p_lo, w_hi, 0.0)
    dest = lax.broadcasted_iota(jnp.int32, (SORT_CAP, ts), 0).astype(F32)
    onehot = jnp.where(jnp.logical_or(dest == p_lo, dest == p_hi), 1.0, 0.0).astype(BF16)
    hs_ref[slot] = jnp.dot(onehot, h_ref[...], preferred_element_type=F32).astype(BF16)
    slot_rows = jnp.concatenate([p_lo, p_hi, w_lo, w_hi, jnp.zeros((124, ts), F32)], axis=0)
    slots_ref[...] = slot_rows.T
    _segment_copies(t, off_s, gs_s, l16_s, hs_ref.at[slot], xs_ref, sem.at[slot], to_global=True, wait=False)

    @pl.when(t > 0)
    def _():
        _segment_copies(t - 1, off_s, gs_s, l16_s, hs_ref.at[other], xs_ref, sem.at[other], to_global=True, wait=True)

    @pl.when(t == pl.num_programs(0) - 1)
    def _():
        _segment_copies(t, off_s, gs_s, l16_s, hs_ref.at[slot], xs_ref, sem.at[slot], to_global=True, wait=True)
        zero_ref[...] = jnp.zeros_like(zero_ref)
        n_blocks = xs_ref.shape[0] // MOE_BLOCK

        def spare_block(b):
            rows = pl.ds(pl.multiple_of(b * MOE_BLOCK, MOE_BLOCK), MOE_BLOCK)
            return pltpu.make_async_copy(zero_ref, xs_ref.at[rows], sem.at[2])

        for wait in (False, True):
            for e in range(N_EXPERTS):
                _piece_copies(0, tail_s[e], tail16_s[e], zero_ref, xs_ref, sem.at[2], to_global=True, wait=wait,
                              repeat_local=True)

            @pl.loop(tail_s[N_EXPERTS], n_blocks)
            def _(b):
                if wait:
                    spare_block(b).wait()
                else:
                    spare_block(b).start()


def _sort_call(h2, comb_t, tables, tails, offcol, n_rows_sorted):
    rows, d = h2.shape
    nt = rows // SORT_TILE
    grid_spec = pltpu.PrefetchScalarGridSpec(
        num_scalar_prefetch=5,
        grid=(nt,),
        in_specs=[
            pl.BlockSpec((SORT_TILE, d), lambda i, *_: (i, 0)),
            pl.BlockSpec((N_EXPERTS, SORT_TILE), lambda i, *_: (0, i)),
            pl.BlockSpec((1, N_EXPERTS, 128), lambda i, *_: (i, 0, 0)),
        ],
        out_specs=[
            pl.BlockSpec((SORT_TILE, 128), lambda i, *_: (i, 0)),
            pl.BlockSpec(memory_space=pl.ANY),
        ],
        scratch_shapes=[pltpu.VMEM((2, SORT_CAP, d), BF16), pltpu.VMEM((MOE_BLOCK, d), BF16),
                        pltpu.SemaphoreType.DMA((3,))],
    )
    slots, xs = pl.pallas_call(
        _sort_kernel,
        grid_spec=grid_spec,
        out_shape=[jax.ShapeDtypeStruct((rows, 128), F32), jax.ShapeDtypeStruct((n_rows_sorted, d), BF16)],
        compiler_params=_cparams(("arbitrary",)),
        name="moe_sort",
    )(*tables, *tails, h2, comb_t, offcol)
    return slots, xs


def _experts_kernel(be_s, bv_s, x_ref, w1_ref, w3_ref, w2_ref, y_ref):
    b = pl.program_id(0)

    @pl.when(bv_s[b] != 0)
    def _():
        x = x_ref[...]
        a = jnp.dot(x, w1_ref[0, 0].astype(BF16), preferred_element_type=F32)
        g = jnp.dot(x, w3_ref[0, 0].astype(BF16), preferred_element_type=F32)
        hid = (_silu(a) * g).astype(BF16)
        y_ref[...] = jnp.dot(hid, w2_ref[0, 0].astype(BF16), preferred_element_type=F32).astype(BF16)

    @pl.when(bv_s[b] == 0)
    def _():
        y_ref[...] = jnp.zeros_like(y_ref)


def _experts_call(xs, blk_expert, blk_valid, w1, w3, w2, layer):
    rows, d = xs.shape
    nb = rows // MOE_BLOCK
    grid_spec = pltpu.PrefetchScalarGridSpec(
        num_scalar_prefetch=2,
        grid=(nb,),
        in_specs=[
            pl.BlockSpec((MOE_BLOCK, d), lambda b, be, bv: (b, 0)),
            pl.BlockSpec((1, 1, d, D_EXPERT), lambda b, be, bv: (layer, be[b], 0, 0)),
            pl.BlockSpec((1, 1, d, D_EXPERT), lambda b, be, bv: (layer, be[b], 0, 0)),
            pl.BlockSpec((1, 1, D_EXPERT, d), lambda b, be, bv: (layer, be[b], 0, 0)),
        ],
        out_specs=pl.BlockSpec((MOE_BLOCK, d), lambda b, be, bv: (b, 0)),
    )
    return pl.pallas_call(
        _experts_kernel,
        grid_spec=grid_spec,
        out_shape=jax.ShapeDtypeStruct((rows, d), BF16),
        compiler_params=_cparams(("arbitrary",)),
        name="moe_experts",
    )(blk_expert, blk_valid, xs, w1, w3, w2)


def _unsort_kernel(final, off_s, gs_s, l16_s, ys_ref, slots_ref, x_ref, fg_ref, g2a_ref, g2b_ref,
                   o_ref, yt_ref, sem):
    t = pl.program_id(0)
    ts = SORT_TILE
    slot = t % 2
    other = 1 - slot

    def fetch(tile, buf):
        yt_ref[buf] = jnp.zeros(yt_ref.shape[1:], yt_ref.dtype)
        _segment_copies(tile, off_s, gs_s, l16_s, yt_ref.at[buf], ys_ref, sem.at[buf], to_global=False, wait=False)

    @pl.when(t == 0)
    def _():
        fetch(t, slot)

    @pl.when(t + 1 < pl.num_programs(0))
    def _():
        fetch(t + 1, other)

    _segment_copies(t, off_s, gs_s, l16_s, yt_ref.at[slot], ys_ref, sem.at[slot], to_global=False, wait=True)
    slots = slots_ref[...]
    src = lax.broadcasted_iota(jnp.int32, (ts, SORT_CAP), 1).astype(F32)
    weights = jnp.where(src == slots[:, 0:1], slots[:, 2:3], 0.0) + jnp.where(src == slots[:, 1:2], slots[:, 3:4], 0.0)
    y = jnp.dot(weights.astype(BF16), yt_ref[slot], preferred_element_type=F32)
    for s, g2_ref in enumerate((g2a_ref, g2b_ref)):
        rws = slice(s * ROW_TILE, (s + 1) * ROW_TILE)
        xn = x_ref[rws, :] + g2_ref[0] * y[rws, :]
        if final:
            ms = jnp.mean(xn * xn, axis=-1, keepdims=True)
            xn = xn * lax.rsqrt(ms + EPS) * fg_ref[...]
        o_ref[rws, :] = xn


def _unsort_call(ys, slots, x, tables, final_g, mods, layer, mod_row, tile_off, final):
    rows, d = x.shape
    nt = rows // SORT_TILE
    sub = SORT_TILE // ROW_TILE
    g2_idx = (layer * 6 + 5) * COND_ROWS
    row_spec = pl.BlockSpec((SORT_TILE, d), lambda i, *_: (i, 0))

    def mod(s):
        return pl.BlockSpec((1, 1, d), lambda i, *_: (g2_idx + mod_row(i * sub + s + tile_off), 0, 0))

    grid_spec = pltpu.PrefetchScalarGridSpec(
        num_scalar_prefetch=3,
        grid=(nt,),
        in_specs=[
            pl.BlockSpec(memory_space=pl.ANY),
            pl.BlockSpec((SORT_TILE, 128), lambda i, *_: (i, 0)),
            row_spec,
            pl.BlockSpec((1, d), lambda i, *_: (0, 0)),
            mod(0), mod(1),
        ],
        out_specs=row_spec,
        scratch_shapes=[pltpu.VMEM((2, SORT_CAP, d), BF16), pltpu.SemaphoreType.DMA((2,))],
    )
    return pl.pallas_call(
        functools.partial(_unsort_kernel, final),
        grid_spec=grid_spec,
        out_shape=jax.ShapeDtypeStruct((rows, d), F32),
        compiler_params=_cparams(("arbitrary",)),
        name="moe_unsort_residual",
    )(*tables, ys, slots, x, final_g, mods, mods)


def _sum_before(a, axis):
    i = jnp.arange(a.shape[axis])
    mask = (i[None, :] < i[:, None]).astype(a.dtype)
    moved = jnp.moveaxis(a, axis, -1)
    return jnp.moveaxis(jnp.sum(moved[..., None, :] * mask, axis=-1), -1, axis)


def _moe_tables(cnt_blocks, n_blocks):
    cnt = cnt_blocks[:, :, 0].astype(jnp.int32)
    seg = (cnt + SEG_ALIGN - 1) // SEG_ALIGN * SEG_ALIGN
    off = _sum_before(seg, 1)
    blocks_e = (seg.sum(axis=0) + MOE_BLOCK - 1) // MOE_BLOCK
    first_block = _sum_before(blocks_e, 0)
    gstart = first_block[None, :] * MOE_BLOCK + _sum_before(seg, 0)
    blk = jnp.arange(n_blocks, dtype=jnp.int32)
    last_block = first_block + blocks_e
    blk_expert = jnp.minimum(jnp.sum((blk[:, None] >= last_block[None, :]).astype(jnp.int32), axis=1),
                             N_EXPERTS - 1)
    blk_valid = (blk < blocks_e.sum()).astype(jnp.int32)
    tables = (off.reshape(-1).astype(jnp.int32), gstart.reshape(-1).astype(jnp.int32),
              (seg // SEG_ALIGN).reshape(-1).astype(jnp.int32))
    offcol = jnp.broadcast_to(off.astype(F32)[:, :, None], off.shape + (128,))
    used = seg.sum(axis=0)
    tails = (jnp.concatenate([first_block * MOE_BLOCK + used, blocks_e.sum()[None]]).astype(jnp.int32),
             ((blocks_e * MOE_BLOCK - used) // SEG_ALIGN).astype(jnp.int32))
    return tables, tails, offcol, blk_expert.astype(jnp.int32), blk_valid


def _moe_call(h2, comb_t, cnt_blocks, x, w1, w3, w2, final_g, mods, layer, mod_row, tile_off, final):
    rows = x.shape[0]
    n_tiles = rows // SORT_TILE
    max_rows = 2 * rows + n_tiles * N_EXPERTS * (SEG_ALIGN - 1)
    n_blocks = max_rows // MOE_BLOCK + N_EXPERTS
    tables, tails, offcol, blk_expert, blk_valid = _moe_tables(cnt_blocks, n_blocks)
    slots, xs = _sort_call(h2, comb_t, tables, tails, offcol, n_blocks * MOE_BLOCK)
    ys = _experts_call(xs, blk_expert, blk_valid, w1, w3, w2, layer)
    return _unsort_call(ys, slots, x, tables, final_g, mods, layer, mod_row, tile_off, final)


def kernel(x, c, ctx, c_ctx, w_ada, b_ada, norm1_g, norm2_g, w_in, a_ln_g, a_w_s, a_b_s, b_conv_w, b_A_log, b_dt_bias, b_norm_g, w_proj_a, w_proj_b, w_out, w_router, b_router, w_e1, w_e3, w_e2, final_g):
    batch, seq_len, d = x.shape
    ctx_len = ctx.shape[1]
    n_layers = w_ada.shape[0]
    assert d == D_MODEL and batch + 1 <= COND_ROWS
    assert ctx_len % ROW_TILE == 0 and seq_len % ROW_TILE == 0
    assert (batch * ctx_len) % SORT_TILE == 0 and (batch * seq_len) % SORT_TILE == 0
    assert MIX_TILE == SORT_TILE and (batch * ctx_len) % MIX_TILE == 0 and seq_len % MIX_TILE == 0
    ctx_tiles = ctx_len // ROW_TILE
    lat_tiles = seq_len // ROW_TILE
    n_ctx_tiles = batch * ctx_tiles

    def mod_row(tile):
        return jnp.where(tile < n_ctx_tiles, batch, (tile - n_ctx_tiles) // lat_tiles)

    xs = (ctx.reshape(batch * ctx_len, d), x.reshape(batch * seq_len, d))
    cond = jnp.concatenate([c, c_ctx[None, :], jnp.zeros((COND_ROWS - batch - 1, d), F32)], axis=0)
    mods = _ada_call(cond, w_ada, b_ada).reshape(n_layers * 6 * COND_ROWS, 1, d)

    n_ab = 4 * N_HEADS
    ab_lo = 6 * d
    wr_t = w_router.T
    br = b_router.reshape(N_EXPERTS, 1)
    fg = final_g.reshape(1, d)
    w_bf = w_in.astype(BF16)
    w_all = jnp.concatenate([w_bf[..., :ab_lo], w_bf[..., ab_lo + n_ab:], w_bf[..., ab_lo:ab_lo + n_ab],
                             jnp.zeros((n_layers, d, 128 - n_ab), BF16)], axis=-1)
    for l in range(n_layers):
        conv_w = jnp.pad(b_conv_w[l], ((0, 8 - CONV_TAPS), (0, 0)))
        zeros8 = jnp.zeros((N_HEADS,), F32)
        alog_row = jnp.pad(jnp.concatenate([b_A_log[l, 0], zeros8, b_A_log[l, 1], zeros8]), (0, 128 - n_ab)).reshape(1, 128)
        dtb_row = jnp.pad(jnp.concatenate([b_dt_bias[l, 0], zeros8, b_dt_bias[l, 1], zeros8]), (0, 128 - n_ab)).reshape(1, 128)
        u, v, z, ga, gb, qn, kn, vs, gates = _in_call(
            xs, mods, l, norm1_g[l].reshape(1, d), w_all, conv_w, alog_row, dtb_row, mod_row,
            n_ctx_tiles, ctx_tiles, lat_tiles)
        o_f, o_b = _gdn_call(qn, kn, vs, gates, batch, ctx_len, seq_len)

        bs = jnp.repeat(a_b_s[l].T, SGU_CHUNK, axis=1)
        bng = b_norm_g[l].reshape(1, HEAD_DIM)
        last = l == n_layers - 1
        tile_off = n_ctx_tiles if last else 0
        x_new, h2, comb_t, cnt = _mix_call(
            xs, u, v, o_f, o_b, z, ga, gb, mods, l, norm2_g[l].reshape(1, d), a_ln_g[l].reshape(1, d),
            a_w_s[l].astype(BF16), bs, bng,
            w_proj_a[l].astype(BF16), w_proj_b[l].astype(BF16), w_out[l].astype(BF16), wr_t, br, mod_row,
            batch * ctx_len // MIX_TILE, tile_off * ROW_TILE // MIX_TILE)
        xs = (_moe_call(h2, comb_t, cnt, x_new, w_e1, w_e3, w_e2,
                        fg, mods, l, mod_row, tile_off, final=last),)
    return xs[0].reshape(batch, seq_len, d)
```

```python
import functools

import jax
import jax.numpy as jnp
from jax import lax
from jax.experimental import pallas as pl
from jax.experimental.pallas import tpu as pltpu

F32 = jnp.float32
BF16 = jnp.bfloat16
HIGHEST = lax.Precision.HIGHEST

EPS = 1e-6
D_MODEL = 1024
N_HEADS = 8
HEAD_DIM = 128
SGU_CHUNK = 128
SGU_GROUPS = 8
CONV_TAPS = 5
N_EXPERTS = 16
EXPERTS_PER_GROUP = 4
N_EXPERT_GROUPS = 4
D_EXPERT = 512
ROW_TILE = 256
MIX_TILE = 512
GDN_CHUNK = 128
GDN_BLOCK = 2
SORT_TILE = 512
SEG_ALIGN = 16
SORT_CAP = 1280
MOE_BLOCK = 512
SEG_BITS = (32, 16, 8, 4, 2, 1)
COND_ROWS = 8
HALO = 8
VMEM_LIMIT = 56 * 1024 * 1024


def _cparams(sem):
    return pltpu.CompilerParams(dimension_semantics=sem, vmem_limit_bytes=VMEM_LIMIT)


def _sigmoid(x):
    return 0.5 * jnp.tanh(0.5 * x) + 0.5


def _silu(x):
    return x * _sigmoid(x)


def _gelu_tanh(x):
    return 0.5 * x * (1.0 + jnp.tanh(0.7978845608028654 * (x + 0.044715 * (x * x * x))))


def _softplus(x):
    return jnp.maximum(x, 0.0) + jnp.log1p(jnp.exp(-jnp.abs(x)))


def _ada_kernel(cond_ref, w_ref, b_ref, o_ref):
    s = _silu(cond_ref[...])
    o_ref[0, 0] = jnp.dot(s, w_ref[0], precision=HIGHEST, preferred_element_type=F32) + b_ref[0, 0]


def _ada_call(cond, w_ada, b_ada):
    n_layers = w_ada.shape[0]
    d = D_MODEL
    return pl.pallas_call(
        _ada_kernel,
        grid=(n_layers, 6),
        in_specs=[
            pl.BlockSpec((COND_ROWS, d), lambda l, j: (0, 0)),
            pl.BlockSpec((1, d, d), lambda l, j: (l, 0, j)),
            pl.BlockSpec((1, 1, 1, d), lambda l, j: (l, j, 0, 0)),
        ],
        out_specs=pl.BlockSpec((1, 1, COND_ROWS, d), lambda l, j: (l, j, 0, 0)),
        out_shape=jax.ShapeDtypeStruct((n_layers, 6, COND_ROWS, d), F32),
        compiler_params=_cparams(("arbitrary", "arbitrary")),
        name="ada_params",
    )(cond, w_ada, b_ada.reshape(n_layers, 6, 1, d))


def _load_rows(x_refs, tile, n_ctx_tiles):
    if len(x_refs) == 1:
        return x_refs[0][...]
    return jnp.where(tile < n_ctx_tiles, x_refs[0][...], x_refs[1][...])


def _row_specs(xs, n_ctx_tiles, tile_off, d, tile=ROW_TILE):
    if len(xs) == 1:
        return [pl.BlockSpec((tile, d), lambda i: (i + tile_off, 0))]
    return [pl.BlockSpec((tile, d), lambda i: (jnp.minimum(i + tile_off, n_ctx_tiles - 1), 0)),
            pl.BlockSpec((tile, d), lambda i: (jnp.maximum(i + tile_off - n_ctx_tiles, 0), 0))]


def _halo_specs(xs, n_ctx_tiles, d, nxt):
    per = ROW_TILE // HALO

    def spec(n_blocks, first_tile):
        if nxt:
            return pl.BlockSpec((HALO, d), lambda i: (jnp.clip((i - first_tile + 1) * per, 0, n_blocks - 1), 0))
        return pl.BlockSpec((HALO, d), lambda i: (jnp.clip((i - first_tile) * per - 1, 0, n_blocks - 1), 0))

    if len(xs) == 1:
        return [spec(xs[0].shape[0] // HALO, 0)]
    return [spec(xs[0].shape[0] // HALO, 0), spec(xs[1].shape[0] // HALO, n_ctx_tiles)]


def _in_kernel(n_src, n_ctx_tiles, ctx_tiles, lat_tiles, *refs):
    x_refs, xp_refs, xn_refs = refs[:n_src], refs[n_src:2 * n_src], refs[2 * n_src:3 * n_src]
    (sh_ref, sc_ref, g_ref, w_ref, wg_ref, wab_ref, cw_ref, alog_ref, dtb_ref,
     u_ref, v_ref, z_ref, ga_ref, gb_ref, qo_ref, ko_ref, vo_ref, gate_ref, ext_ref) = refs[3 * n_src:]
    i = pl.program_id(0)
    d = D_MODEL
    j = jnp.where(i < n_ctx_tiles, i % ctx_tiles, (i - n_ctx_tiles) % lat_tiles)
    last = jnp.where(i < n_ctx_tiles, ctx_tiles - 1, lat_tiles - 1)
    has_prev = (j != 0).astype(F32)
    has_next = (j != last).astype(F32)

    x = jnp.concatenate([_load_rows(xp_refs, i, n_ctx_tiles), _load_rows(x_refs, i, n_ctx_tiles),
                         _load_rows(xn_refs, i, n_ctx_tiles)], axis=0)
    ms = jnp.mean(x * x, axis=-1, keepdims=True)
    h = x * lax.rsqrt(ms + EPS) * g_ref[...]
    h = h * (1.0 + sc_ref[0]) + sh_ref[0]
    hb_ext = h.astype(BF16)
    hb = h[HALO:HALO + ROW_TILE].astype(BF16)
    pad = CONV_TAPS // 2
    lo, hi = HALO, HALO + ROW_TILE
    step = 2 * HEAD_DIM

    def project_qkv(t, c0):
        cols = slice((2 + t) * d + c0, (2 + t) * d + c0 + step)
        proj = jnp.dot(hb_ext, w_ref[0, :, cols], preferred_element_type=F32)
        ext_ref[t, 0:lo, c0:c0 + step] = proj[0:lo] * has_prev
        ext_ref[t, lo:hi, c0:c0 + step] = proj[lo:hi]
        ext_ref[t, hi:hi + HALO, c0:c0 + step] = proj[hi:hi + HALO] * has_next
        return proj[hi + HALO - 1:hi + HALO, 0:HEAD_DIM]

    def conv_head(t, hd, o_ref, l2, scale, anchor):
        lanes = slice(hd * HEAD_DIM, (hd + 1) * HEAD_DIM)
        acc = None
        for tap in range(CONV_TAPS):
            w_row = cw_ref[tap:tap + 1, t * d + hd * HEAD_DIM:t * d + (hd + 1) * HEAD_DIM]
            if tap == 0 and anchor is not None:
                w_row = w_row + jnp.where(i < 0, anchor, 0.0)
            start = lo - pad + tap
            term = ext_ref[t, start:start + ROW_TILE, lanes] * w_row
            acc = term if acc is None else acc + term
        y = _silu(acc)
        if l2:
            ss = jnp.sum(y * y, axis=-1, keepdims=True)
            y = y * (lax.rsqrt(ss + EPS) * scale)
        o_ref[:, lanes] = y.astype(BF16)

    passes = [functools.partial(project_qkv, t, c0) for t in range(3) for c0 in range(0, d, step)]

    def project_other(n, o_ref, c0):
        src_ref, col = (w_ref, n * d + c0) if n < 6 else (wg_ref, (n - 6) * d + c0)
        res = jnp.dot(hb, src_ref[0, :, col:col + step], preferred_element_type=F32)
        o_ref[:, c0:c0 + step] = res.astype(BF16)
        return res[ROW_TILE - 1:ROW_TILE, 0:HEAD_DIM]

    passes += [functools.partial(project_other, n, o_ref, c0)
               for n, o_ref in ((0, u_ref), (1, v_ref), (5, z_ref), (6, ga_ref), (7, gb_ref))
               for c0 in range(0, d, step)]
    n_front = d // step
    for p in passes[:n_front]:
        p()
    rest = passes[n_front:]
    convs = [(t, hd) for t in range(3) for hd in range(N_HEADS)]
    n_rest = len(rest)
    conv_args = ((qo_ref, True, HEAD_DIM ** -0.5), (ko_ref, True, 1.0), (vo_ref, False, 1.0))
    anchor = None
    for j, (t, hd) in enumerate(convs):
        for _ in range((j + 1) * n_rest // len(convs) - j * n_rest // len(convs)):
            anchor = rest.pop(0)()
        conv_head(t, hd, *conv_args[t], anchor)
    for p in rest:
        p()
    ab = jnp.dot(hb, wab_ref[0], preferred_element_type=F32)
    lane = lax.broadcasted_iota(jnp.int32, ab.shape, 1)
    is_decay = ((lane // N_HEADS) % 2) == 0
    g = -jnp.exp(alog_ref[...]) * _softplus(ab + dtb_ref[...])
    gate = jnp.where(is_decay, g, _sigmoid(ab))
    gate = jnp.where(lane < 4 * N_HEADS, gate, 0.0)
    gate_ref[0] = gate
    gate_ref[1] = pltpu.roll(gate, 128 - 2 * N_HEADS, axis=1)


def _in_call(xs, mods, layer, norm_g, w_parts, conv_w, alog_row, dtb_row, mod_row, n_ctx_tiles, ctx_tiles, lat_tiles):
    d = D_MODEL
    rows = sum(a.shape[0] for a in xs)
    nt = rows // ROW_TILE
    sh_idx = (layer * 6 + 0) * COND_ROWS
    sc_idx = (layer * 6 + 1) * COND_ROWS
    row_spec = pl.BlockSpec((ROW_TILE, d), lambda i: (i, 0))
    small = pl.BlockSpec((1, 128), lambda i: (0, 0))
    return pl.pallas_call(
        functools.partial(_in_kernel, len(xs), n_ctx_tiles, ctx_tiles, lat_tiles),
        grid=(nt,),
        in_specs=(_row_specs(xs, n_ctx_tiles, 0, d) + _halo_specs(xs, n_ctx_tiles, d, False)
                  + _halo_specs(xs, n_ctx_tiles, d, True) + [
            pl.BlockSpec((1, 1, d), lambda i: (sh_idx + mod_row(i), 0, 0)),
            pl.BlockSpec((1, 1, d), lambda i: (sc_idx + mod_row(i), 0, 0)),
            pl.BlockSpec((1, d), lambda i: (0, 0)),
            pl.BlockSpec((1, d, 6 * d), lambda i: (layer, 0, 0), pipeline_mode=pl.Buffered(1)),
            pl.BlockSpec((1, d, 2 * d), lambda i: (layer, 0, 0), pipeline_mode=pl.Buffered(1)),
            pl.BlockSpec((1, d, 128), lambda i: (layer, 0, 0), pipeline_mode=pl.Buffered(1)),
            pl.BlockSpec((8, 3 * d), lambda i: (0, 0)),
            small, small,
        ]),
        out_specs=[row_spec] * 8 + [pl.BlockSpec((2, ROW_TILE, 128), lambda i: (0, i, 0))],
        out_shape=[jax.ShapeDtypeStruct((rows, d), BF16)] * 8 + [jax.ShapeDtypeStruct((2, rows, 128), F32)],
        scratch_shapes=[pltpu.VMEM((3, ROW_TILE + 2 * HALO, d), F32)],
        compiler_params=_cparams(("arbitrary",)),
        name="norm_in_proj_conv",
    )(*xs, *xs, *xs, mods, mods, norm_g, *w_parts, conv_w, alog_row, dtb_row)


def _gdn_kernel(qf_ref, kf_ref, vf_ref, gf_ref, qb_ref, kb_ref, vb_ref, gb_ref, of_ref, ob_ref, s_ref):
    @pl.when(pl.program_id(1) == 0)
    def _():
        s_ref[...] = jnp.zeros_like(s_ref)

    for n in range(GDN_BLOCK):
        rows_f = slice(n * GDN_CHUNK, (n + 1) * GDN_CHUNK)
        rows_b = slice((GDN_BLOCK - 1 - n) * GDN_CHUNK, (GDN_BLOCK - n) * GDN_CHUNK)
        _gdn_chunk(((True, rows_f, qf_ref, kf_ref, vf_ref, gf_ref, of_ref),
                    (False, rows_b, qb_ref, kb_ref, vb_ref, gb_ref, ob_ref)), s_ref)


def _gdn_chunk(scans, s_ref):
    c = GDN_CHUNK
    row = lax.broadcasted_iota(jnp.int32, (c, c), 0)
    col = lax.broadcasted_iota(jnp.int32, (c, c), 1)
    eye = (row == col).astype(F32)
    nt_dims = (((1,), (1,)), ((), ()))
    lanes = [slice(h * HEAD_DIM, (h + 1) * HEAD_DIM) for h in range(N_HEADS)]

    q, k, v, gc_col, gc_row, gtot_row, beta_row, incl, strict, dst = [], [], [], [], [], [], [], [], [], []
    for d_idx, (fwd, rows, q_ref, k_ref, v_ref, gate_ref, o_ref) in enumerate(scans):
        incl_d = (row >= col) if fwd else (row <= col)
        strict_d = (row > col) if fwd else (row < col)
        gate = gate_ref[0, rows, :]
        gc_all = jnp.dot(incl_d.astype(F32), gate, precision=HIGHEST, preferred_element_type=F32)
        gtot_all = jnp.dot(jnp.ones((c, c), F32), gate, precision=HIGHEST, preferred_element_type=F32)
        gate_t = gate.T
        gc_t = gc_all.T
        gtot_t = gtot_all.T
        for h in range(N_HEADS):
            q.append(q_ref[rows, lanes[h]])
            k.append(k_ref[rows, lanes[h]])
            v.append(v_ref[rows, lanes[h]])
            gc_col.append(jnp.broadcast_to(gc_all[:, h:h + 1], (c, c)))
            gc_row.append(gc_t[h:h + 1, :])
            gtot_row.append(gtot_t[h:h + 1, :])
            beta_row.append(gate_t[N_HEADS + h:N_HEADS + h + 1, :])
            incl.append(incl_d)
            strict.append(strict_d)
            dst.append((o_ref, rows, lanes[h], d_idx, h))
    chains = range(len(q))

    decay = [jnp.exp(jnp.where(incl[i], gc_col[i] - gc_row[i], -jnp.inf)) * beta_row[i] for i in chains]
    kq = [lax.dot_general(jnp.concatenate([k[i], q[i]], axis=0), k[i], nt_dims, preferred_element_type=F32)
          for i in chains]
    n_mat = [jnp.where(strict[i], kq[i][:c] * decay[i], 0.0) for i in chains]
    a_mat = [(kq[i][c:] * decay[i]).astype(BF16) for i in chains]
    a_b = [(eye + n_mat[i]).astype(BF16) for i in chains]
    t_inv = [eye - n_mat[i] for i in chains]
    for _ in range(6):
        tb = [t_inv[i].astype(BF16) for i in chains]
        err = [(eye - jnp.dot(a_b[i], tb[i], preferred_element_type=F32)).astype(BF16) for i in chains]
        t_inv = [t_inv[i] + jnp.dot(tb[i], err[i], preferred_element_type=F32) for i in chains]
    u_val = [jnp.dot(t_inv[i].astype(BF16), v[i], preferred_element_type=F32) for i in chains]
    w_key = [jnp.dot((t_inv[i] * jnp.exp(gc_row[i])).astype(BF16), k[i], preferred_element_type=F32)
             for i in chains]

    s_old = [s_ref[dst[i][3], dst[i][4]] for i in chains]
    sb = [s_old[i].astype(BF16) for i in chains]
    wq_s = [jnp.dot(jnp.concatenate([w_key[i].astype(BF16), q[i]], axis=0), sb[i], preferred_element_type=F32)
            for i in chains]
    v_new = [(u_val[i] - wq_s[i][:c]).astype(BF16) for i in chains]
    kd_t = [(k[i].astype(F32).T * (jnp.exp(gtot_row[i] - gc_row[i]) * beta_row[i])).astype(BF16) for i in chains]
    ak_v = [jnp.dot(jnp.concatenate([a_mat[i], kd_t[i]], axis=0), v_new[i], preferred_element_type=F32)
            for i in chains]
    for i in chains:
        o_ref, rows, ln, d_idx, h = dst[i]
        s_ref[d_idx, h] = s_old[i] * jnp.exp(gtot_row[i]) + ak_v[i][c:]
        o_ref[rows, ln] = (jnp.exp(gc_col[i]) * wq_s[i][c:] + ak_v[i][:c]).astype(BF16)


def _gdn_call(q, k, v, gates, batch, ctx_len, seq_len):
    rows, d = q.shape
    c = GDN_BLOCK * GDN_CHUNK
    assert ctx_len % c == 0 and seq_len % c == 0
    n_ctx = ctx_len // c
    n_lat = seq_len // c
    n_steps = n_ctx + n_lat

    def chunk(b, fwd, s):
        pos = s if fwd else jnp.where(s < n_ctx, n_ctx - 1 - s, n_ctx + n_steps - 1 - s)
        return jnp.where(pos < n_ctx, b * n_ctx + pos, batch * n_ctx + b * n_lat + pos - n_ctx)

    def specs(fwd):
        blk = pl.BlockSpec((c, d), lambda b, s: (chunk(b, fwd, s), 0))
        return [blk, blk, blk, pl.BlockSpec((1, c, 128), lambda b, s: (0 if fwd else 1, chunk(b, fwd, s), 0))]

    return pl.pallas_call(
        _gdn_kernel,
        grid=(batch, n_steps),
        in_specs=specs(True) + specs(False),
        out_specs=[pl.BlockSpec((c, d), lambda b, s: (chunk(b, True, s), 0)),
                   pl.BlockSpec((c, d), lambda b, s: (chunk(b, False, s), 0))],
        out_shape=[jax.ShapeDtypeStruct((rows, d), BF16)] * 2,
        scratch_shapes=[pltpu.VMEM((2, N_HEADS, HEAD_DIM, HEAD_DIM), F32)],
        compiler_params=_cparams(("arbitrary", "arbitrary")),
        name="gated_delta",
    )(q, k, v, gates, q, k, v, gates)


def _mix_kernel(n_src, n_ctx_tiles, tile_off, *refs):
    x_refs = refs[:n_src]
    (u_ref, v_ref, of_ref, ob_ref, z_ref, ga_ref, gb_ref,
     g1_ref, sh2_ref, sc2_ref, n2g_ref, lng_ref, ws_ref, bs_ref, bng_ref,
     wpa_ref, wpb_ref, wout_ref, wr_ref, br_ref,
     xo_ref, h2_ref, comb_ref, cnt_ref, sa_ref, sb_ref) = refs[n_src:]
    d = D_MODEL
    ug = _gelu_tanh(u_ref[...].astype(F32))
    vg = _gelu_tanh(v_ref[...].astype(F32))
    mu = jnp.mean(vg, axis=-1, keepdims=True)
    vc = vg - mu
    var = jnp.mean(vc * vc, axis=-1, keepdims=True)
    vn = (vc * lax.rsqrt(var + EPS) * lng_ref[...]).astype(BF16)
    for ch in range(MIX_TILE // SGU_CHUNK):
        rws = slice(ch * SGU_CHUNK, (ch + 1) * SGU_CHUNK)
        for g in range(SGU_GROUPS):
            lanes = slice(g * 128, (g + 1) * 128)
            mixed = jnp.dot(ws_ref[g], vn[rws, lanes], preferred_element_type=F32) + bs_ref[:, lanes]
            sa_ref[rws, lanes] = (ug[rws, lanes] * mixed).astype(BF16)
    y_a = jnp.dot(sa_ref[...], wpa_ref[...], preferred_element_type=F32)
    for h in range(N_HEADS):
        lanes = slice(h * HEAD_DIM, (h + 1) * HEAD_DIM)
        o = of_ref[:, lanes].astype(F32) + ob_ref[:, lanes].astype(F32)
        ms = jnp.mean(o * o, axis=-1, keepdims=True)
        o = o * lax.rsqrt(ms + EPS) * bng_ref[...]
        sb_ref[:, lanes] = (o * _silu(z_ref[:, lanes].astype(F32))).astype(BF16)
    y_b = jnp.dot(sb_ref[...], wpb_ref[...], preferred_element_type=F32)
    merged = _sigmoid(ga_ref[...].astype(F32)) * y_a + _sigmoid(gb_ref[...].astype(F32)) * y_b
    y = jnp.dot(merged.astype(BF16), wout_ref[...], preferred_element_type=F32)
    xn = _load_rows(x_refs, pl.program_id(0) + tile_off, n_ctx_tiles) + g1_ref[0] * y
    xo_ref[...] = xn
    ms = jnp.mean(xn * xn, axis=-1, keepdims=True)
    h2 = xn * lax.rsqrt(ms + EPS) * n2g_ref[...]
    h2 = h2 * (1.0 + sc2_ref[0]) + sh2_ref[0]
    h2_ref[...] = h2.astype(BF16)
    logits = lax.dot_general(wr_ref[...], h2, (((1,), (1,)), ((), ())),
                             precision=HIGHEST, preferred_element_type=F32)
    scores = _sigmoid(logits)
    sel = scores + br_ref[...]
    srow = [sel[e:e + 1, :] for e in range(N_EXPERTS)]
    grp = []
    for g in range(N_EXPERT_GROUPS):
        m = srow[4 * g:4 * g + 4]
        best2 = None
        for a in range(4):
            for b in range(a + 1, 4):
                pair = m[a] + m[b]
                best2 = pair if best2 is None else jnp.maximum(best2, pair)
        grp.append(best2)
    best_val = grp[0]
    best_idx = jnp.zeros_like(best_val, dtype=jnp.int32)
    for g in range(1, N_EXPERT_GROUPS):
        better = grp[g] > best_val
        best_val = jnp.where(better, grp[g], best_val)
        best_idx = jnp.where(better, g, best_idx)
    picked = []
    for e in range(N_EXPERTS):
        g = e // EXPERTS_PER_GROUP
        rank = jnp.zeros_like(best_idx)
        for o_e in range(4 * g, 4 * g + 4):
            if o_e == e:
                continue
            ahead = (srow[o_e] > srow[e]) if o_e > e else (srow[o_e] >= srow[e])
            rank = rank + ahead.astype(jnp.int32)
        chosen = jnp.logical_and(best_idx == g, rank < 2)
        picked.append(jnp.where(chosen, scores[e:e + 1, :], 0.0))
    total = picked[0]
    for e in range(1, N_EXPERTS):
        total = total + picked[e]
    inv = 1.0 / total
    for e in range(N_EXPERTS):
        w_e = picked[e] * inv
        comb_ref[e:e + 1, :] = w_e
        n_e = jnp.sum(jnp.where(w_e > 0.0, 1.0, 0.0), axis=-1, keepdims=True)
        cnt_ref[0, e:e + 1, :] = jnp.broadcast_to(n_e, (1, 128))


def _mix_call(xs, u, v, o_f, o_b, z, ga, gb, mods, layer, norm2_g, ln_g, ws, bs, bng,
              wpa, wpb, wout, wr_t, br, mod_row, n_ctx_tiles, tile_off):
    d = D_MODEL
    nt = sum(a.shape[0] for a in xs) // MIX_TILE - tile_off
    rows = nt * MIX_TILE
    per = MIX_TILE // ROW_TILE
    g1_idx = (layer * 6 + 2) * COND_ROWS
    sh2_idx = (layer * 6 + 3) * COND_ROWS
    sc2_idx = (layer * 6 + 4) * COND_ROWS
    row_spec = pl.BlockSpec((MIX_TILE, d), lambda i: (i + tile_off, 0))
    out_spec = pl.BlockSpec((MIX_TILE, d), lambda i: (i, 0))
    vec = pl.BlockSpec((1, d), lambda i: (0, 0))
    wspec = pl.BlockSpec((d, d), lambda i: (0, 0))

    def mod(idx):
        return pl.BlockSpec((1, 1, d), lambda i: (idx + mod_row((i + tile_off) * per), 0, 0))

    return pl.pallas_call(
        functools.partial(_mix_kernel, len(xs), n_ctx_tiles, tile_off),
        grid=(nt,),
        in_specs=_row_specs(xs, n_ctx_tiles, tile_off, d, MIX_TILE) + [
            row_spec, row_spec,
            row_spec, row_spec,
            row_spec, row_spec, row_spec,
            mod(g1_idx), mod(sh2_idx), mod(sc2_idx),
            vec, vec,
            pl.BlockSpec((SGU_GROUPS, SGU_CHUNK, SGU_CHUNK), lambda i: (0, 0, 0)),
            pl.BlockSpec((SGU_CHUNK, d), lambda i: (0, 0)),
            pl.BlockSpec((1, HEAD_DIM), lambda i: (0, 0)),
            wspec, wspec, wspec,
            pl.BlockSpec((N_EXPERTS, d), lambda i: (0, 0)),
            pl.BlockSpec((N_EXPERTS, 1), lambda i: (0, 0)),
        ],
        out_specs=[out_spec, out_spec, pl.BlockSpec((N_EXPERTS, MIX_TILE), lambda i: (0, i)),
                   pl.BlockSpec((1, N_EXPERTS, 128), lambda i: (i, 0, 0))],
        out_shape=[jax.ShapeDtypeStruct((rows, d), F32), jax.ShapeDtypeStruct((rows, d), BF16),
                   jax.ShapeDtypeStruct((N_EXPERTS, rows), F32),
                   jax.ShapeDtypeStruct((nt, N_EXPERTS, 128), F32)],
        scratch_shapes=[pltpu.VMEM((MIX_TILE, d), BF16), pltpu.VMEM((MIX_TILE, d), BF16)],
        compiler_params=_cparams(("arbitrary",)),
        name="mix_merge_router",
    )(*xs, u, v, o_f, o_b, z, ga, gb, mods, mods, mods, norm2_g, ln_g, ws, bs, bng,
      wpa, wpb, wout, wr_t, br)


def _piece_copies(lo, go, n16, local_ref, global_ref, sem, to_global, wait, repeat_local=False):
    done = jnp.int32(0)
    for bit in SEG_BITS:
        size = bit * SEG_ALIGN
        present = (n16 & bit) != 0
        l_start = lo if repeat_local else lo + done
        l_at = local_ref.at[pl.ds(pl.multiple_of(l_start, SEG_ALIGN), size)]
        g_at = global_ref.at[pl.ds(pl.multiple_of(go + done, SEG_ALIGN), size)]
        cp = pltpu.make_async_copy(l_at, g_at, sem) if to_global else pltpu.make_async_copy(g_at, l_at, sem)

        @pl.when(present)
        def _():
            if wait:
                cp.wait()
            else:
                cp.start()

        done = done + jnp.where(present, size, 0)


def _segment_copies(t, off_s, gs_s, l16_s, local_ref, global_ref, sem, to_global, wait):
    for e in range(N_EXPERTS):
        idx = t * N_EXPERTS + e
        _piece_copies(off_s[idx], gs_s[idx], l16_s[idx], local_ref, global_ref, sem, to_global, wait)


def _sort_kernel(off_s, gs_s, l16_s, tail_s, tail16_s, h_ref, comb_ref, offcol_ref, slots_ref, xs_ref,
                 hs_ref, zero_ref, sem):
    t = pl.program_id(0)
    slot = t % 2
    other = 1 - slot
    ts = SORT_TILE
    comb = comb_ref[...]
    asg = comb > 0.0
    row = lax.broadcasted_iota(jnp.int32, (ts, ts), 0)
    col = lax.broadcasted_iota(jnp.int32, (ts, ts), 1)
    before = jnp.where(row < col, 1.0, 0.0).astype(BF16)
    rank = jnp.dot(jnp.where(asg, 1.0, 0.0).astype(BF16), before, preferred_element_type=F32)
    pos = offcol_ref[0][:, 0:1] + rank
    p_lo = jnp.min(jnp.where(asg, pos, 1e9), axis=0, keepdims=True)
    p_hi = jnp.max(jnp.where(asg, pos, -1.0), axis=0, keepdims=True)
    w_lo = jnp.sum(jnp.where(jnp.logical_and(asg, pos == p_lo), comb, 0.0), axis=0, keepdims=True)
    w_hi = jnp.sum(jnp.where(jnp.logical_and(asg, pos == p_hi), comb, 0.0), axis=0, keepdims=True)
    w_hi = jnp.where(p_hi > p_lo, w_hi, 0.0)
    dest = lax.broadcasted_iota(jnp.int32, (SORT_CAP, ts), 0).astype(F32)
    onehot = jnp.where(jnp.logical_or(dest == p_lo, dest == p_hi), 1.0, 0.0).astype(BF16)
    hs_ref[slot] = jnp.dot(onehot, h_ref[...], preferred_element_type=F32).astype(BF16)
    slot_rows = jnp.concatenate([p_lo, p_hi, w_lo, w_hi, jnp.zeros((124, ts), F32)], axis=0)
    slots_ref[...] = slot_rows.T
    _segment_copies(t, off_s, gs_s, l16_s, hs_ref.at[slot], xs_ref, sem.at[slot], to_global=True, wait=False)

    @pl.when(t > 0)
    def _():
        _segment_copies(t - 1, off_s, gs_s, l16_s, hs_ref.at[other], xs_ref, sem.at[other], to_global=True, wait=True)

    @pl.when(t == pl.num_programs(0) - 1)
    def _():
        _segment_copies(t, off_s, gs_s, l16_s, hs_ref.at[slot], xs_ref, sem.at[slot], to_global=True, wait=True)
        zero_ref[...] = jnp.zeros_like(zero_ref)
        n_blocks = xs_ref.shape[0] // MOE_BLOCK

        def spare_block(b):
            rows = pl.ds(pl.multiple_of(b * MOE_BLOCK, MOE_BLOCK), MOE_BLOCK)
            return pltpu.make_async_copy(zero_ref, xs_ref.at[rows], sem.at[2])

        for wait in (False, True):
            for e in range(N_EXPERTS):
                _piece_copies(0, tail_s[e], tail16_s[e], zero_ref, xs_ref, sem.at[2], to_global=True, wait=wait,
                              repeat_local=True)

            @pl.loop(tail_s[N_EXPERTS], n_blocks)
            def _(b):
                if wait:
                    spare_block(b).wait()
                else:
                    spare_block(b).start()


def _sort_call(h2, comb_t, tables, tails, offcol, n_rows_sorted):
    rows, d = h2.shape
    nt = rows // SORT_TILE
    grid_spec = pltpu.PrefetchScalarGridSpec(
        num_scalar_prefetch=5,
        grid=(nt,),
        in_specs=[
            pl.BlockSpec((SORT_TILE, d), lambda i, *_: (i, 0)),
            pl.BlockSpec((N_EXPERTS, SORT_TILE), lambda i, *_: (0, i)),
            pl.BlockSpec((1, N_EXPERTS, 128), lambda i, *_: (i, 0, 0)),
        ],
        out_specs=[
            pl.BlockSpec((SORT_TILE, 128), lambda i, *_: (i, 0)),
            pl.BlockSpec(memory_space=pl.ANY),
        ],
        scratch_shapes=[pltpu.VMEM((2, SORT_CAP, d), BF16), pltpu.VMEM((MOE_BLOCK, d), BF16),
                        pltpu.SemaphoreType.DMA((3,))],
    )
    slots, xs = pl.pallas_call(
        _sort_kernel,
        grid_spec=grid_spec,
        out_shape=[jax.ShapeDtypeStruct((rows, 128), F32), jax.ShapeDtypeStruct((n_rows_sorted, d), BF16)],
        compiler_params=_cparams(("arbitrary",)),
        name="moe_sort",
    )(*tables, *tails, h2, comb_t, offcol)
    return slots, xs


def _experts_kernel(be_s, bv_s, x_ref, w1_ref, w3_ref, w2_ref, y_ref):
    b = pl.program_id(0)

    @pl.when(bv_s[b] != 0)
    def _():
        x = x_ref[...]
        a = jnp.dot(x, w1_ref[0, 0].astype(BF16), preferred_element_type=F32)
        g = jnp.dot(x, w3_ref[0, 0].astype(BF16), preferred_element_type=F32)
        hid = (_silu(a) * g).astype(BF16)
        y_ref[...] = jnp.dot(hid, w2_ref[0, 0].astype(BF16), preferred_element_type=F32).astype(BF16)

    @pl.when(bv_s[b] == 0)
    def _():
        y_ref[...] = jnp.zeros_like(y_ref)


def _experts_call(xs, blk_expert, blk_valid, w1, w3, w2, layer):
    rows, d = xs.shape
    nb = rows // MOE_BLOCK
    grid_spec = pltpu.PrefetchScalarGridSpec(
        num_scalar_prefetch=2,
        grid=(nb,),
        in_specs=[
            pl.BlockSpec((MOE_BLOCK, d), lambda b, be, bv: (b, 0)),
            pl.BlockSpec((1, 1, d, D_EXPERT), lambda b, be, bv: (layer, be[b], 0, 0)),
            pl.BlockSpec((1, 1, d, D_EXPERT), lambda b, be, bv: (layer, be[b], 0, 0)),
            pl.BlockSpec((1, 1, D_EXPERT, d), lambda b, be, bv: (layer, be[b], 0, 0)),
        ],
        out_specs=pl.BlockSpec((MOE_BLOCK, d), lambda b, be, bv: (b, 0)),
    )
    return pl.pallas_call(
        _experts_kernel,
        grid_spec=grid_spec,
        out_shape=jax.ShapeDtypeStruct((rows, d), BF16),
        compiler_params=_cparams(("arbitrary",)),
        name="moe_experts",
    )(blk_expert, blk_valid, xs, w1, w3, w2)


def _unsort_kernel(final, off_s, gs_s, l16_s, ys_ref, slots_ref, x_ref, fg_ref, g2a_ref, g2b_ref,
                   o_ref, yt_ref, sem):
    t = pl.program_id(0)
    ts = SORT_TILE
    slot = t % 2
    other = 1 - slot

    def fetch(tile, buf):
        yt_ref[buf] = jnp.zeros(yt_ref.shape[1:], yt_ref.dtype)
        _segment_copies(tile, off_s, gs_s, l16_s, yt_ref.at[buf], ys_ref, sem.at[buf], to_global=False, wait=False)

    @pl.when(t == 0)
    def _():
        fetch(t, slot)

    @pl.when(t + 1 < pl.num_programs(0))
    def _():
        fetch(t + 1, other)

    _segment_copies(t, off_s, gs_s, l16_s, yt_ref.at[slot], ys_ref, sem.at[slot], to_global=False, wait=True)
    slots = slots_ref[...]
    src = lax.broadcasted_iota(jnp.int32, (ts, SORT_CAP), 1).astype(F32)
    weights = jnp.where(src == slots[:, 0:1], slots[:, 2:3], 0.0) + jnp.where(src == slots[:, 1:2], slots[:, 3:4], 0.0)
    y = jnp.dot(weights.astype(BF16), yt_ref[slot], preferred_element_type=F32)
    for s, g2_ref in enumerate((g2a_ref, g2b_ref)):
        rws = slice(s * ROW_TILE, (s + 1) * ROW_TILE)
        xn = x_ref[rws, :] + g2_ref[0] * y[rws, :]
        if final:
            ms = jnp.mean(xn * xn, axis=-1, keepdims=True)
            xn = xn * lax.rsqrt(ms + EPS) * fg_ref[...]
        o_ref[rws, :] = xn


def _unsort_call(ys, slots, x, tables, final_g, mods, layer, mod_row, tile_off, final):
    rows, d = x.shape
    nt = rows // SORT_TILE
    sub = SORT_TILE // ROW_TILE
    g2_idx = (layer * 6 + 5) * COND_ROWS
    row_spec = pl.BlockSpec((SORT_TILE, d), lambda i, *_: (i, 0))

    def mod(s):
        return pl.BlockSpec((1, 1, d), lambda i, *_: (g2_idx + mod_row(i * sub + s + tile_off), 0, 0))

    grid_spec = pltpu.PrefetchScalarGridSpec(
        num_scalar_prefetch=3,
        grid=(nt,),
        in_specs=[
            pl.BlockSpec(memory_space=pl.ANY),
            pl.BlockSpec((SORT_TILE, 128), lambda i, *_: (i, 0)),
            row_spec,
            pl.BlockSpec((1, d), lambda i, *_: (0, 0)),
            mod(0), mod(1),
        ],
        out_specs=row_spec,
        scratch_shapes=[pltpu.VMEM((2, SORT_CAP, d), BF16), pltpu.SemaphoreType.DMA((2,))],
    )
    return pl.pallas_call(
        functools.partial(_unsort_kernel, final),
        grid_spec=grid_spec,
        out_shape=jax.ShapeDtypeStruct((rows, d), F32),
        compiler_params=_cparams(("arbitrary",)),
        name="moe_unsort_residual",
    )(*tables, ys, slots, x, final_g, mods, mods)


def _sum_before(a, axis):
    i = jnp.arange(a.shape[axis])
    mask = (i[None, :] < i[:, None]).astype(a.dtype)
    moved = jnp.moveaxis(a, axis, -1)
    return jnp.moveaxis(jnp.sum(moved[..., None, :] * mask, axis=-1), -1, axis)


def _moe_tables(cnt_blocks, n_blocks):
    cnt = cnt_blocks[:, :, 0].astype(jnp.int32)
    seg = (cnt + SEG_ALIGN - 1) // SEG_ALIGN * SEG_ALIGN
    off = _sum_before(seg, 1)
    blocks_e = (seg.sum(axis=0) + MOE_BLOCK - 1) // MOE_BLOCK
    first_block = _sum_before(blocks_e, 0)
    gstart = first_block[None, :] * MOE_BLOCK + _sum_before(seg, 0)
    blk = jnp.arange(n_blocks, dtype=jnp.int32)
    last_block = first_block + blocks_e
    blk_expert = jnp.minimum(jnp.sum((blk[:, None] >= last_block[None, :]).astype(jnp.int32), axis=1),
                             N_EXPERTS - 1)
    blk_valid = (blk < blocks_e.sum()).astype(jnp.int32)
    tables = (off.reshape(-1).astype(jnp.int32), gstart.reshape(-1).astype(jnp.int32),
              (seg // SEG_ALIGN).reshape(-1).astype(jnp.int32))
    offcol = jnp.broadcast_to(off.astype(F32)[:, :, None], off.shape + (128,))
    used = seg.sum(axis=0)
    tails = (jnp.concatenate([first_block * MOE_BLOCK + used, blocks_e.sum()[None]]).astype(jnp.int32),
             ((blocks_e * MOE_BLOCK - used) // SEG_ALIGN).astype(jnp.int32))
    return tables, tails, offcol, blk_expert.astype(jnp.int32), blk_valid


def _moe_call(h2, comb_t, cnt_blocks, x, w1, w3, w2, final_g, mods, layer, mod_row, tile_off, final):
    rows = x.shape[0]
    n_tiles = rows // SORT_TILE
    max_rows = 2 * rows + n_tiles * N_EXPERTS * (SEG_ALIGN - 1)
    n_blocks = max_rows // MOE_BLOCK + N_EXPERTS
    tables, tails, offcol, blk_expert, blk_valid = _moe_tables(cnt_blocks, n_blocks)
    slots, xs = _sort_call(h2, comb_t, tables, tails, offcol, n_blocks * MOE_BLOCK)
    ys = _experts_call(xs, blk_expert, blk_valid, w1, w3, w2, layer)
    return _unsort_call(ys, slots, x, tables, final_g, mods, layer, mod_row, tile_off, final)


def kernel(x, c, ctx, c_ctx, w_ada, b_ada, norm1_g, norm2_g, w_in, a_ln_g, a_w_s, a_b_s, b_conv_w, b_A_log, b_dt_bias, b_norm_g, w_proj_a, w_proj_b, w_out, w_router, b_router, w_e1, w_e3, w_e2, final_g):
    batch, seq_len, d = x.shape
    ctx_len = ctx.shape[1]
    n_layers = w_ada.shape[0]
    assert d == D_MODEL and batch + 1 <= COND_ROWS
    assert ctx_len % ROW_TILE == 0 and seq_len % ROW_TILE == 0
    assert (batch * ctx_len) % SORT_TILE == 0 and (batch * seq_len) % SORT_TILE == 0
    assert MIX_TILE == SORT_TILE and (batch * ctx_len) % MIX_TILE == 0 and seq_len % MIX_TILE == 0
    ctx_tiles = ctx_len // ROW_TILE
    lat_tiles = seq_len // ROW_TILE
    n_ctx_tiles = batch * ctx_tiles

    def mod_row(tile):
        return jnp.where(tile < n_ctx_tiles, batch, (tile - n_ctx_tiles) // lat_tiles)

    xs = (ctx.reshape(batch * ctx_len, d), x.reshape(batch * seq_len, d))
    cond = jnp.concatenate([c, c_ctx[None, :], jnp.zeros((COND_ROWS - batch - 1, d), F32)], axis=0)
    mods = _ada_call(cond, w_ada, b_ada).reshape(n_layers * 6 * COND_ROWS, 1, d)

    n_ab = 4 * N_HEADS
    ab_lo = 6 * d
    wr_t = w_router.T
    br = b_router.reshape(N_EXPERTS, 1)
    fg = final_g.reshape(1, d)
    w_bf = w_in.astype(BF16)
    w_parts = (w_bf[..., :ab_lo], w_bf[..., ab_lo + n_ab:],
               jnp.pad(w_bf[..., ab_lo:ab_lo + n_ab], ((0, 0), (0, 0), (0, 128 - n_ab))))
    for l in range(n_layers):
        conv_w = jnp.pad(b_conv_w[l], ((0, 8 - CONV_TAPS), (0, 0)))
        zeros8 = jnp.zeros((N_HEADS,), F32)
        alog_row = jnp.pad(jnp.concatenate([b_A_log[l, 0], zeros8, b_A_log[l, 1], zeros8]), (0, 128 - n_ab)).reshape(1, 128)
        dtb_row = jnp.pad(jnp.concatenate([b_dt_bias[l, 0], zeros8, b_dt_bias[l, 1], zeros8]), (0, 128 - n_ab)).reshape(1, 128)
        u, v, z, ga, gb, qn, kn, vs, gates = _in_call(
            xs, mods, l, norm1_g[l].reshape(1, d), w_parts, conv_w, alog_row, dtb_row, mod_row,
            n_ctx_tiles, ctx_tiles, lat_tiles)
        o_f, o_b = _gdn_call(qn, kn, vs, gates, batch, ctx_len, seq_len)

        bs = jnp.repeat(a_b_s[l].T, SGU_CHUNK, axis=1)
        bng = b_norm_g[l].reshape(1, HEAD_DIM)
        last = l == n_layers - 1
        tile_off = n_ctx_tiles if last else 0
        x_new, h2, comb_t, cnt = _mix_call(
            xs, u, v, o_f, o_b, z, ga, gb, mods, l, norm2_g[l].reshape(1, d), a_ln_g[l].reshape(1, d),
            a_w_s[l].astype(BF16), bs, bng,
            w_proj_a[l].astype(BF16), w_proj_b[l].astype(BF16), w_out[l].astype(BF16), wr_t, br, mod_row,
            batch * ctx_len // MIX_TILE, tile_off * ROW_TILE // MIX_TILE)
        xs = (_moe_call(h2, comb_t, cnt, x_new, w_e1, w_e3, w_e2,
                        fg, mods, l, mod_row, tile_off, final=last),)
    return xs[0].reshape(batch, seq_len, d)
```

```python
import functools

import jax
import jax.numpy as jnp
from jax import lax
from jax.experimental import pallas as pl
from jax.experimental.pallas import tpu as pltpu

F32 = jnp.float32
BF16 = jnp.bfloat16
HIGHEST = lax.Precision.HIGHEST

EPS = 1e-6
D_MODEL = 1024
N_HEADS = 8
HEAD_DIM = 128
SGU_CHUNK = 128
SGU_GROUPS = 8
SGU_GROUP_DIM = D_MODEL // SGU_GROUPS
CONV_TAPS = 5
N_EXPERTS = 16
EXPERTS_PER_GROUP = 4
N_EXPERT_GROUPS = 4
D_EXPERT = 512
ROW_TILE = 256
MIX_TILE = 512
GDN_CHUNK = 128
GDN_BLOCK = 2
SORT_TILE = 512
SEG_ALIGN = 16
LANES = 128
SORT_CAP = -(-(2 * SORT_TILE + N_EXPERTS * (SEG_ALIGN - 1)) // LANES) * LANES
MOE_BLOCK = 512
SEG_BITS = (32, 16, 8, 4, 2, 1)
COND_ROWS = 8
HALO = 8
VMEM_LIMIT = 56 * 1024 * 1024


def _cparams(sem):
    return pltpu.CompilerParams(dimension_semantics=sem, vmem_limit_bytes=VMEM_LIMIT)


def _sigmoid(x):
    return 0.5 * jnp.tanh(0.5 * x) + 0.5


def _silu(x):
    return x * _sigmoid(x)


def _gelu_tanh(x):
    return 0.5 * x * (1.0 + jnp.tanh(0.7978845608028654 * (x + 0.044715 * (x * x * x))))


def _softplus(x):
    return jnp.maximum(x, 0.0) + jnp.log1p(jnp.exp(-jnp.abs(x)))


def _ada_kernel(cond_ref, w_ref, b_ref, o_ref):
    s = _silu(cond_ref[...])
    o_ref[0, 0] = jnp.dot(s, w_ref[0], precision=HIGHEST, preferred_element_type=F32) + b_ref[0, 0]


def _ada_call(cond, w_ada, b_ada):
    n_layers = w_ada.shape[0]
    d = D_MODEL
    return pl.pallas_call(
        _ada_kernel,
        grid=(n_layers, 6),
        in_specs=[
            pl.BlockSpec((COND_ROWS, d), lambda l, j: (0, 0)),
            pl.BlockSpec((1, d, d), lambda l, j: (l, 0, j)),
            pl.BlockSpec((1, 1, 1, d), lambda l, j: (l, j, 0, 0)),
        ],
        out_specs=pl.BlockSpec((1, 1, COND_ROWS, d), lambda l, j: (l, j, 0, 0)),
        out_shape=jax.ShapeDtypeStruct((n_layers, 6, COND_ROWS, d), F32),
        compiler_params=_cparams(("arbitrary", "arbitrary")),
        name="ada_params",
    )(cond, w_ada, b_ada.reshape(n_layers, 6, 1, d))


def _load_rows(x_refs, tile, n_ctx_tiles):
    if len(x_refs) == 1:
        return x_refs[0][...]
    return jnp.where(tile < n_ctx_tiles, x_refs[0][...], x_refs[1][...])


def _row_specs(xs, n_ctx_tiles, tile_off, d, tile=ROW_TILE):
    if len(xs) == 1:
        return [pl.BlockSpec((tile, d), lambda i: (i + tile_off, 0))]
    return [pl.BlockSpec((tile, d), lambda i: (jnp.minimum(i + tile_off, n_ctx_tiles - 1), 0)),
            pl.BlockSpec((tile, d), lambda i: (jnp.maximum(i + tile_off - n_ctx_tiles, 0), 0))]


def _halo_specs(xs, n_ctx_tiles, d, nxt):
    per = ROW_TILE // HALO

    def spec(n_blocks, first_tile):
        if nxt:
            return pl.BlockSpec((HALO, d), lambda i: (jnp.clip((i - first_tile + 1) * per, 0, n_blocks - 1), 0))
        return pl.BlockSpec((HALO, d), lambda i: (jnp.clip((i - first_tile) * per - 1, 0, n_blocks - 1), 0))

    if len(xs) == 1:
        return [spec(xs[0].shape[0] // HALO, 0)]
    return [spec(xs[0].shape[0] // HALO, 0), spec(xs[1].shape[0] // HALO, n_ctx_tiles)]


def _in_kernel(n_src, n_ctx_tiles, ctx_tiles, lat_tiles, *refs):
    x_refs, xp_refs, xn_refs = refs[:n_src], refs[n_src:2 * n_src], refs[2 * n_src:3 * n_src]
    (sh_ref, sc_ref, g_ref, w_ref, cw_ref, alog_ref, dtb_ref,
     u_ref, v_ref, z_ref, ga_ref, gb_ref, qo_ref, ko_ref, vo_ref, gate_ref, ext_ref) = refs[3 * n_src:]
    i = pl.program_id(0)
    d = D_MODEL
    j = jnp.where(i < n_ctx_tiles, i % ctx_tiles, (i - n_ctx_tiles) % lat_tiles)
    last = jnp.where(i < n_ctx_tiles, ctx_tiles - 1, lat_tiles - 1)
    has_prev = (j != 0).astype(F32)
    has_next = (j != last).astype(F32)

    x = jnp.concatenate([_load_rows(xp_refs, i, n_ctx_tiles), _load_rows(x_refs, i, n_ctx_tiles),
                         _load_rows(xn_refs, i, n_ctx_tiles)], axis=0)
    ms = jnp.mean(x * x, axis=-1, keepdims=True)
    h = x * lax.rsqrt(ms + EPS) * g_ref[...]
    h = h * (1.0 + sc_ref[0]) + sh_ref[0]
    hb_ext = h.astype(BF16)
    hb = h[HALO:HALO + ROW_TILE].astype(BF16)
    pad = CONV_TAPS // 2
    lo, hi = HALO, HALO + ROW_TILE
    step = 2 * HEAD_DIM

    def project_qkv(t, c0):
        cols = slice((2 + t) * d + c0, (2 + t) * d + c0 + step)
        proj = jnp.dot(hb_ext, w_ref[0, :, cols], preferred_element_type=F32)
        ext_ref[t, 0:lo, c0:c0 + step] = proj[0:lo] * has_prev
        ext_ref[t, lo:hi, c0:c0 + step] = proj[lo:hi]
        ext_ref[t, hi:hi + HALO, c0:c0 + step] = proj[hi:hi + HALO] * has_next
        return proj[hi + HALO - 1:hi + HALO, 0:HEAD_DIM]

    def conv_head(t, hd, o_ref, l2, scale, anchor):
        lanes = slice(hd * HEAD_DIM, (hd + 1) * HEAD_DIM)
        acc = None
        for tap in range(CONV_TAPS):
            w_row = cw_ref[tap:tap + 1, t * d + hd * HEAD_DIM:t * d + (hd + 1) * HEAD_DIM]
            if tap == 0 and anchor is not None:
                w_row = w_row + jnp.where(i < 0, anchor, 0.0)
            start = lo - pad + tap
            term = ext_ref[t, start:start + ROW_TILE, lanes] * w_row
            acc = term if acc is None else acc + term
        y = _silu(acc)
        if l2:
            ss = jnp.sum(y * y, axis=-1, keepdims=True)
            y = y * (lax.rsqrt(ss + EPS) * scale)
        o_ref[:, lanes] = y.astype(BF16)

    passes = [functools.partial(project_qkv, t, c0) for t in range(3) for c0 in range(0, d, step)]

    def project_other(n, o_ref, c0):
        res = jnp.dot(hb, w_ref[0, :, n * d + c0:n * d + c0 + step], preferred_element_type=F32)
        o_ref[:, c0:c0 + step] = res.astype(BF16)
        return res[ROW_TILE - 1:ROW_TILE, 0:HEAD_DIM]

    passes += [functools.partial(project_other, n, o_ref, c0)
               for n, o_ref in ((0, u_ref), (1, v_ref), (5, z_ref), (6, ga_ref), (7, gb_ref))
               for c0 in range(0, d, step)]
    n_front = d // step
    for p in passes[:n_front]:
        p()
    rest = passes[n_front:]
    convs = [(t, hd) for t in range(3) for hd in range(N_HEADS)]
    n_rest = len(rest)
    conv_args = ((qo_ref, True, HEAD_DIM ** -0.5), (ko_ref, True, 1.0), (vo_ref, False, 1.0))
    anchor = None
    for j, (t, hd) in enumerate(convs):
        for _ in range((j + 1) * n_rest // len(convs) - j * n_rest // len(convs)):
            anchor = rest.pop(0)()
        conv_head(t, hd, *conv_args[t], anchor)
    for p in rest:
        p()
    ab = jnp.dot(hb, w_ref[0, :, 8 * d:8 * d + LANES], preferred_element_type=F32)
    lane = lax.broadcasted_iota(jnp.int32, ab.shape, 1)
    is_decay = ((lane // N_HEADS) % 2) == 0
    g = -jnp.exp(alog_ref[...]) * _softplus(ab + dtb_ref[...])
    gate = jnp.where(is_decay, g, _sigmoid(ab))
    gate = jnp.where(lane < 4 * N_HEADS, gate, 0.0)
    gate_ref[0] = gate
    gate_ref[1] = pltpu.roll(gate, LANES - 2 * N_HEADS, axis=1)


def _in_call(xs, mods, layer, norm_g, w_all, conv_w, alog_row, dtb_row, mod_row, n_ctx_tiles, ctx_tiles, lat_tiles):
    d = D_MODEL
    rows = sum(a.shape[0] for a in xs)
    nt = rows // ROW_TILE
    sh_idx = (layer * 6 + 0) * COND_ROWS
    sc_idx = (layer * 6 + 1) * COND_ROWS
    row_spec = pl.BlockSpec((ROW_TILE, d), lambda i: (i, 0))
    small = pl.BlockSpec((1, LANES), lambda i: (0, 0))
    return pl.pallas_call(
        functools.partial(_in_kernel, len(xs), n_ctx_tiles, ctx_tiles, lat_tiles),
        grid=(nt,),
        in_specs=(_row_specs(xs, n_ctx_tiles, 0, d) + _halo_specs(xs, n_ctx_tiles, d, False)
                  + _halo_specs(xs, n_ctx_tiles, d, True) + [
            pl.BlockSpec((1, 1, d), lambda i: (sh_idx + mod_row(i), 0, 0)),
            pl.BlockSpec((1, 1, d), lambda i: (sc_idx + mod_row(i), 0, 0)),
            pl.BlockSpec((1, d), lambda i: (0, 0)),
            pl.BlockSpec((1, d, 8 * d + LANES), lambda i: (layer, 0, 0), pipeline_mode=pl.Buffered(1)),
            pl.BlockSpec((8, 3 * d), lambda i: (0, 0)),
            small, small,
        ]),
        out_specs=[row_spec] * 8 + [pl.BlockSpec((2, ROW_TILE, LANES), lambda i: (0, i, 0))],
        out_shape=[jax.ShapeDtypeStruct((rows, d), BF16)] * 8 + [jax.ShapeDtypeStruct((2, rows, LANES), F32)],
        scratch_shapes=[pltpu.VMEM((3, ROW_TILE + 2 * HALO, d), F32)],
        compiler_params=_cparams(("arbitrary",)),
        name="norm_in_proj_conv",
    )(*xs, *xs, *xs, mods, mods, norm_g, w_all, conv_w, alog_row, dtb_row)


def _gdn_kernel(qf_ref, kf_ref, vf_ref, gf_ref, qb_ref, kb_ref, vb_ref, gb_ref, of_ref, ob_ref, s_ref):
    @pl.when(pl.program_id(1) == 0)
    def _():
        s_ref[...] = jnp.zeros_like(s_ref)

    for n in range(GDN_BLOCK):
        rows_f = slice(n * GDN_CHUNK, (n + 1) * GDN_CHUNK)
        rows_b = slice((GDN_BLOCK - 1 - n) * GDN_CHUNK, (GDN_BLOCK - n) * GDN_CHUNK)
        _gdn_chunk(((True, rows_f, qf_ref, kf_ref, vf_ref, gf_ref, of_ref),
                    (False, rows_b, qb_ref, kb_ref, vb_ref, gb_ref, ob_ref)), s_ref)


def _gdn_chunk(scans, s_ref):
    c = GDN_CHUNK
    row = lax.broadcasted_iota(jnp.int32, (c, c), 0)
    col = lax.broadcasted_iota(jnp.int32, (c, c), 1)
    eye = (row == col).astype(F32)
    nt_dims = (((1,), (1,)), ((), ()))
    lanes = [slice(h * HEAD_DIM, (h + 1) * HEAD_DIM) for h in range(N_HEADS)]

    q, k, v, gc_col, gc_row, gtot_row, beta_row, incl, strict, dst = [], [], [], [], [], [], [], [], [], []
    for d_idx, (fwd, rows, q_ref, k_ref, v_ref, gate_ref, o_ref) in enumerate(scans):
        incl_d = (row >= col) if fwd else (row <= col)
        strict_d = (row > col) if fwd else (row < col)
        gate = gate_ref[0, rows, :]
        gc_all = jnp.dot(incl_d.astype(F32), gate, precision=HIGHEST, preferred_element_type=F32)
        gtot_all = jnp.dot(jnp.ones((c, c), F32), gate, precision=HIGHEST, preferred_element_type=F32)
        gate_t = gate.T
        gc_t = gc_all.T
        gtot_t = gtot_all.T
        for h in range(N_HEADS):
            q.append(q_ref[rows, lanes[h]])
            k.append(k_ref[rows, lanes[h]])
            v.append(v_ref[rows, lanes[h]])
            gc_col.append(jnp.broadcast_to(gc_all[:, h:h + 1], (c, c)))
            gc_row.append(gc_t[h:h + 1, :])
            gtot_row.append(gtot_t[h:h + 1, :])
            beta_row.append(gate_t[N_HEADS + h:N_HEADS + h + 1, :])
            incl.append(incl_d)
            strict.append(strict_d)
            dst.append((o_ref, rows, lanes[h], d_idx, h))
    chains = range(len(q))

    decay = [jnp.exp(jnp.where(incl[i], gc_col[i] - gc_row[i], -jnp.inf)) * beta_row[i] for i in chains]
    kq = [lax.dot_general(jnp.concatenate([k[i], q[i]], axis=0), k[i], nt_dims, preferred_element_type=F32)
          for i in chains]
    n_mat = [jnp.where(strict[i], kq[i][:c] * decay[i], 0.0) for i in chains]
    a_mat = [(kq[i][c:] * decay[i]).astype(BF16) for i in chains]
    a_b = [(eye + n_mat[i]).astype(BF16) for i in chains]
    t_inv = [eye - n_mat[i] for i in chains]
    for _ in range(6):
        tb = [t_inv[i].astype(BF16) for i in chains]
        err = [(eye - jnp.dot(a_b[i], tb[i], preferred_element_type=F32)).astype(BF16) for i in chains]
        t_inv = [t_inv[i] + jnp.dot(tb[i], err[i], preferred_element_type=F32) for i in chains]
    u_val = [jnp.dot(t_inv[i].astype(BF16), v[i], preferred_element_type=F32) for i in chains]
    w_key = [jnp.dot((t_inv[i] * jnp.exp(gc_row[i])).astype(BF16), k[i], preferred_element_type=F32)
             for i in chains]

    s_old = [s_ref[dst[i][3], dst[i][4]] for i in chains]
    sb = [s_old[i].astype(BF16) for i in chains]
    wq_s = [jnp.dot(jnp.concatenate([w_key[i].astype(BF16), q[i]], axis=0), sb[i], preferred_element_type=F32)
            for i in chains]
    v_new = [(u_val[i] - wq_s[i][:c]).astype(BF16) for i in chains]
    kd_t = [(k[i].astype(F32).T * (jnp.exp(gtot_row[i] - gc_row[i]) * beta_row[i])).astype(BF16) for i in chains]
    ak_v = [jnp.dot(jnp.concatenate([a_mat[i], kd_t[i]], axis=0), v_new[i], preferred_element_type=F32)
            for i in chains]
    for i in chains:
        o_ref, rows, ln, d_idx, h = dst[i]
        s_ref[d_idx, h] = s_old[i] * jnp.exp(gtot_row[i]) + ak_v[i][c:]
        o_ref[rows, ln] = (jnp.exp(gc_col[i]) * wq_s[i][c:] + ak_v[i][:c]).astype(BF16)


def _gdn_call(q, k, v, gates, batch, ctx_len, seq_len):
    rows, d = q.shape
    c = GDN_BLOCK * GDN_CHUNK
    assert ctx_len % c == 0 and seq_len % c == 0
    n_ctx = ctx_len // c
    n_lat = seq_len // c
    n_steps = n_ctx + n_lat

    def chunk(b, fwd, s):
        pos = s if fwd else jnp.where(s < n_ctx, n_ctx - 1 - s, n_ctx + n_steps - 1 - s)
        return jnp.where(pos < n_ctx, b * n_ctx + pos, batch * n_ctx + b * n_lat + pos - n_ctx)

    def specs(fwd):
        blk = pl.BlockSpec((c, d), lambda b, s: (chunk(b, fwd, s), 0))
        return [blk, blk, blk, pl.BlockSpec((1, c, LANES), lambda b, s: (0 if fwd else 1, chunk(b, fwd, s), 0))]

    return pl.pallas_call(
        _gdn_kernel,
        grid=(batch, n_steps),
        in_specs=specs(True) + specs(False),
        out_specs=[pl.BlockSpec((c, d), lambda b, s: (chunk(b, True, s), 0)),
                   pl.BlockSpec((c, d), lambda b, s: (chunk(b, False, s), 0))],
        out_shape=[jax.ShapeDtypeStruct((rows, d), BF16)] * 2,
        scratch_shapes=[pltpu.VMEM((2, N_HEADS, HEAD_DIM, HEAD_DIM), F32)],
        compiler_params=_cparams(("arbitrary", "arbitrary")),
        name="gated_delta",
    )(q, k, v, gates, q, k, v, gates)


def _mix_kernel(n_src, n_ctx_tiles, tile_off, *refs):
    x_refs = refs[:n_src]
    (u_ref, v_ref, of_ref, ob_ref, z_ref, ga_ref, gb_ref,
     g1_ref, sh2_ref, sc2_ref, n2g_ref, lng_ref, ws_ref, bs_ref, bng_ref,
     wpa_ref, wpb_ref, wout_ref, wr_ref, br_ref,
     xo_ref, h2_ref, comb_ref, cnt_ref, sa_ref, sb_ref) = refs[n_src:]
    d = D_MODEL
    ug = _gelu_tanh(u_ref[...].astype(F32))
    vg = _gelu_tanh(v_ref[...].astype(F32))
    mu = jnp.mean(vg, axis=-1, keepdims=True)
    vc = vg - mu
    var = jnp.mean(vc * vc, axis=-1, keepdims=True)
    vn = (vc * lax.rsqrt(var + EPS) * lng_ref[...]).astype(BF16)
    for ch in range(MIX_TILE // SGU_CHUNK):
        rws = slice(ch * SGU_CHUNK, (ch + 1) * SGU_CHUNK)
        for g in range(SGU_GROUPS):
            lanes = slice(g * SGU_GROUP_DIM, (g + 1) * SGU_GROUP_DIM)
            mixed = jnp.dot(ws_ref[g], vn[rws, lanes], preferred_element_type=F32) + bs_ref[:, lanes]
            sa_ref[rws, lanes] = (ug[rws, lanes] * mixed).astype(BF16)
    y_a = jnp.dot(sa_ref[...], wpa_ref[...], preferred_element_type=F32)
    for h in range(N_HEADS):
        lanes = slice(h * HEAD_DIM, (h + 1) * HEAD_DIM)
        o = of_ref[:, lanes].astype(F32) + ob_ref[:, lanes].astype(F32)
        ms = jnp.mean(o * o, axis=-1, keepdims=True)
        o = o * lax.rsqrt(ms + EPS) * bng_ref[...]
        sb_ref[:, lanes] = (o * _silu(z_ref[:, lanes].astype(F32))).astype(BF16)
    y_b = jnp.dot(sb_ref[...], wpb_ref[...], preferred_element_type=F32)
    merged = _sigmoid(ga_ref[...].astype(F32)) * y_a + _sigmoid(gb_ref[...].astype(F32)) * y_b
    y = jnp.dot(merged.astype(BF16), wout_ref[...], preferred_element_type=F32)
    xn = _load_rows(x_refs, pl.program_id(0) + tile_off, n_ctx_tiles) + g1_ref[0] * y
    xo_ref[...] = xn
    ms = jnp.mean(xn * xn, axis=-1, keepdims=True)
    h2 = xn * lax.rsqrt(ms + EPS) * n2g_ref[...]
    h2 = h2 * (1.0 + sc2_ref[0]) + sh2_ref[0]
    h2_ref[...] = h2.astype(BF16)
    logits = lax.dot_general(wr_ref[...], h2, (((1,), (1,)), ((), ())),
                             precision=HIGHEST, preferred_element_type=F32)
    scores = _sigmoid(logits)
    sel = scores + br_ref[...]
    srow = [sel[e:e + 1, :] for e in range(N_EXPERTS)]
    grp = []
    for g in range(N_EXPERT_GROUPS):
        m = srow[4 * g:4 * g + 4]
        best2 = None
        for a in range(4):
            for b in range(a + 1, 4):
                pair = m[a] + m[b]
                best2 = pair if best2 is None else jnp.maximum(best2, pair)
        grp.append(best2)
    best_val = grp[0]
    best_idx = jnp.zeros_like(best_val, dtype=jnp.int32)
    for g in range(1, N_EXPERT_GROUPS):
        better = grp[g] > best_val
        best_val = jnp.where(better, grp[g], best_val)
        best_idx = jnp.where(better, g, best_idx)
    picked = []
    for e in range(N_EXPERTS):
        g = e // EXPERTS_PER_GROUP
        rank = jnp.zeros_like(best_idx)
        for o_e in range(4 * g, 4 * g + 4):
            if o_e == e:
                continue
            ahead = (srow[o_e] > srow[e]) if o_e > e else (srow[o_e] >= srow[e])
            rank = rank + ahead.astype(jnp.int32)
        chosen = jnp.logical_and(best_idx == g, rank < 2)
        picked.append(jnp.where(chosen, scores[e:e + 1, :], 0.0))
    total = picked[0]
    for e in range(1, N_EXPERTS):
        total = total + picked[e]
    inv = 1.0 / total
    for e in range(N_EXPERTS):
        w_e = picked[e] * inv
        comb_ref[e:e + 1, :] = w_e
        n_e = jnp.sum(jnp.where(w_e > 0.0, 1.0, 0.0), axis=-1, keepdims=True)
        cnt_ref[0, e:e + 1, :] = jnp.broadcast_to(n_e, (1, LANES))


def _mix_call(xs, u, v, o_f, o_b, z, ga, gb, mods, layer, norm2_g, ln_g, ws, bs, bng,
              wpa, wpb, wout, wr_t, br, mod_row, n_ctx_tiles, tile_off):
    d = D_MODEL
    nt = sum(a.shape[0] for a in xs) // MIX_TILE - tile_off
    rows = nt * MIX_TILE
    per = MIX_TILE // ROW_TILE
    g1_idx = (layer * 6 + 2) * COND_ROWS
    sh2_idx = (layer * 6 + 3) * COND_ROWS
    sc2_idx = (layer * 6 + 4) * COND_ROWS
    row_spec = pl.BlockSpec((MIX_TILE, d), lambda i: (i + tile_off, 0))
    out_spec = pl.BlockSpec((MIX_TILE, d), lambda i: (i, 0))
    vec = pl.BlockSpec((1, d), lambda i: (0, 0))
    wspec = pl.BlockSpec((d, d), lambda i: (0, 0))

    def mod(idx):
        return pl.BlockSpec((1, 1, d), lambda i: (idx + mod_row((i + tile_off) * per), 0, 0))

    return pl.pallas_call(
        functools.partial(_mix_kernel, len(xs), n_ctx_tiles, tile_off),
        grid=(nt,),
        in_specs=_row_specs(xs, n_ctx_tiles, tile_off, d, MIX_TILE) + [
            row_spec, row_spec,
            row_spec, row_spec,
            row_spec, row_spec, row_spec,
            mod(g1_idx), mod(sh2_idx), mod(sc2_idx),
            vec, vec,
            pl.BlockSpec((SGU_GROUPS, SGU_CHUNK, SGU_CHUNK), lambda i: (0, 0, 0)),
            pl.BlockSpec((SGU_CHUNK, d), lambda i: (0, 0)),
            pl.BlockSpec((1, HEAD_DIM), lambda i: (0, 0)),
            wspec, wspec, wspec,
            pl.BlockSpec((N_EXPERTS, d), lambda i: (0, 0)),
            pl.BlockSpec((N_EXPERTS, 1), lambda i: (0, 0)),
        ],
        out_specs=[out_spec, out_spec, pl.BlockSpec((N_EXPERTS, MIX_TILE), lambda i: (0, i)),
                   pl.BlockSpec((1, N_EXPERTS, LANES), lambda i: (i, 0, 0))],
        out_shape=[jax.ShapeDtypeStruct((rows, d), F32), jax.ShapeDtypeStruct((rows, d), BF16),
                   jax.ShapeDtypeStruct((N_EXPERTS, rows), F32),
                   jax.ShapeDtypeStruct((nt, N_EXPERTS, LANES), F32)],
        scratch_shapes=[pltpu.VMEM((MIX_TILE, d), BF16), pltpu.VMEM((MIX_TILE, d), BF16)],
        compiler_params=_cparams(("arbitrary",)),
        name="mix_merge_router",
    )(*xs, u, v, o_f, o_b, z, ga, gb, mods, mods, mods, norm2_g, ln_g, ws, bs, bng,
      wpa, wpb, wout, wr_t, br)


def _piece_copies(lo, go, n16, local_ref, global_ref, sem, to_global, wait, repeat_local=False):
    done = jnp.int32(0)
    for bit in SEG_BITS:
        size = bit * SEG_ALIGN
        present = (n16 & bit) != 0
        l_start = lo if repeat_local else lo + done
        l_at = local_ref.at[pl.ds(pl.multiple_of(l_start, SEG_ALIGN), size)]
        g_at = global_ref.at[pl.ds(pl.multiple_of(go + done, SEG_ALIGN), size)]
        cp = pltpu.make_async_copy(l_at, g_at, sem) if to_global else pltpu.make_async_copy(g_at, l_at, sem)

        @pl.when(present)
        def _():
            if wait:
                cp.wait()
            else:
                cp.start()

        done = done + jnp.where(present, size, 0)


def _segment_copies(t, off_s, gs_s, l16_s, local_ref, global_ref, sem, to_global, wait):
    for e in range(N_EXPERTS):
        idx = t * N_EXPERTS + e
        _piece_copies(off_s[idx], gs_s[idx], l16_s[idx], local_ref, global_ref, sem, to_global, wait)


def _sort_kernel(off_s, gs_s, l16_s, tail_s, tail16_s, h_ref, comb_ref, offcol_ref, slots_ref, xs_ref,
                 hs_ref, zero_ref, sem):
    t = pl.program_id(0)
    slot = t % 2
    other = 1 - slot
    ts = SORT_TILE
    comb = comb_ref[...]
    asg = comb > 0.0
    row = lax.broadcasted_iota(jnp.int32, (ts, ts), 0)
    col = lax.broadcasted_iota(jnp.int32, (ts, ts), 1)
    before = jnp.where(row < col, 1.0, 0.0).astype(BF16)
    rank = jnp.dot(jnp.where(asg, 1.0, 0.0).astype(BF16), before, preferred_element_type=F32)
    pos = offcol_ref[0][:, 0:1] + rank
    p_lo = jnp.min(jnp.where(asg, pos, 1e9), axis=0, keepdims=True)
    p_hi = jnp.max(jnp.where(asg, pos, -1.0), axis=0, keepdims=True)
    w_lo = jnp.sum(jnp.where(jnp.logical_and(asg, pos == p_lo), comb, 0.0), axis=0, keepdims=True)
    w_hi = jnp.sum(jnp.where(jnp.logical_and(asg, pos == p_hi), comb, 0.0), axis=0, keepdims=True)
    w_hi = jnp.where(p_hi > p_lo, w_hi, 0.0)
    dest = lax.broadcasted_iota(jnp.int32, (SORT_CAP, ts), 0).astype(F32)
    onehot = jnp.where(jnp.logical_or(dest == p_lo, dest == p_hi), 1.0, 0.0).astype(BF16)
    hs_ref[slot] = jnp.dot(onehot, h_ref[...], preferred_element_type=F32).astype(BF16)
    slot_rows = jnp.concatenate([p_lo, p_hi, w_lo, w_hi, jnp.zeros((124, ts), F32)], axis=0)
    slots_ref[...] = slot_rows.T
    _segment_copies(t, off_s, gs_s, l16_s, hs_ref.at[slot], xs_ref, sem.at[slot], to_global=True, wait=False)

    @pl.when(t > 0)
    def _():
        _segment_copies(t - 1, off_s, gs_s, l16_s, hs_ref.at[other], xs_ref, sem.at[other], to_global=True, wait=True)

    @pl.when(t == pl.num_programs(0) - 1)
    def _():
        _segment_copies(t, off_s, gs_s, l16_s, hs_ref.at[slot], xs_ref, sem.at[slot], to_global=True, wait=True)
        zero_ref[...] = jnp.zeros_like(zero_ref)
        n_blocks = xs_ref.shape[0] // MOE_BLOCK

        def spare_block(b):
            rows = pl.ds(pl.multiple_of(b * MOE_BLOCK, MOE_BLOCK), MOE_BLOCK)
            return pltpu.make_async_copy(zero_ref, xs_ref.at[rows], sem.at[2])

        for wait in (False, True):
            for e in range(N_EXPERTS):
                _piece_copies(0, tail_s[e], tail16_s[e], zero_ref, xs_ref, sem.at[2], to_global=True, wait=wait,
                              repeat_local=True)

            @pl.loop(tail_s[N_EXPERTS], n_blocks)
            def _(b):
                if wait:
                    spare_block(b).wait()
                else:
                    spare_block(b).start()


def _sort_call(h2, comb_t, tables, tails, offcol, n_rows_sorted):
    rows, d = h2.shape
    nt = rows // SORT_TILE
    grid_spec = pltpu.PrefetchScalarGridSpec(
        num_scalar_prefetch=5,
        grid=(nt,),
        in_specs=[
            pl.BlockSpec((SORT_TILE, d), lambda i, *_: (i, 0)),
            pl.BlockSpec((N_EXPERTS, SORT_TILE), lambda i, *_: (0, i)),
            pl.BlockSpec((1, N_EXPERTS, LANES), lambda i, *_: (i, 0, 0)),
        ],
        out_specs=[
            pl.BlockSpec((SORT_TILE, LANES), lambda i, *_: (i, 0)),
            pl.BlockSpec(memory_space=pl.ANY),
        ],
        scratch_shapes=[pltpu.VMEM((2, SORT_CAP, d), BF16), pltpu.VMEM((MOE_BLOCK, d), BF16),
                        pltpu.SemaphoreType.DMA((3,))],
    )
    slots, xs = pl.pallas_call(
        _sort_kernel,
        grid_spec=grid_spec,
        out_shape=[jax.ShapeDtypeStruct((rows, LANES), F32), jax.ShapeDtypeStruct((n_rows_sorted, d), BF16)],
        compiler_params=_cparams(("arbitrary",)),
        name="moe_sort",
    )(*tables, *tails, h2, comb_t, offcol)
    return slots, xs


def _experts_kernel(be_s, bv_s, x_ref, w1_ref, w3_ref, w2_ref, y_ref):
    b = pl.program_id(0)

    @pl.when(bv_s[b] != 0)
    def _():
        x = x_ref[...]
        a = jnp.dot(x, w1_ref[0, 0].astype(BF16), preferred_element_type=F32)
        g = jnp.dot(x, w3_ref[0, 0].astype(BF16), preferred_element_type=F32)
        hid = (_silu(a) * g).astype(BF16)
        y_ref[...] = jnp.dot(hid, w2_ref[0, 0].astype(BF16), preferred_element_type=F32).astype(BF16)

    @pl.when(bv_s[b] == 0)
    def _():
        y_ref[...] = jnp.zeros_like(y_ref)


def _experts_call(xs, blk_expert, blk_valid, w1, w3, w2, layer):
    rows, d = xs.shape
    nb = rows // MOE_BLOCK
    grid_spec = pltpu.PrefetchScalarGridSpec(
        num_scalar_prefetch=2,
        grid=(nb,),
        in_specs=[
            pl.BlockSpec((MOE_BLOCK, d), lambda b, be, bv: (b, 0)),
            pl.BlockSpec((1, 1, d, D_EXPERT), lambda b, be, bv: (layer, be[b], 0, 0)),
            pl.BlockSpec((1, 1, d, D_EXPERT), lambda b, be, bv: (layer, be[b], 0, 0)),
            pl.BlockSpec((1, 1, D_EXPERT, d), lambda b, be, bv: (layer, be[b], 0, 0)),
        ],
        out_specs=pl.BlockSpec((MOE_BLOCK, d), lambda b, be, bv: (b, 0)),
    )
    return pl.pallas_call(
        _experts_kernel,
        grid_spec=grid_spec,
        out_shape=jax.ShapeDtypeStruct((rows, d), BF16),
        compiler_params=_cparams(("arbitrary",)),
        name="moe_experts",
    )(blk_expert, blk_valid, xs, w1, w3, w2)


def _unsort_kernel(final, off_s, gs_s, l16_s, ys_ref, slots_ref, x_ref, fg_ref, g2a_ref, g2b_ref,
                   o_ref, yt_ref, sem):
    t = pl.program_id(0)
    ts = SORT_TILE
    slot = t % 2
    other = 1 - slot

    def fetch(tile, buf):
        yt_ref[buf] = jnp.zeros(yt_ref.shape[1:], yt_ref.dtype)
        _segment_copies(tile, off_s, gs_s, l16_s, yt_ref.at[buf], ys_ref, sem.at[buf], to_global=False, wait=False)

    @pl.when(t == 0)
    def _():
        fetch(t, slot)

    @pl.when(t + 1 < pl.num_programs(0))
    def _():
        fetch(t + 1, other)

    _segment_copies(t, off_s, gs_s, l16_s, yt_ref.at[slot], ys_ref, sem.at[slot], to_global=False, wait=True)
    slots = slots_ref[...]
    src = lax.broadcasted_iota(jnp.int32, (ts, SORT_CAP), 1).astype(F32)
    weights = jnp.where(src == slots[:, 0:1], slots[:, 2:3], 0.0) + jnp.where(src == slots[:, 1:2], slots[:, 3:4], 0.0)
    y = jnp.dot(weights.astype(BF16), yt_ref[slot], preferred_element_type=F32)
    for s, g2_ref in enumerate((g2a_ref, g2b_ref)):
        rws = slice(s * ROW_TILE, (s + 1) * ROW_TILE)
        xn = x_ref[rws, :] + g2_ref[0] * y[rws, :]
        if final:
            ms = jnp.mean(xn * xn, axis=-1, keepdims=True)
            xn = xn * lax.rsqrt(ms + EPS) * fg_ref[...]
        o_ref[rws, :] = xn


def _unsort_call(ys, slots, x, tables, final_g, mods, layer, mod_row, tile_off, final):
    rows, d = x.shape
    nt = rows // SORT_TILE
    sub = SORT_TILE // ROW_TILE
    g2_idx = (layer * 6 + 5) * COND_ROWS
    row_spec = pl.BlockSpec((SORT_TILE, d), lambda i, *_: (i, 0))

    def mod(s):
        return pl.BlockSpec((1, 1, d), lambda i, *_: (g2_idx + mod_row(i * sub + s + tile_off), 0, 0))

    grid_spec = pltpu.PrefetchScalarGridSpec(
        num_scalar_prefetch=3,
        grid=(nt,),
        in_specs=[
            pl.BlockSpec(memory_space=pl.ANY),
            pl.BlockSpec((SORT_TILE, LANES), lambda i, *_: (i, 0)),
            row_spec,
            pl.BlockSpec((1, d), lambda i, *_: (0, 0)),
            mod(0), mod(1),
        ],
        out_specs=row_spec,
        scratch_shapes=[pltpu.VMEM((2, SORT_CAP, d), BF16), pltpu.SemaphoreType.DMA((2,))],
    )
    return pl.pallas_call(
        functools.partial(_unsort_kernel, final),
        grid_spec=grid_spec,
        out_shape=jax.ShapeDtypeStruct((rows, d), F32),
        compiler_params=_cparams(("arbitrary",)),
        name="moe_unsort_residual",
    )(*tables, ys, slots, x, final_g, mods, mods)


def _sum_before(a, axis):
    i = jnp.arange(a.shape[axis])
    mask = (i[None, :] < i[:, None]).astype(a.dtype)
    moved = jnp.moveaxis(a, axis, -1)
    return jnp.moveaxis(jnp.sum(moved[..., None, :] * mask, axis=-1), -1, axis)


def _moe_tables(cnt_blocks, n_blocks):
    cnt = cnt_blocks[:, :, 0].astype(jnp.int32)
    seg = (cnt + SEG_ALIGN - 1) // SEG_ALIGN * SEG_ALIGN
    off = _sum_before(seg, 1)
    blocks_e = (seg.sum(axis=0) + MOE_BLOCK - 1) // MOE_BLOCK
    first_block = _sum_before(blocks_e, 0)
    gstart = first_block[None, :] * MOE_BLOCK + _sum_before(seg, 0)
    blk = jnp.arange(n_blocks, dtype=jnp.int32)
    last_block = first_block + blocks_e
    blk_expert = jnp.minimum(jnp.sum((blk[:, None] >= last_block[None, :]).astype(jnp.int32), axis=1),
                             N_EXPERTS - 1)
    blk_valid = (blk < blocks_e.sum()).astype(jnp.int32)
    tables = (off.reshape(-1).astype(jnp.int32), gstart.reshape(-1).astype(jnp.int32),
              (seg // SEG_ALIGN).reshape(-1).astype(jnp.int32))
    offcol = jnp.broadcast_to(off.astype(F32)[:, :, None], off.shape + (LANES,))
    used = seg.sum(axis=0)
    tails = (jnp.concatenate([first_block * MOE_BLOCK + used, blocks_e.sum()[None]]).astype(jnp.int32),
             ((blocks_e * MOE_BLOCK - used) // SEG_ALIGN).astype(jnp.int32))
    return tables, tails, offcol, blk_expert.astype(jnp.int32), blk_valid


def _moe_call(h2, comb_t, cnt_blocks, x, w1, w3, w2, final_g, mods, layer, mod_row, tile_off, final):
    rows = x.shape[0]
    n_tiles = rows // SORT_TILE
    max_rows = 2 * rows + n_tiles * N_EXPERTS * (SEG_ALIGN - 1)
    n_blocks = max_rows // MOE_BLOCK + N_EXPERTS
    tables, tails, offcol, blk_expert, blk_valid = _moe_tables(cnt_blocks, n_blocks)
    slots, xs = _sort_call(h2, comb_t, tables, tails, offcol, n_blocks * MOE_BLOCK)
    ys = _experts_call(xs, blk_expert, blk_valid, w1, w3, w2, layer)
    return _unsort_call(ys, slots, x, tables, final_g, mods, layer, mod_row, tile_off, final)


def kernel(x, c, ctx, c_ctx, w_ada, b_ada, norm1_g, norm2_g, w_in, a_ln_g, a_w_s, a_b_s, b_conv_w, b_A_log, b_dt_bias, b_norm_g, w_proj_a, w_proj_b, w_out, w_router, b_router, w_e1, w_e3, w_e2, final_g):
    batch, seq_len, d = x.shape
    ctx_len = ctx.shape[1]
    n_layers = w_ada.shape[0]
    assert d == D_MODEL and batch + 1 <= COND_ROWS
    assert ctx_len % ROW_TILE == 0 and seq_len % ROW_TILE == 0
    assert (batch * ctx_len) % SORT_TILE == 0 and (batch * seq_len) % SORT_TILE == 0
    assert MIX_TILE == SORT_TILE and (batch * ctx_len) % MIX_TILE == 0 and seq_len % MIX_TILE == 0
    ctx_tiles = ctx_len // ROW_TILE
    lat_tiles = seq_len // ROW_TILE
    n_ctx_tiles = batch * ctx_tiles

    def mod_row(tile):
        return jnp.where(tile < n_ctx_tiles, batch, (tile - n_ctx_tiles) // lat_tiles)

    xs = (ctx.reshape(batch * ctx_len, d), x.reshape(batch * seq_len, d))
    cond = jnp.concatenate([c, c_ctx[None, :], jnp.zeros((COND_ROWS - batch - 1, d), F32)], axis=0)
    mods = _ada_call(cond, w_ada, b_ada).reshape(n_layers * 6 * COND_ROWS, 1, d)

    n_ab = 4 * N_HEADS
    ab_lo = 6 * d
    wr_t = w_router.T
    br = b_router.reshape(N_EXPERTS, 1)
    fg = final_g.reshape(1, d)
    w_bf = w_in.astype(BF16)
    w_all = jnp.concatenate([w_bf[..., :ab_lo], w_bf[..., ab_lo + n_ab:], w_bf[..., ab_lo:ab_lo + n_ab],
                             jnp.zeros((n_layers, d, LANES - n_ab), BF16)], axis=-1)
    for l in range(n_layers):
        conv_w = jnp.pad(b_conv_w[l], ((0, 8 - CONV_TAPS), (0, 0)))
        zeros8 = jnp.zeros((N_HEADS,), F32)
        alog_row = jnp.pad(jnp.concatenate([b_A_log[l, 0], zeros8, b_A_log[l, 1], zeros8]), (0, LANES - n_ab)).reshape(1, LANES)
        dtb_row = jnp.pad(jnp.concatenate([b_dt_bias[l, 0], zeros8, b_dt_bias[l, 1], zeros8]), (0, LANES - n_ab)).reshape(1, LANES)
        u, v, z, ga, gb, qn, kn, vs, gates = _in_call(
            xs, mods, l, norm1_g[l].reshape(1, d), w_all, conv_w, alog_row, dtb_row, mod_row,
            n_ctx_tiles, ctx_tiles, lat_tiles)
        o_f, o_b = _gdn_call(qn, kn, vs, gates, batch, ctx_len, seq_len)

        bs = jnp.repeat(a_b_s[l].T, SGU_CHUNK, axis=1)
        bng = b_norm_g[l].reshape(1, HEAD_DIM)
        last = l == n_layers - 1
        tile_off = n_ctx_tiles if last else 0
        x_new, h2, comb_t, cnt = _mix_call(
            xs, u, v, o_f, o_b, z, ga, gb, mods, l, norm2_g[l].reshape(1, d), a_ln_g[l].reshape(1, d),
            a_w_s[l].astype(BF16), bs, bng,
            w_proj_a[l].astype(BF16), w_proj_b[l].astype(BF16), w_out[l].astype(BF16), wr_t, br, mod_row,
            batch * ctx_len // MIX_TILE, tile_off * ROW_TILE // MIX_TILE)
        xs = (_moe_call(h2, comb_t, cnt, x_new, w_e1, w_e3, w_e2,
                        fg, mods, l, mod_row, tile_off, final=last),)
    return xs[0].reshape(batch, seq_len, d)
```
